```python
import jax, jax.numpy as jnp
from jax import lax
import numpy as np

D_MODEL = 2048
BATCH = 2
SEQ = 8192
DEPTH = 1
DEC_BATCH = 8
DEC_SEQ = 64
PAST_LEN = 2048

CHUNK = 64
MIX_WIDTH = D_MODEL
POOL_WIDTH = MIX_WIDTH // 2
POOL_WINDOWS = (2, 4, 8, 16)
POOL_GROUP = POOL_WIDTH // len(POOL_WINDOWS)
POOL_PAD = POOL_WINDOWS[-1] - 1
MLSTM_WIDTH = MIX_WIDTH - POOL_WIDTH
MLSTM_HEADS = 4
HEAD_DIM = MLSTM_WIDTH // MLSTM_HEADS
PROJ_WIDTH = POOL_WIDTH + 4 * MLSTM_WIDTH + 2 * MLSTM_HEADS
N_EXPERTS = 32
TOP_K = 4
D_FF = D_MODEL
SWIGLU_LIMIT = 7.0
SWIGLU_ALPHA = 1.702
EXPERT_BLOCK = 256
EPS = 1e-6

kernel_name = "hybrid_pool_mlstm_moe_stream_step"


def _rms(x):
    xf = x.astype(jnp.float32)
    return xf * lax.rsqrt(jnp.mean(xf * xf, axis=-1, keepdims=True) + EPS)


def _adaln(c, w_ada, b_ada):
    mod = jax.nn.silu(c.astype(jnp.float32)) @ w_ada + b_ada
    return jnp.split(mod, 6, axis=-1)


def _pool_mixer(u, pool_state, pos0, pool_w, pool_scale):
    B, S, C = u.shape
    uf = u.astype(jnp.float32)
    xp = jnp.concatenate([pool_state.astype(jnp.float32), uf], axis=1)
    cs = jnp.concatenate([jnp.zeros((B, 1, C), jnp.float32), jnp.cumsum(xp, axis=1)], axis=1)
    pos = pos0 + jnp.arange(S)
    base = POOL_PAD + 1
    means = []
    for g, w in enumerate(POOL_WINDOWS):
        sl = slice(g * POOL_GROUP, (g + 1) * POOL_GROUP)
        win = cs[:, base:base + S, sl] - cs[:, base - w:base - w + S, sl]
        cnt = jnp.minimum(pos + 1, w).astype(jnp.float32)
        means.append(win / cnt[None, :, None])
    pooled = (jnp.concatenate(means, axis=-1) - uf).reshape(B, S, len(POOL_WINDOWS), POOL_GROUP)
    mixed = jnp.einsum('bsgc,gcd->bsgd', pooled, pool_w).reshape(B, S, C) * pool_scale
    return mixed, xp[:, -POOL_PAD:]


def _mlstm_chunkwise(q, k, v, ig, lf, C0, n0, m0):
    B, S, H, Dh = q.shape
    L = min(CHUNK, S)
    N = S // L

    def to_chunks(t):
        t = t.reshape((B, N, L, H) + t.shape[3:])
        return jnp.moveaxis(t, (1, 2, 3), (0, 3, 2))

    causal = jnp.tril(jnp.ones((L, L), bool))

    def step(carry, inp):
        C, n, m = carry
        qc, kc, vc, ic, fc = inp
        F = jnp.cumsum(fc, axis=-1)
        a = ic - F
        m_t = F + jnp.maximum(m[..., None], lax.cummax(a, axis=2))
        logD = F[..., :, None] + a[..., None, :] - m_t[..., :, None]
        Dm = jnp.exp(jnp.where(causal, logD, -jnp.inf))
        Sc = jnp.einsum('bhtd,bhsd->bhts', qc, kc) * Dm
        inter = jnp.exp(F + m[..., None] - m_t)
        num = jnp.einsum('bhts,bhsd->bhtd', Sc, vc) + inter[..., None] * jnp.einsum('bhek,bhtk->bhte', C, qc)
        den = jnp.sum(Sc, axis=-1) + inter * jnp.einsum('bhk,bhtk->bht', n, qc)
        h = num / jnp.maximum(jnp.abs(den), jnp.exp(-m_t))[..., None]
        mL = m_t[..., -1]
        FL = F[..., -1]
        w_s = jnp.exp(FL[..., None] + a - mL[..., None])
        decay = jnp.exp(FL + m - mL)
        C_new = decay[..., None, None] * C + jnp.einsum('bhs,bhse,bhsk->bhek', w_s, vc, kc)
        n_new = decay[..., None] * n + jnp.einsum('bhs,bhsk->bhk', w_s, kc)
        return (C_new, n_new, mL), h

    (C, n, m), hs = lax.scan(step, (C0, n0, m0),
                             (to_chunks(q), to_chunks(k), to_chunks(v), to_chunks(ig), to_chunks(lf)))
    h = jnp.transpose(hs, (1, 0, 3, 2, 4)).reshape(B, S, H, Dh)
    return h, C, n, m


def _token_mix(h, pool_state, C0, n0, m0, pos0, w_in, b_gate, pool_w, pool_scale, mlstm_gain, w_out):
    B, S, _ = h.shape
    proj = (h @ w_in).astype(jnp.float32)
    u = proj[..., :POOL_WIDTH]
    o0 = POOL_WIDTH
    q = proj[..., o0:o0 + MLSTM_WIDTH].reshape(B, S, MLSTM_HEADS, HEAD_DIM)
    k = proj[..., o0 + MLSTM_WIDTH:o0 + 2 * MLSTM_WIDTH].reshape(B, S, MLSTM_HEADS, HEAD_DIM) * (HEAD_DIM ** -0.5)
    v = proj[..., o0 + 2 * MLSTM_WIDTH:o0 + 3 * MLSTM_WIDTH].reshape(B, S, MLSTM_HEADS, HEAD_DIM)
    og = proj[..., o0 + 3 * MLSTM_WIDTH:o0 + 4 * MLSTM_WIDTH]
    gates = proj[..., o0 + 4 * MLSTM_WIDTH:] + b_gate
    ig = gates[..., :MLSTM_HEADS]
    lf = jax.nn.log_sigmoid(gates[..., MLSTM_HEADS:])

    pool_out, new_pool = _pool_mixer(u, pool_state, pos0, pool_w, pool_scale)

    hm, C, n, m = _mlstm_chunkwise(q, k, v, ig, lf, C0.astype(jnp.float32), n0.astype(jnp.float32),
                                   m0.astype(jnp.float32))
    hm = hm * lax.rsqrt(jnp.mean(hm * hm, axis=-1, keepdims=True) + EPS) * mlstm_gain
    mlstm_out = jax.nn.sigmoid(og) * hm.reshape(B, S, MLSTM_WIDTH)

    out = jnp.concatenate([pool_out, mlstm_out], axis=-1) @ w_out
    return out, (new_pool, C, n, m)


def _moe(h, w_router, b_router, w_up, b_up, w_down, b_down):
    B, S, D = h.shape
    T = B * S
    hf = h.reshape(T, D)
    logits = (hf @ w_router).astype(jnp.float32) + b_router
    top_vals, top_idx = lax.top_k(logits, TOP_K)
    gates = jax.nn.softmax(top_vals, axis=-1)
    A = T * TOP_K
    e_flat = top_idx.reshape(A)
    order = jnp.argsort(e_flat)
    e_sorted = e_flat[order]
    tok_sorted = (order // TOP_K).astype(jnp.int32)
    g_sorted = gates.reshape(A)[order]
    counts = jnp.bincount(e_flat, length=N_EXPERTS)
    padded = (counts + EXPERT_BLOCK - 1) // EXPERT_BLOCK * EXPERT_BLOCK
    start = jnp.cumsum(counts) - counts
    pend = jnp.cumsum(padded)
    pstart = pend - padded
    dest = pstart[e_sorted] + jnp.arange(A) - start[e_sorted]
    n_blocks = -(-(A + N_EXPERTS * (EXPERT_BLOCK - 1)) // EXPERT_BLOCK)
    rows = n_blocks * EXPERT_BLOCK
    buf_tok = jnp.zeros((rows,), jnp.int32).at[dest].set(tok_sorted)
    block_expert = jnp.minimum(
        jnp.searchsorted(pend, jnp.arange(n_blocks) * EXPERT_BLOCK, side='right'), N_EXPERTS - 1)
    xs = hf[buf_tok].reshape(n_blocks, EXPERT_BLOCK, D)

    def expert_block(args):
        xb, e = args
        up = xb @ w_up[e] + b_up[e]
        g = jnp.minimum(up[:, :D_FF], SWIGLU_LIMIT)
        lin = jnp.clip(up[:, D_FF:], -SWIGLU_LIMIT, SWIGLU_LIMIT)
        act = g * jax.nn.sigmoid(SWIGLU_ALPHA * g) * (lin + 1.0)
        return act @ w_down[e] + b_down[e]

    ys = lax.map(expert_block, (xs, block_expert)).reshape(rows, D)[dest]
    out = jax.ops.segment_sum(ys * g_sorted[:, None], tok_sorted, num_segments=T)
    return out.reshape(B, S, D)


def _layer(x, c, pool_state, C0, n0, m0, pos0, w_ada, b_ada, w_in, b_gate, pool_w, pool_scale,
           mlstm_gain, w_out, w_router, b_router, w_up, b_up, w_down, b_down):
    sh1, sc1, g1, sh2, sc2, g2 = _adaln(c, w_ada, b_ada)
    h = _rms(x) * (1.0 + sc1[:, None]) + sh1[:, None]
    mix, states = _token_mix(h, pool_state, C0, n0, m0, pos0, w_in, b_gate, pool_w, pool_scale,
                             mlstm_gain, w_out)
    x = x + g1[:, None] * mix
    h = _rms(x) * (1.0 + sc2[:, None]) + sh2[:, None]
    x = x + g2[:, None] * _moe(h, w_router, b_router, w_up, b_up, w_down, b_down)
    return x, states


def setup_inputs(seed: int = 0) -> dict:
    key = jax.random.key(seed)
    ks = jax.random.split(key, 24)
    f32 = jnp.float32
    nrm = lambda k, s, sc: jax.random.normal(k, s, f32) * sc
    b_gate = jnp.concatenate([
        nrm(ks[12], (DEPTH, MLSTM_HEADS), 0.1),
        jnp.broadcast_to(jnp.linspace(3.0, 6.0, MLSTM_HEADS, dtype=f32), (DEPTH, MLSTM_HEADS))
        + nrm(ks[13], (DEPTH, MLSTM_HEADS), 0.1)], axis=-1)
    return {
        "x_prompt": nrm(ks[0], (BATCH, SEQ, D_MODEL), 1.0),
        "x_sample": nrm(ks[1], (DEC_BATCH, DEC_SEQ, D_MODEL), 1.0),
        "state_pool": nrm(ks[2], (DEPTH, DEC_BATCH, POOL_PAD, POOL_WIDTH), 1.0),
        "state_mlstm_C": nrm(ks[3], (DEPTH, DEC_BATCH, MLSTM_HEADS, HEAD_DIM, HEAD_DIM), 0.05),
        "state_mlstm_n": nrm(ks[4], (DEPTH, DEC_BATCH, MLSTM_HEADS, HEAD_DIM), 0.05),
        "state_mlstm_m": 1.0 + nrm(ks[5], (DEPTH, DEC_BATCH, MLSTM_HEADS), 0.5),
        "c_prompt": nrm(ks[6], (BATCH, D_MODEL), 1.0),
        "c_sample": nrm(ks[7], (DEC_BATCH, D_MODEL), 1.0),
        "w_ada": nrm(ks[8], (DEPTH, D_MODEL, 6 * D_MODEL), D_MODEL ** -0.5),
        "b_ada": nrm(ks[9], (DEPTH, 6 * D_MODEL), 0.02),
        "w_in": nrm(ks[10], (DEPTH, D_MODEL, PROJ_WIDTH), D_MODEL ** -0.5),
        "b_gate": b_gate,
        "pool_w": nrm(ks[11], (DEPTH, len(POOL_WINDOWS), POOL_GROUP, POOL_GROUP), POOL_GROUP ** -0.5),
        "pool_scale": 1.0 + nrm(ks[14], (DEPTH, POOL_WIDTH), 0.1),
        "mlstm_gain": 1.0 + nrm(ks[15], (DEPTH, MLSTM_HEADS, HEAD_DIM), 0.1),
        "w_out": nrm(ks[16], (DEPTH, MIX_WIDTH, D_MODEL), MIX_WIDTH ** -0.5),
        "w_router": nrm(ks[17], (DEPTH, D_MODEL, N_EXPERTS), D_MODEL ** -0.5),
        "b_router": nrm(ks[18], (DEPTH, N_EXPERTS), 0.01),
        "w_up": nrm(ks[19], (DEPTH, N_EXPERTS, D_MODEL, 2 * D_FF), D_MODEL ** -0.5),
        "b_up": nrm(ks[20], (DEPTH, N_EXPERTS, 2 * D_FF), 0.02),
        "w_down": nrm(ks[21], (DEPTH, N_EXPERTS, D_FF, D_MODEL), D_FF ** -0.5),
        "b_down": nrm(ks[22], (DEPTH, N_EXPERTS, D_MODEL), 0.02),
        "final_gain": 1.0 + nrm(ks[23], (D_MODEL,), 0.1),
    }


def reference(x_prompt, x_sample, state_pool, state_mlstm_C, state_mlstm_n, state_mlstm_m,
              c_prompt, c_sample, w_ada, b_ada, w_in, b_gate, pool_w, pool_scale, mlstm_gain,
              w_out, w_router, b_router, w_up, b_up, w_down, b_down, final_gain):
    f32 = jnp.float32
    Bp = x_prompt.shape[0]
    hp, hs = x_prompt, x_sample
    pp, Cp, np_, mp = [], [], [], []
    ps, Cs, ns, ms = [], [], [], []
    for l in range(DEPTH):
        shared = (w_ada[l], b_ada[l], w_in[l], b_gate[l], pool_w[l], pool_scale[l], mlstm_gain[l],
                  w_out[l], w_router[l], b_router[l], w_up[l], b_up[l], w_down[l], b_down[l])
        hp, st_p = _layer(hp, c_prompt,
                          jnp.zeros((Bp, POOL_PAD, POOL_WIDTH), f32),
                          jnp.zeros((Bp, MLSTM_HEADS, HEAD_DIM, HEAD_DIM), f32),
                          jnp.zeros((Bp, MLSTM_HEADS, HEAD_DIM), f32),
                          jnp.zeros((Bp, MLSTM_HEADS), f32), 0, *shared)
        hs, st_s = _layer(hs, c_sample, state_pool[l], state_mlstm_C[l], state_mlstm_n[l],
                          state_mlstm_m[l], PAST_LEN, *shared)
        pp.append(st_p[0]); Cp.append(st_p[1]); np_.append(st_p[2]); mp.append(st_p[3])
        ps.append(st_s[0]); Cs.append(st_s[1]); ns.append(st_s[2]); ms.append(st_s[3])
    y_prompt = (_rms(hp) * final_gain).astype(x_prompt.dtype)
    y_sample = (_rms(hs) * final_gain).astype(x_sample.dtype)
    return (y_prompt, y_sample,
            jnp.stack(pp), jnp.stack(Cp), jnp.stack(np_), jnp.stack(mp),
            jnp.stack(ps), jnp.stack(Cs), jnp.stack(ns), jnp.stack(ms))
```

```python
import functools

import jax
import jax.numpy as jnp
from jax import lax
from jax.experimental import pallas as pl
from jax.experimental.pallas import tpu as pltpu

F32 = jnp.float32
BF16 = jnp.bfloat16
I32 = jnp.int32
U32 = jnp.uint32

EPS = 1e-6
POOL_WINDOWS = (2, 4, 8, 16)
TOP_K = 4
SWIGLU_LIMIT = 7.0
SWIGLU_ALPHA = 1.702
PAST_LEN = 2048

LANES = 128
SUBLANES = 8
MOD_CHUNK = 64
TOK_TILE = 512
POOL_HALO = 16
MLSTM_CHUNK = 256
ADA_TILE = 1024
MOE_ROWS = 1024
MOE_SUB = 256
MOE_FF_TILE = 256
COMB_TILE = 512
VMEM_LIMIT = 56 * 1024 * 1024


def _cparams(sem, vmem=VMEM_LIMIT):
    return pltpu.CompilerParams(dimension_semantics=sem, vmem_limit_bytes=vmem)


def _resident(shape, index_map):
    return pl.BlockSpec(shape, index_map, pipeline_mode=pl.Buffered(1))


def _dot(a, b):
    return jnp.dot(a, b, preferred_element_type=F32)


def _dot_nt(a, b):
    return lax.dot_general(a, b, (((1,), (1,)), ((), ())), preferred_element_type=F32)


def _split3(x):
    hi = x.astype(BF16)
    r1 = x - hi.astype(F32)
    mid = r1.astype(BF16)
    lo = (r1 - mid.astype(F32)).astype(BF16)
    return hi, mid, lo


def _log_sigmoid(x):
    return jnp.minimum(x, 0.0) - jnp.log1p(jnp.exp(-jnp.abs(x)))


def _adaln_kernel(c_ref, w_ref, b_ref, o_ref):
    c = c_ref[...]
    s = (c * jax.nn.sigmoid(c)).astype(BF16)
    o_ref[...] = _dot(s, w_ref[...].astype(BF16)) + b_ref[...]


def _adaln(c_all, w_ada, b_ada):
    rows, d = c_all.shape
    n = w_ada.shape[1]
    return pl.pallas_call(
        _adaln_kernel,
        out_shape=jax.ShapeDtypeStruct((rows, n), F32),
        grid=(n // ADA_TILE,),
        in_specs=[pl.BlockSpec((rows, d), lambda j: (0, 0)),
                  pl.BlockSpec((d, ADA_TILE), lambda j: (0, j)),
                  pl.BlockSpec((1, ADA_TILE), lambda j: (0, j))],
        out_specs=pl.BlockSpec((rows, ADA_TILE), lambda j: (0, j)),
        compiler_params=_cparams(("parallel",)),
        name="adaln",
    )(c_all, w_ada, b_ada)


def _modulated_norm(x, sc_ref, sh_ref, h_scr):
    xn = x * lax.rsqrt(jnp.mean(x * x, axis=-1, keepdims=True) + EPS)
    for c in range(x.shape[0] // MOD_CHUNK):
        rows = slice(c * MOD_CHUNK, (c + 1) * MOD_CHUNK)
        h_scr[rows, :] = (xn[rows, :] * (1.0 + sc_ref[c:c + 1, :]) + sh_ref[c:c + 1, :]).astype(h_scr.dtype)


def _inproj_kernel(xp_ref, xs_ref, sc_ref, sh_ref, w_ref, wkt_ref, wg_ref, wgt_ref, bgc_ref, bgr_ref,
                   u_ref, q_ref, kt_ref, v_ref, og_ref, gc_ref, gr_ref, h_scr, *, n_p, pw, mw):
    i = pl.program_id(0)

    @pl.when(i < n_p)
    def _():
        _modulated_norm(xp_ref[...], sc_ref, sh_ref, h_scr)

    @pl.when(i >= n_p)
    def _():
        _modulated_norm(xs_ref[...], sc_ref, sh_ref, h_scr)

    h = h_scr[...]
    u_ref[...] = _dot(h, w_ref[:, 0:pw])
    q_ref[...] = _dot(h, w_ref[:, pw:pw + mw]).astype(BF16)
    v_ref[...] = _dot(h, w_ref[:, pw + mw:pw + 2 * mw]).astype(BF16)
    og_ref[...] = _dot(h, w_ref[:, pw + 2 * mw:pw + 3 * mw])
    kt_ref[...] = _dot_nt(wkt_ref[...], h).astype(BF16)
    gc_ref[...] = _dot(h, wg_ref[...]) + bgc_ref[...]
    gr_ref[...] = _dot_nt(wgt_ref[...], h) + bgr_ref[...]


def _inproj(xp, xs, sc1, sh1, w_main, w_kt, w_gc, w_gt, b_gc, b_gr, pw, mw):
    tp, d = xp.shape
    ts = xs.shape[0]
    t = tp + ts
    n_p = tp // TOK_TILE
    n_s = ts // TOK_TILE
    cpt = TOK_TILE // MOD_CHUNK
    ng = w_gt.shape[0]
    kern = functools.partial(_inproj_kernel, n_p=n_p, pw=pw, mw=mw)
    tok = lambda i: (i, 0)
    return pl.pallas_call(
        kern,
        out_shape=(jax.ShapeDtypeStruct((t, pw), F32),
                   jax.ShapeDtypeStruct((t, mw), BF16),
                   jax.ShapeDtypeStruct((mw, t), BF16),
                   jax.ShapeDtypeStruct((t, mw), BF16),
                   jax.ShapeDtypeStruct((t, mw), F32),
                   jax.ShapeDtypeStruct((t, LANES), F32),
                   jax.ShapeDtypeStruct((ng, t), F32)),
        grid=(n_p + n_s,),
        in_specs=[pl.BlockSpec((TOK_TILE, d), lambda i: (jnp.minimum(i, n_p - 1), 0)),
                  pl.BlockSpec((TOK_TILE, d), lambda i: (jnp.maximum(i - n_p, 0), 0)),
                  pl.BlockSpec((cpt, d), tok),
                  pl.BlockSpec((cpt, d), tok),
                  _resident(w_main.shape, lambda i: (0, 0)),
                  _resident(w_kt.shape, lambda i: (0, 0)),
                  _resident(w_gc.shape, lambda i: (0, 0)),
                  _resident(w_gt.shape, lambda i: (0, 0)),
                  _resident(b_gc.shape, lambda i: (0, 0)),
                  _resident(b_gr.shape, lambda i: (0, 0))],
        out_specs=(pl.BlockSpec((TOK_TILE, pw), tok),
                   pl.BlockSpec((TOK_TILE, mw), tok),
                   pl.BlockSpec((mw, TOK_TILE), lambda i: (0, i)),
                   pl.BlockSpec((TOK_TILE, mw), tok),
                   pl.BlockSpec((TOK_TILE, mw), tok),
                   pl.BlockSpec((TOK_TILE, LANES), tok),
                   pl.BlockSpec((ng, TOK_TILE), lambda i: (0, i))),
        scratch_shapes=[pltpu.VMEM((TOK_TILE, d), BF16)],
        compiler_params=_cparams(("parallel",)),
        name="inproj",
    )(xp, xs, sc1, sh1, w_main, w_kt, w_gc, w_gt, b_gc, b_gr)


def _pool_kernel(u_ref, st_ref, pw_ref, ps_ref, o_ref, xp_scr, *, tm, pos0, group):
    j = pl.program_id(1)

    @pl.when(j == 0)
    def _():
        xp_scr[0:POOL_HALO, :] = st_ref[...]

    @pl.when(j > 0)
    def _():
        xp_scr[0:POOL_HALO, :] = xp_scr[tm:tm + POOL_HALO, :]

    xp_scr[POOL_HALO:POOL_HALO + tm, :] = u_ref[...]
    pos = pos0 + j * tm + lax.broadcasted_iota(I32, (tm, group), 0)
    for g, w in enumerate(POOL_WINDOWS):
        cs = slice(g * group, (g + 1) * group)
        x = xp_scr[POOL_HALO:POOL_HALO + tm, cs]
        acc = x
        for s in range(1, w):
            acc = acc + xp_scr[POOL_HALO - s:POOL_HALO - s + tm, cs]
        cnt = jnp.minimum(pos + 1, w).astype(F32)
        pooled = acc / cnt - x
        mixed = _dot(pooled.astype(BF16), pw_ref[g]) * ps_ref[:, cs]
        o_ref[:, cs] = mixed.astype(o_ref.dtype)


def _pool(u, state16, pool_w, pool_scale, *, batch, seq, row0, tm, pos0):
    c = u.shape[1]
    group = c // len(POOL_WINDOWS)
    nt = seq // tm
    blk0 = row0 // tm
    kern = functools.partial(_pool_kernel, tm=tm, pos0=pos0, group=group)
    return pl.pallas_call(
        kern,
        out_shape=jax.ShapeDtypeStruct((batch * seq, c), BF16),
        grid=(batch, nt),
        in_specs=[pl.BlockSpec((tm, c), lambda b, j: (blk0 + b * nt + j, 0)),
                  pl.BlockSpec((None, POOL_HALO, c), lambda b, j: (b, 0, 0)),
                  _resident(pool_w.shape, lambda b, j: (0, 0, 0)),
                  _resident(pool_scale.shape, lambda b, j: (0, 0))],
        out_specs=pl.BlockSpec((tm, c), lambda b, j: (b * nt + j, 0)),
        scratch_shapes=[pltpu.VMEM((POOL_HALO + tm, c), F32)],
        compiler_params=_cparams(("parallel", "arbitrary")),
        name="pool",
    )(u, state16, pool_w, pool_scale)


def _mlstm_kernel(q_ref, kt_ref, v_ref, og_ref, gc_ref, gr_ref, gain_ref, c0_ref, m0_ref,
                  o_ref, c_ref, m_ref, *, chunk, heads, dh):
    ci = pl.program_id(1)
    L = chunk

    @pl.when(ci == 0)
    def _():
        c_ref[...] = c0_ref[...]
        m_ref[...] = m0_ref[...]

    gc = gc_ref[...]
    gr = gr_ref[...]
    lf_c = _log_sigmoid(gc)
    lf_r = _log_sigmoid(gr)
    row_i = lax.broadcasted_iota(I32, (L, L), 0)
    col_i = lax.broadcasted_iota(I32, (L, L), 1)
    causal = col_i <= row_i
    tri = causal.astype(BF16)
    tri_t = (row_i <= col_i).astype(BF16)
    f_c = sum(_dot(tri, p) for p in _split3(lf_c))
    f_r = sum(_dot(p, tri_t) for p in _split3(lf_r))
    one_col = (lax.broadcasted_iota(I32, (L, LANES), 1) == 0).astype(BF16)
    neg_inf = jnp.float32(-jnp.inf)

    for h in range(heads):
        hs = slice(h * dh, (h + 1) * dh)
        fc = f_c[:, heads + h:heads + h + 1]
        a_r = gr[h:h + 1, :] - f_r[heads + h:heads + h + 1, :]
        m_prev = m_ref[h]
        cm = jnp.max(jnp.where(causal, a_r, neg_inf), axis=1, keepdims=True)
        m_t = fc + jnp.maximum(m_prev, cm)
        dm = jnp.exp(jnp.where(causal, (fc - m_t) + a_r, neg_inf))
        qh = q_ref[:, hs]
        kth = kt_ref[hs, :]
        vh = v_ref[:, hs]
        sc = _dot(qh, kth) * dm
        inter = jnp.exp(fc + m_prev - m_t)
        cx = c_ref[h]
        g = _dot(qh, cx.astype(BF16))
        num = _dot(sc.astype(BF16), vh) + inter * g[:, 0:dh]
        den = jnp.sum(sc, axis=1, keepdims=True) + inter * g[:, dh:dh + 1]
        hh = num * (1.0 / jnp.maximum(jnp.abs(den), jnp.exp(-m_t)))
        hn = hh * lax.rsqrt(jnp.mean(hh * hh, axis=1, keepdims=True) + EPS) * gain_ref[:, hs]
        o_ref[:, hs] = (jax.nn.sigmoid(og_ref[:, hs]) * hn).astype(o_ref.dtype)

        m_last = m_t[L - 1:L, :]
        f_last = fc[L - 1:L, :]
        w_r = jnp.exp(f_last + a_r - m_last)
        decay = jnp.exp(f_last + m_prev - m_last)
        kw = (kth.astype(F32) * w_r).astype(BF16)
        v_ext = jnp.concatenate([vh, one_col], axis=1)
        c_ref[h] = decay * cx + _dot(kw, v_ext)
        m_ref[h] = m_last


def _mlstm(q, kt, v, og, gc, gr, gain, c0, m0, *, batch, seq, chunk, row0, kt_per_seq):
    mw = q.shape[1]
    heads, dh = c0.shape[1], c0.shape[2]
    nc = seq // chunk
    blk0 = row0 // chunk
    ng = gr.shape[-2]
    tok = lambda b, c: (blk0 + b * nc + c, 0)
    if kt_per_seq:
        kt_spec = pl.BlockSpec((None, mw, chunk), lambda b, c: (b, 0, c))
        gr_spec = pl.BlockSpec((None, ng, chunk), lambda b, c: (b, 0, c))
    else:
        kt_spec = pl.BlockSpec((mw, chunk), lambda b, c: (0, blk0 + b * nc + c))
        gr_spec = pl.BlockSpec((ng, chunk), lambda b, c: (0, blk0 + b * nc + c))
    kern = functools.partial(_mlstm_kernel, chunk=chunk, heads=heads, dh=dh)
    st_spec = pl.BlockSpec((None, heads, dh, dh + LANES), lambda b, c: (b, 0, 0, 0))
    m_spec = pl.BlockSpec((None, heads, 1, 1), lambda b, c: (b, 0, 0, 0))
    return pl.pallas_call(
        kern,
        out_shape=(jax.ShapeDtypeStruct((batch * seq, mw), BF16),
                   jax.ShapeDtypeStruct(c0.shape, F32),
                   jax.ShapeDtypeStruct(m0.shape, F32)),
        grid=(batch, nc),
        in_specs=[pl.BlockSpec((chunk, mw), tok),
                  kt_spec,
                  pl.BlockSpec((chunk, mw), tok),
                  pl.BlockSpec((chunk, mw), tok),
                  pl.BlockSpec((chunk, LANES), tok),
                  gr_spec,
                  _resident(gain.shape, lambda b, c: (0, 0)),
                  st_spec, m_spec],
        out_specs=(pl.BlockSpec((chunk, mw), lambda b, c: (b * nc + c, 0)), st_spec, m_spec),
        compiler_params=_cparams(("parallel", "arbitrary")),
        name="mlstm",
    )(q, kt, v, og, gc, gr, gain, c0, m0)


def _outproj_body(p_ref, m_ref, x_ref, g1_ref, sc_ref, sh_ref, wo_ref, wrt_ref, br_ref,
                  x1_ref, hp_ref, idx_ref, gate_ref, h_scr, *, pw, n_exp):
    mix = _dot(p_ref[...], wo_ref[0:pw, :]) + _dot(m_ref[...], wo_ref[pw:, :])
    x = x_ref[...]
    tm, d = x.shape
    for c in range(tm // MOD_CHUNK):
        rows = slice(c * MOD_CHUNK, (c + 1) * MOD_CHUNK)
        x1_ref[rows, :] = x[rows, :] + g1_ref[c:c + 1, :] * mix[rows, :]
    _modulated_norm(x1_ref[...], sc_ref, sh_ref, h_scr)
    hb = h_scr[...]
    hi = lax.bitcast_convert_type(hb[:, 0:d // 2].astype(F32), U32)
    lo = lax.bitcast_convert_type(hb[:, d // 2:].astype(F32), U32)
    hp_ref[...] = (hi & jnp.uint32(0xFFFF0000)) | (lo >> 16)

    logits = _dot_nt(wrt_ref[...], hb) + br_ref[...]
    e_iota = lax.broadcasted_iota(I32, logits.shape, 0)
    vals = []
    for j in range(TOP_K):
        mx = jnp.max(logits, axis=0, keepdims=True)
        ix = jnp.min(jnp.where(logits == mx, e_iota, n_exp), axis=0, keepdims=True)
        idx_ref[j:j + 1, :] = ix
        vals.append(mx)
        logits = jnp.where(e_iota == ix, -jnp.inf, logits)
    ex = [jnp.exp(v - vals[0]) for v in vals]
    tot = ex[0]
    for e in ex[1:]:
        tot = tot + e
    inv = 1.0 / tot
    for j in range(TOP_K):
        gate_ref[j:j + 1, :] = ex[j] * inv


def _outproj_kernel(pp_ref, ps_ref, mp_ref, ms_ref, xp_ref, xs_ref, g1_ref, sc_ref, sh_ref,
                    wo_ref, wrt_ref, br_ref, x1_ref, hp_ref, idx_ref, gate_ref, h_scr, *, n_p, pw, n_exp):
    i = pl.program_id(0)
    rest = (g1_ref, sc_ref, sh_ref, wo_ref, wrt_ref, br_ref, x1_ref, hp_ref, idx_ref, gate_ref, h_scr)

    @pl.when(i < n_p)
    def _():
        _outproj_body(pp_ref, mp_ref, xp_ref, *rest, pw=pw, n_exp=n_exp)

    @pl.when(i >= n_p)
    def _():
        _outproj_body(ps_ref, ms_ref, xs_ref, *rest, pw=pw, n_exp=n_exp)


def _outproj(pool_p, pool_s, ml_p, ml_s, xp, xs, g1, sc2, sh2, w_out, w_rt, b_r):
    tp, d = xp.shape
    ts = xs.shape[0]
    t = tp + ts
    pw = pool_p.shape[1]
    mw = ml_p.shape[1]
    n_exp = w_rt.shape[0]
    n_p = tp // TOK_TILE
    n_s = ts // TOK_TILE
    cpt = TOK_TILE // MOD_CHUNK
    kern = functools.partial(_outproj_kernel, n_p=n_p, pw=pw, n_exp=n_exp)
    tok = lambda i: (i, 0)
    pidx = lambda i: (jnp.minimum(i, n_p - 1), 0)
    sidx = lambda i: (jnp.maximum(i - n_p, 0), 0)
    return pl.pallas_call(
        kern,
        out_shape=(jax.ShapeDtypeStruct((t, d), F32),
                   jax.ShapeDtypeStruct((t, d // 2), U32),
                   jax.ShapeDtypeStruct((TOP_K, t), I32),
                   jax.ShapeDtypeStruct((TOP_K, t), F32)),
        grid=(n_p + n_s,),
        in_specs=[pl.BlockSpec((TOK_TILE, pw), pidx), pl.BlockSpec((TOK_TILE, pw), sidx),
                  pl.BlockSpec((TOK_TILE, mw), pidx), pl.BlockSpec((TOK_TILE, mw), sidx),
                  pl.BlockSpec((TOK_TILE, d), pidx), pl.BlockSpec((TOK_TILE, d), sidx),
                  pl.BlockSpec((cpt, d), tok), pl.BlockSpec((cpt, d), tok), pl.BlockSpec((cpt, d), tok),
                  _resident(w_out.shape, lambda i: (0, 0)),
                  _resident(w_rt.shape, lambda i: (0, 0)),
                  _resident(b_r.shape, lambda i: (0, 0))],
        out_specs=(pl.BlockSpec((TOK_TILE, d), tok),
                   pl.BlockSpec((TOK_TILE, d // 2), tok),
                   pl.BlockSpec((TOP_K, TOK_TILE), lambda i: (0, i)),
                   pl.BlockSpec((TOP_K, TOK_TILE), lambda i: (0, i))),
        scratch_shapes=[pltpu.VMEM((TOK_TILE, d), BF16)],
        compiler_params=_cparams(("parallel",)),
        name="outproj",
    )(pool_p, pool_s, ml_p, ml_s, xp, xs, g1, sc2, sh2, w_out, w_rt, b_r)


def _rank_kernel(idx_ref, rank_ref, cnt_ref, carry_scr, *, n_exp):
    i = pl.program_id(0)

    @pl.when(i == 0)
    def _():
        carry_scr[...] = jnp.zeros_like(carry_scr)

    tm = idx_ref.shape[1]
    e_iota = lax.broadcasted_iota(I32, (n_exp, tm), 0)
    hots = [e_iota == idx_ref[j:j + 1, :] for j in range(TOP_K)]
    cnt = hots[0].astype(F32)
    for hot in hots[1:]:
        cnt = cnt + hot.astype(F32)
    r = lax.broadcasted_iota(I32, (tm, tm), 0)
    c = lax.broadcasted_iota(I32, (tm, tm), 1)
    before = (r < c).astype(BF16)
    prefix = _dot(cnt.astype(BF16), before) + carry_scr[:, 0:1]
    for j in range(TOP_K):
        rank_ref[j:j + 1, :] = jnp.sum(jnp.where(hots[j], prefix, 0.0), axis=0, keepdims=True).astype(I32)
    carry_scr[...] = carry_scr[...] + jnp.sum(cnt, axis=1, keepdims=True)
    cnt_ref[...] = carry_scr[...]


def _ranks(idx_t, n_exp):
    k, t = idx_t.shape
    return pl.pallas_call(
        functools.partial(_rank_kernel, n_exp=n_exp),
        out_shape=(jax.ShapeDtypeStruct((k, t), I32), jax.ShapeDtypeStruct((n_exp, LANES), F32)),
        grid=(t // TOK_TILE,),
        in_specs=[pl.BlockSpec((k, TOK_TILE), lambda i: (0, i))],
        out_specs=(pl.BlockSpec((k, TOK_TILE), lambda i: (0, i)),
                   pl.BlockSpec((n_exp, LANES), lambda i: (0, 0))),
        scratch_shapes=[pltpu.VMEM((n_exp, LANES), F32)],
        compiler_params=_cparams(("arbitrary",)),
        name="ranks",
    )(idx_t)


def _dispatch_kernel(dest_ref, padrow_ref, padlen_ref, fillrow_ref, filln_ref, h_ref, xs_ref, zero_scr, sem,
                     *, n_tok, n_regions):
    i = pl.program_id(0)
    tm = h_ref.shape[0]
    base = i * tm

    def row_copy(t, j):
        d = dest_ref[j * n_tok + base + t]
        return pltpu.make_async_copy(h_ref.at[pl.ds(t, 1), :], xs_ref.at[pl.ds(d, 1), :], sem.at[0])

    def issue(t, carry):
        for j in range(TOP_K):
            row_copy(t, j).start()
        return carry

    lax.fori_loop(0, tm, issue, 0)

    @pl.when(i == 0)
    def _():
        zero_scr[...] = jnp.zeros_like(zero_scr)

        def pad_copy(e, r):
            return pltpu.make_async_copy(zero_scr.at[pl.ds(0, 1), :],
                                         xs_ref.at[pl.ds(padrow_ref[e] + r, 1), :], sem.at[1])

        def fill_copy(e, c):
            row = pl.multiple_of(fillrow_ref[e] + c * MOE_SUB, MOE_SUB)
            return pltpu.make_async_copy(zero_scr, xs_ref.at[pl.ds(row, MOE_SUB), :], sem.at[2])

        def start_region(e, carry):
            lax.fori_loop(0, padlen_ref[e], lambda r, c: (pad_copy(e, r).start(), c)[1], 0)
            lax.fori_loop(0, filln_ref[e], lambda k, c: (fill_copy(e, k).start(), c)[1], 0)
            return carry

        def wait_region(e, carry):
            lax.fori_loop(0, padlen_ref[e], lambda r, c: (pad_copy(e, r).wait(), c)[1], 0)
            lax.fori_loop(0, filln_ref[e], lambda k, c: (fill_copy(e, k).wait(), c)[1], 0)
            return carry

        lax.fori_loop(0, n_regions, start_region, 0)
        lax.fori_loop(0, n_regions, wait_region, 0)

    def drain(t, carry):
        for j in range(TOP_K):
            row_copy(t, j).wait()
        return carry

    lax.fori_loop(0, tm, drain, 0)


def _dispatch(dest_flat, pad_row, pad_len, fill_row, fill_n, h_packed, rows):
    t, half = h_packed.shape
    kern = functools.partial(_dispatch_kernel, n_tok=t, n_regions=pad_row.shape[0])
    return pl.pallas_call(
        kern,
        out_shape=jax.ShapeDtypeStruct((rows, half), U32),
        grid_spec=pltpu.PrefetchScalarGridSpec(
            num_scalar_prefetch=5,
            grid=(t // TOK_TILE,),
            in_specs=[pl.BlockSpec((TOK_TILE, half), lambda i, *_: (i, 0))],
            out_specs=pl.BlockSpec(memory_space=pl.ANY),
            scratch_shapes=[pltpu.VMEM((MOE_SUB, half), U32), pltpu.SemaphoreType.DMA((3,))]),
        compiler_params=_cparams(("arbitrary",)),
        name="dispatch",
    )(dest_flat, pad_row, pad_len, fill_row, fill_n, h_packed)


def _unpack_rows(words):
    hi = lax.bitcast_convert_type(words & jnp.uint32(0xFFFF0000), F32).astype(BF16)
    lo = lax.bitcast_convert_type(words << 16, F32).astype(BF16)
    return jnp.concatenate([hi, lo], axis=1)


def _moe_kernel(be_ref, bv_ref, na_ref, x_ref, wg_ref, wl_ref, bg_ref, bl_ref, wd_ref, bd_ref,
                o_ref, wup_scr, wd_scr, *, tf):
    b = pl.program_id(0)
    f = pl.program_id(1)
    rows_total = x_ref.shape[0]

    @pl.when(jnp.logical_and(b >= na_ref[0], f == 0))
    def _():
        o_ref[...] = jnp.zeros_like(o_ref)

    @pl.when(b < na_ref[0])
    def _():
        valid = bv_ref[b]

        @pl.when(f == 0)
        def _():
            o_ref[...] = jnp.broadcast_to(bd_ref[...], o_ref.shape)

        wup_scr[:, 0:tf] = wg_ref[...].astype(BF16)
        wup_scr[:, tf:] = wl_ref[...].astype(BF16)
        wd_scr[...] = wd_ref[...].astype(BF16)
        for s in range(rows_total // MOE_SUB):
            rows = slice(s * MOE_SUB, (s + 1) * MOE_SUB)

            @pl.when(s * MOE_SUB < valid)
            def _():
                x = _unpack_rows(x_ref[rows, :])
                up = _dot(x, wup_scr[...])
                g = jnp.minimum(up[:, 0:tf] + bg_ref[...], SWIGLU_LIMIT)
                lin = jnp.clip(up[:, tf:] + bl_ref[...], -SWIGLU_LIMIT, SWIGLU_LIMIT)
                act = g * jax.nn.sigmoid(SWIGLU_ALPHA * g) * (lin + 1.0)
                o_ref[rows, :] = o_ref[rows, :] + _dot(act.astype(BF16), wd_scr[...])


def _moe(blk_e, blk_valid, n_act, xs, w_up, b_up, w_down, b_down):
    rows, half = xs.shape
    d = 2 * half
    n_exp, _, two_f = w_up.shape[1:]
    ff = two_f // 2
    tf = MOE_FF_TILE
    nf = ff // tf
    nb = rows // MOE_ROWS
    b_up3 = b_up.reshape(n_exp, 1, two_f)
    b_dn3 = b_down.reshape(n_exp, 1, d)

    def blk(b, f, be, bv, na):
        return jnp.minimum(b, na[0] - 1)

    def ftile(b, f, be, bv, na):
        return jnp.where(b < na[0], f, nf - 1)

    return pl.pallas_call(
        functools.partial(_moe_kernel, tf=tf),
        out_shape=jax.ShapeDtypeStruct((rows, d), F32),
        grid_spec=pltpu.PrefetchScalarGridSpec(
            num_scalar_prefetch=3,
            grid=(nb, nf),
            in_specs=[pl.BlockSpec((MOE_ROWS, half), lambda *a: (blk(*a), 0)),
                      pl.BlockSpec((None, None, d, tf), lambda *a: (0, a[2][a[0]], 0, ftile(*a))),
                      pl.BlockSpec((None, None, d, tf), lambda *a: (0, a[2][a[0]], 0, ftile(*a) + nf)),
                      pl.BlockSpec((None, 1, tf), lambda *a: (a[2][a[0]], 0, ftile(*a))),
                      pl.BlockSpec((None, 1, tf), lambda *a: (a[2][a[0]], 0, ftile(*a) + nf)),
                      pl.BlockSpec((None, None, tf, d), lambda *a: (0, a[2][a[0]], ftile(*a), 0)),
                      pl.BlockSpec((None, 1, d), lambda *a: (a[2][a[0]], 0, 0))],
            out_specs=pl.BlockSpec((MOE_ROWS, d), lambda *a: (a[0], 0)),
            scratch_shapes=[pltpu.VMEM((d, 2 * tf), BF16), pltpu.VMEM((tf, d), BF16)]),
        compiler_params=_cparams(("arbitrary", "arbitrary")),
        name="moe_experts",
    )(blk_e, blk_valid, n_act, xs, w_up, w_up, b_up3, b_up3, w_down, b_dn3)


def _combine_kernel(dest_ref, ys_ref, x1_ref, gate_ref, g2_ref, fg_ref, yp_ref, yo_ref, buf, sem, *, n_tok, n_p):
    i = pl.program_id(0)
    tm = x1_ref.shape[0]
    base = i * tm

    def row_copy(t, j):
        d = dest_ref[j * n_tok + base + t]
        return pltpu.make_async_copy(ys_ref.at[pl.ds(d, 1), :], buf.at[j, pl.ds(t, 1), :], sem.at[0])

    def issue(t, carry):
        for j in range(TOP_K):
            row_copy(t, j).start()
        return carry

    def drain(t, carry):
        for j in range(TOP_K):
            row_copy(t, j).wait()
        return carry

    lax.fori_loop(0, tm, issue, 0)
    lax.fori_loop(0, tm, drain, 0)

    moe = gate_ref[:, 0:1] * buf[0]
    for j in range(1, TOP_K):
        moe = moe + gate_ref[:, j:j + 1] * buf[j]
    x1 = x1_ref[...]

    def finish(out_ref):
        for c in range(tm // MOD_CHUNK):
            rows = slice(c * MOD_CHUNK, (c + 1) * MOD_CHUNK)
            xo = x1[rows, :] + g2_ref[c:c + 1, :] * moe[rows, :]
            out_ref[rows, :] = xo * lax.rsqrt(jnp.mean(xo * xo, axis=-1, keepdims=True) + EPS) * fg_ref[...]

    @pl.when(i < n_p)
    def _():
        finish(yp_ref)

    @pl.when(i >= n_p)
    def _():
        finish(yo_ref)


def _combine(dest_flat, ys, x1, gate_c, g2, final_gain, tp):
    t, d = x1.shape
    ts = t - tp
    n_p = tp // COMB_TILE
    n_s = ts // COMB_TILE
    cpt = COMB_TILE // MOD_CHUNK
    kern = functools.partial(_combine_kernel, n_tok=t, n_p=n_p)
    tok = lambda i, *_: (i, 0)
    return pl.pallas_call(
        kern,
        out_shape=(jax.ShapeDtypeStruct((tp, d), F32), jax.ShapeDtypeStruct((ts, d), F32)),
        grid_spec=pltpu.PrefetchScalarGridSpec(
            num_scalar_prefetch=1,
            grid=(n_p + n_s,),
            in_specs=[pl.BlockSpec(memory_space=pl.ANY),
                      pl.BlockSpec((COMB_TILE, d), tok),
                      pl.BlockSpec((COMB_TILE, TOP_K), tok),
                      pl.BlockSpec((cpt, d), tok),
                      pl.BlockSpec((1, d), lambda i, *_: (0, 0))],
            out_specs=(pl.BlockSpec((COMB_TILE, d), lambda i, *_: (jnp.minimum(i, n_p - 1), 0)),
                       pl.BlockSpec((COMB_TILE, d), lambda i, *_: (jnp.maximum(i - n_p, 0), 0))),
            scratch_shapes=[pltpu.VMEM((TOP_K, COMB_TILE, d), F32), pltpu.SemaphoreType.DMA((1,))]),
        compiler_params=_cparams(("arbitrary",)),
        name="combine",
    )(dest_flat, ys, x1, gate_c, g2, final_gain)


def _state_ext(c_state, n_state):
    b, h, dh, _ = c_state.shape
    ct = jnp.swapaxes(c_state, -1, -2)
    pad = jnp.zeros((b, h, dh, LANES - 1), F32)
    return jnp.concatenate([ct, n_state[..., None], pad], axis=-1)


def _state_split(cx):
    dh = cx.shape[2]
    return jnp.swapaxes(cx[..., 0:dh], -1, -2), cx[..., dh]


def kernel(x_prompt, x_sample, state_pool, state_mlstm_C, state_mlstm_n, state_mlstm_m, c_prompt, c_sample,
           w_ada, b_ada, w_in, b_gate, pool_w, pool_scale, mlstm_gain, w_out, w_router, b_router,
           w_up, b_up, w_down, b_down, final_gain):
    depth = w_ada.shape[0]
    assert depth == 1, "single-layer trunk"
    bp, sp, d = x_prompt.shape
    bs, ss, _ = x_sample.shape
    tp, ts = bp * sp, bs * ss
    t = tp + ts
    pw = state_pool.shape[-1]
    heads, dh = state_mlstm_C.shape[2], state_mlstm_C.shape[3]
    mw = heads * dh
    n_exp = w_router.shape[-1]
    pad_rows = state_pool.shape[2]
    assert tp % TOK_TILE == 0 and ts % TOK_TILE == 0 and sp % TOK_TILE == 0
    assert ss % MOD_CHUNK == 0 and 2 * heads <= SUBLANES and pad_rows < POOL_HALO

    xp = x_prompt.reshape(tp, d)
    xs = x_sample.reshape(ts, d)

    n_c = bp + bs
    c_rows = -(-n_c // SUBLANES) * SUBLANES
    c_all = jnp.concatenate([c_prompt, c_sample, jnp.zeros((c_rows - n_c, d), F32)], axis=0)
    mod = _adaln(c_all, w_ada[0], b_ada)
    row_of_chunk = jnp.concatenate([jnp.repeat(jnp.arange(bp), sp // MOD_CHUNK),
                                    bp + jnp.repeat(jnp.arange(bs), ss // MOD_CHUNK)])
    mod_c = mod[row_of_chunk]
    sh1, sc1, g1, sh2, sc2, g2 = [mod_c[:, k * d:(k + 1) * d] for k in range(6)]

    w_in0 = w_in[0]
    o0 = pw
    w_main = jnp.concatenate([w_in0[:, 0:pw], w_in0[:, o0:o0 + mw], w_in0[:, o0 + 2 * mw:o0 + 4 * mw]],
                             axis=1).astype(BF16)
    w_kt = (w_in0[:, o0 + mw:o0 + 2 * mw] * (dh ** -0.5)).T.astype(BF16)
    w_g = w_in0[:, o0 + 4 * mw:]
    ng = w_g.shape[1]
    w_gc = jnp.pad(w_g, ((0, 0), (0, LANES - ng))).astype(BF16)
    w_gt = jnp.pad(w_g.T, ((0, SUBLANES - ng), (0, 0))).astype(BF16)
    b_gc = jnp.pad(b_gate[0], (0, LANES - ng)).reshape(1, LANES)
    b_gr = jnp.pad(b_gate[0], (0, SUBLANES - ng)).reshape(SUBLANES, 1)
    u, q, kt, v, og, gc, gr = _inproj(xp, xs, sc1, sh1, w_main, w_kt, w_gc, w_gt, b_gc, b_gr, pw, mw)

    pool_wb = pool_w[0].astype(BF16)
    zeros_p = jnp.zeros((bp, POOL_HALO, pw), F32)
    st_s = jnp.concatenate([jnp.zeros((bs, POOL_HALO - pad_rows, pw), F32), state_pool[0]], axis=1)
    pool_p = _pool(u, zeros_p, pool_wb, pool_scale, batch=bp, seq=sp, row0=0, tm=TOK_TILE, pos0=0)
    pool_s = _pool(u, st_s, pool_wb, pool_scale, batch=bs, seq=ss, row0=tp, tm=ss, pos0=PAST_LEN)
    new_pool_p = u[0:tp].reshape(bp, sp, pw)[:, sp - pad_rows:]
    new_pool_s = u[tp:].reshape(bs, ss, pw)[:, ss - pad_rows:]

    gain = mlstm_gain[0].reshape(1, mw)
    c0_p = jnp.zeros((bp, heads, dh, dh + LANES), F32)
    m0_p = jnp.zeros((bp, heads, 1, 1), F32)
    chunk_p = MLSTM_CHUNK if sp % MLSTM_CHUNK == 0 else MOD_CHUNK
    ml_p, cx_p, m_p = _mlstm(q, kt, v, og, gc, gr, gain, c0_p, m0_p, batch=bp, seq=sp, chunk=chunk_p,
                             row0=0, kt_per_seq=False)
    c0_s = _state_ext(state_mlstm_C[0], state_mlstm_n[0])
    m0_s = state_mlstm_m[0].reshape(bs, heads, 1, 1)
    kt_s = kt[:, tp:].reshape(mw, bs, ss).transpose(1, 0, 2)
    gr_s = gr[:, tp:].reshape(gr.shape[0], bs, ss).transpose(1, 0, 2)
    ml_s, cx_s, m_s = _mlstm(q, kt_s, v, og, gc, gr_s, gain, c0_s, m0_s, batch=bs, seq=ss, chunk=ss,
                             row0=tp, kt_per_seq=True)
    new_c_p, new_n_p = _state_split(cx_p)
    new_c_s, new_n_s = _state_split(cx_s)

    w_rt = w_router[0].T.astype(BF16)
    b_r = b_router[0].reshape(n_exp, 1)
    x1, h_packed, idx_t, gate_t = _outproj(pool_p, pool_s, ml_p, ml_s, xp, xs, g1, sc2, sh2,
                                           w_out[0].astype(BF16), w_rt, b_r)

    rank_t, cnt = _ranks(idx_t, n_exp)
    counts = cnt[:, 0].astype(I32)
    padded = (counts + MOE_ROWS - 1) // MOE_ROWS * MOE_ROWS
    pend = jnp.cumsum(padded)
    pstart = pend - padded
    dest_flat = (pstart[idx_t] + rank_t).reshape(-1)
    n_assign = t * TOP_K
    nb = -(-(n_assign + n_exp * (MOE_ROWS - 1)) // MOE_ROWS)
    n_act = pend[-1] // MOE_ROWS
    blk_ids = jnp.arange(nb, dtype=I32)
    last = jnp.minimum(blk_ids, n_act - 1)
    blk_e = jnp.minimum(jnp.searchsorted(pend, last * MOE_ROWS, side='right'), n_exp - 1).astype(I32)
    blk_valid = jnp.where(blk_ids < n_act,
                          jnp.clip(counts[blk_e] - (last * MOE_ROWS - pstart[blk_e]), 0, MOE_ROWS), 0).astype(I32)
    sub_end = (counts + MOE_SUB - 1) // MOE_SUB * MOE_SUB
    tail = pend[-1:]
    pad_row = jnp.concatenate([pstart + counts, tail]).astype(I32)
    pad_len = jnp.concatenate([sub_end - counts, jnp.zeros((1,), I32)]).astype(I32)
    fill_row = jnp.concatenate([pstart + sub_end, tail]).astype(I32)
    fill_n = jnp.concatenate([(padded - sub_end) // MOE_SUB, (nb * MOE_ROWS - tail) // MOE_SUB]).astype(I32)
    xs_grouped = _dispatch(dest_flat, pad_row, pad_len, fill_row, fill_n, h_packed, nb * MOE_ROWS)
    ys = _moe(blk_e, blk_valid, n_act.reshape(1).astype(I32), xs_grouped, w_up, b_up[0], w_down, b_down[0])

    y_p, y_s = _combine(dest_flat, ys, x1, gate_t.T, g2, final_gain.reshape(1, d), tp)

    return (y_p.reshape(bp, sp, d), y_s.reshape(bs, ss, d),
            new_pool_p[None], new_c_p[None], new_n_p[None], m_p.reshape(1, bp, heads),
            new_pool_s[None], new_c_s[None], new_n_s[None], m_s.reshape(1, bs, heads))
```

```python
import functools

import jax
import jax.numpy as jnp
from jax import lax
from jax.experimental import pallas as pl
from jax.experimental.pallas import tpu as pltpu

F32 = jnp.float32
BF16 = jnp.bfloat16
I32 = jnp.int32
U32 = jnp.uint32

EPS = 1e-6
POOL_WINDOWS = (2, 4, 8, 16)
TOP_K = 4
SWIGLU_LIMIT = 7.0
SWIGLU_ALPHA = 1.702
PAST_LEN = 2048

LANES = 128
SUBLANES = 8
MOD_CHUNK = 64
TOK_TILE = 512
POOL_HALO = 16
MLSTM_CHUNK = 256
ADA_TILE = 1024
MOE_ROWS = 1024
MOE_SUB = 256
MOE_FF_TILE = 256
COMB_TILE = 512
VMEM_LIMIT = 56 * 1024 * 1024


def _cparams(sem, vmem=VMEM_LIMIT):
    return pltpu.CompilerParams(dimension_semantics=sem, vmem_limit_bytes=vmem)


def _resident(shape, index_map):
    return pl.BlockSpec(shape, index_map, pipeline_mode=pl.Buffered(1))


def _dot(a, b):
    return jnp.dot(a, b, preferred_element_type=F32)


def _dot_nt(a, b):
    return lax.dot_general(a, b, (((1,), (1,)), ((), ())), preferred_element_type=F32)


def _split3(x):
    hi = x.astype(BF16)
    r1 = x - hi.astype(F32)
    mid = r1.astype(BF16)
    lo = (r1 - mid.astype(F32)).astype(BF16)
    return hi, mid, lo


def _log_sigmoid(x):
    return jnp.minimum(x, 0.0) - jnp.log1p(jnp.exp(-jnp.abs(x)))


def _adaln_kernel(c_ref, w_ref, b_ref, o_ref):
    c = c_ref[...]
    s = (c * jax.nn.sigmoid(c)).astype(BF16)
    o_ref[...] = _dot(s, w_ref[...].astype(BF16)) + b_ref[...]


def _adaln(c_all, w_ada, b_ada):
    rows, d = c_all.shape
    n = w_ada.shape[1]
    return pl.pallas_call(
        _adaln_kernel,
        out_shape=jax.ShapeDtypeStruct((rows, n), F32),
        grid=(n // ADA_TILE,),
        in_specs=[pl.BlockSpec((rows, d), lambda j: (0, 0)),
                  pl.BlockSpec((d, ADA_TILE), lambda j: (0, j)),
                  pl.BlockSpec((1, ADA_TILE), lambda j: (0, j))],
        out_specs=pl.BlockSpec((rows, ADA_TILE), lambda j: (0, j)),
        compiler_params=_cparams(("parallel",)),
        name="adaln",
    )(c_all, w_ada, b_ada)


def _modulated_norm(x, sc_ref, sh_ref, h_scr):
    xn = x * lax.rsqrt(jnp.mean(x * x, axis=-1, keepdims=True) + EPS)
    for c in range(x.shape[0] // MOD_CHUNK):
        rows = slice(c * MOD_CHUNK, (c + 1) * MOD_CHUNK)
        h_scr[rows, :] = (xn[rows, :] * (1.0 + sc_ref[c:c + 1, :]) + sh_ref[c:c + 1, :]).astype(h_scr.dtype)


def _inproj_kernel(xp_ref, xs_ref, sc_ref, sh_ref, w_ref, wkt_ref, wg_ref, wgt_ref, bgc_ref, bgr_ref,
                   u_ref, q_ref, kt_ref, v_ref, og_ref, gc_ref, gr_ref, h_scr, *, n_p, pw, mw):
    i = pl.program_id(0)

    @pl.when(i < n_p)
    def _():
        _modulated_norm(xp_ref[...], sc_ref, sh_ref, h_scr)

    @pl.when(i >= n_p)
    def _():
        _modulated_norm(xs_ref[...], sc_ref, sh_ref, h_scr)

    h = h_scr[...]
    u_ref[...] = _dot(h, w_ref[:, 0:pw])
    q_ref[...] = _dot(h, w_ref[:, pw:pw + mw]).astype(BF16)
    v_ref[...] = _dot(h, w_ref[:, pw + mw:pw + 2 * mw]).astype(BF16)
    og_ref[...] = _dot(h, w_ref[:, pw + 2 * mw:pw + 3 * mw])
    kt_ref[...] = _dot_nt(wkt_ref[...], h).astype(BF16)
    gc_ref[...] = _dot(h, wg_ref[...]) + bgc_ref[...]
    gr_ref[...] = _dot_nt(wgt_ref[...], h) + bgr_ref[...]


def _inproj(xp, xs, sc1, sh1, w_main, w_kt, w_gc, w_gt, b_gc, b_gr, pw, mw):
    tp, d = xp.shape
    ts = xs.shape[0]
    t = tp + ts
    n_p = tp // TOK_TILE
    n_s = ts // TOK_TILE
    cpt = TOK_TILE // MOD_CHUNK
    ng = w_gt.shape[0]
    kern = functools.partial(_inproj_kernel, n_p=n_p, pw=pw, mw=mw)
    tok = lambda i: (i, 0)
    return pl.pallas_call(
        kern,
        out_shape=(jax.ShapeDtypeStruct((t, pw), F32),
                   jax.ShapeDtypeStruct((t, mw), BF16),
                   jax.ShapeDtypeStruct((mw, t), BF16),
                   jax.ShapeDtypeStruct((t, mw), BF16),
                   jax.ShapeDtypeStruct((t, mw), F32),
                   jax.ShapeDtypeStruct((t, LANES), F32),
                   jax.ShapeDtypeStruct((ng, t), F32)),
        grid=(n_p + n_s,),
        in_specs=[pl.BlockSpec((TOK_TILE, d), lambda i: (jnp.minimum(i, n_p - 1), 0)),
                  pl.BlockSpec((TOK_TILE, d), lambda i: (jnp.maximum(i - n_p, 0), 0)),
                  pl.BlockSpec((cpt, d), tok),
                  pl.BlockSpec((cpt, d), tok),
                  _resident(w_main.shape, lambda i: (0, 0)),
                  _resident(w_kt.shape, lambda i: (0, 0)),
                  _resident(w_gc.shape, lambda i: (0, 0)),
                  _resident(w_gt.shape, lambda i: (0, 0)),
                  _resident(b_gc.shape, lambda i: (0, 0)),
                  _resident(b_gr.shape, lambda i: (0, 0))],
        out_specs=(pl.BlockSpec((TOK_TILE, pw), tok),
                   pl.BlockSpec((TOK_TILE, mw), tok),
                   pl.BlockSpec((mw, TOK_TILE), lambda i: (0, i)),
                   pl.BlockSpec((TOK_TILE, mw), tok),
                   pl.BlockSpec((TOK_TILE, mw), tok),
                   pl.BlockSpec((TOK_TILE, LANES), tok),
                   pl.BlockSpec((ng, TOK_TILE), lambda i: (0, i))),
        scratch_shapes=[pltpu.VMEM((TOK_TILE, d), BF16)],
        compiler_params=_cparams(("parallel",)),
        name="inproj",
    )(xp, xs, sc1, sh1, w_main, w_kt, w_gc, w_gt, b_gc, b_gr)


def _pool_kernel(u_ref, st_ref, pw_ref, ps_ref, o_ref, xp_scr, *, tm, pos0, group):
    j = pl.program_id(1)

    @pl.when(j == 0)
    def _():
        xp_scr[0:POOL_HALO, :] = st_ref[...]

    @pl.when(j > 0)
    def _():
        xp_scr[0:POOL_HALO, :] = xp_scr[tm:tm + POOL_HALO, :]

    xp_scr[POOL_HALO:POOL_HALO + tm, :] = u_ref[...]
    pos = pos0 + j * tm + lax.broadcasted_iota(I32, (tm, group), 0)
    for g, w in enumerate(POOL_WINDOWS):
        cs = slice(g * group, (g + 1) * group)
        x = xp_scr[POOL_HALO:POOL_HALO + tm, cs]
        acc = x
        for s in range(1, w):
            acc = acc + xp_scr[POOL_HALO - s:POOL_HALO - s + tm, cs]
        cnt = jnp.minimum(pos + 1, w).astype(F32)
        pooled = acc / cnt - x
        mixed = _dot(pooled.astype(BF16), pw_ref[g]) * ps_ref[:, cs]
        o_ref[:, cs] = mixed.astype(o_ref.dtype)


def _pool(u, state16, pool_w, pool_scale, *, batch, seq, row0, tm, pos0):
    c = u.shape[1]
    group = c // len(POOL_WINDOWS)
    nt = seq // tm
    blk0 = row0 // tm
    kern = functools.partial(_pool_kernel, tm=tm, pos0=pos0, group=group)
    return pl.pallas_call(
        kern,
        out_shape=jax.ShapeDtypeStruct((batch * seq, c), BF16),
        grid=(batch, nt),
        in_specs=[pl.BlockSpec((tm, c), lambda b, j: (blk0 + b * nt + j, 0)),
                  pl.BlockSpec((None, POOL_HALO, c), lambda b, j: (b, 0, 0)),
                  _resident(pool_w.shape, lambda b, j: (0, 0, 0)),
                  _resident(pool_scale.shape, lambda b, j: (0, 0))],
        out_specs=pl.BlockSpec((tm, c), lambda b, j: (b * nt + j, 0)),
        scratch_shapes=[pltpu.VMEM((POOL_HALO + tm, c), F32)],
        compiler_params=_cparams(("parallel", "arbitrary")),
        name="pool",
    )(u, state16, pool_w, pool_scale)


def _mlstm_kernel(q_ref, kt_ref, v_ref, og_ref, gc_ref, gr_ref, gain_ref, c0_ref, m0_ref,
                  o_ref, c_ref, m_ref, *, chunk, heads, dh):
    ci = pl.program_id(1)
    L = chunk

    @pl.when(ci == 0)
    def _():
        c_ref[...] = c0_ref[...]
        m_ref[...] = m0_ref[...]

    gc = gc_ref[...]
    gr = gr_ref[...]
    lf_c = _log_sigmoid(gc)
    lf_r = _log_sigmoid(gr)
    row_i = lax.broadcasted_iota(I32, (L, L), 0)
    col_i = lax.broadcasted_iota(I32, (L, L), 1)
    causal = col_i <= row_i
    tri = causal.astype(BF16)
    tri_t = (row_i <= col_i).astype(BF16)
    f_c = sum(_dot(tri, p) for p in _split3(lf_c))
    f_r = sum(_dot(p, tri_t) for p in _split3(lf_r))
    one_col = (lax.broadcasted_iota(I32, (L, LANES), 1) == 0).astype(BF16)
    neg_inf = jnp.float32(-jnp.inf)

    for h in range(heads):
        hs = slice(h * dh, (h + 1) * dh)
        fc = f_c[:, heads + h:heads + h + 1]
        a_r = gr[h:h + 1, :] - f_r[heads + h:heads + h + 1, :]
        m_prev = m_ref[h]
        cm = jnp.max(jnp.where(causal, a_r, neg_inf), axis=1, keepdims=True)
        m_t = fc + jnp.maximum(m_prev, cm)
        dm = jnp.exp(jnp.where(causal, (fc - m_t) + a_r, neg_inf))
        qh = q_ref[:, hs]
        kth = kt_ref[hs, :]
        vh = v_ref[:, hs]
        sc = _dot(qh, kth) * dm
        inter = jnp.exp(fc + m_prev - m_t)
        cx = c_ref[h]
        g = _dot(qh, cx.astype(BF16))
        num = _dot(sc.astype(BF16), vh) + inter * g[:, 0:dh]
        den = jnp.sum(sc, axis=1, keepdims=True) + inter * g[:, dh:dh + 1]
        hh = num * (1.0 / jnp.maximum(jnp.abs(den), jnp.exp(-m_t)))
        hn = hh * lax.rsqrt(jnp.mean(hh * hh, axis=1, keepdims=True) + EPS) * gain_ref[:, hs]
        o_ref[:, hs] = (jax.nn.sigmoid(og_ref[:, hs]) * hn).astype(o_ref.dtype)

        m_last = m_t[L - 1:L, :]
        f_last = fc[L - 1:L, :]
        w_r = jnp.exp(f_last + a_r - m_last)
        decay = jnp.exp(f_last + m_prev - m_last)
        kw = (kth.astype(F32) * w_r).astype(BF16)
        v_ext = jnp.concatenate([vh, one_col], axis=1)
        c_ref[h] = decay * cx + _dot(kw, v_ext)
        m_ref[h] = m_last


def _mlstm(q, kt, v, og, gc, gr, gain, c0, m0, *, batch, seq, chunk, row0, kt_per_seq):
    mw = q.shape[1]
    heads, dh = c0.shape[1], c0.shape[2]
    nc = seq // chunk
    blk0 = row0 // chunk
    ng = gr.shape[-2]
    tok = lambda b, c: (blk0 + b * nc + c, 0)
    if kt_per_seq:
        kt_spec = pl.BlockSpec((None, mw, chunk), lambda b, c: (b, 0, c))
        gr_spec = pl.BlockSpec((None, ng, chunk), lambda b, c: (b, 0, c))
    else:
        kt_spec = pl.BlockSpec((mw, chunk), lambda b, c: (0, blk0 + b * nc + c))
        gr_spec = pl.BlockSpec((ng, chunk), lambda b, c: (0, blk0 + b * nc + c))
    kern = functools.partial(_mlstm_kernel, chunk=chunk, heads=heads, dh=dh)
    st_spec = pl.BlockSpec((None, heads, dh, dh + LANES), lambda b, c: (b, 0, 0, 0))
    m_spec = pl.BlockSpec((None, heads, 1, 1), lambda b, c: (b, 0, 0, 0))
    return pl.pallas_call(
        kern,
        out_shape=(jax.ShapeDtypeStruct((batch * seq, mw), BF16),
                   jax.ShapeDtypeStruct(c0.shape, F32),
                   jax.ShapeDtypeStruct(m0.shape, F32)),
        grid=(batch, nc),
        in_specs=[pl.BlockSpec((chunk, mw), tok),
                  kt_spec,
                  pl.BlockSpec((chunk, mw), tok),
                  pl.BlockSpec((chunk, mw), tok),
                  pl.BlockSpec((chunk, LANES), tok),
                  gr_spec,
                  _resident(gain.shape, lambda b, c: (0, 0)),
                  st_spec, m_spec],
        out_specs=(pl.BlockSpec((chunk, mw), lambda b, c: (b * nc + c, 0)), st_spec, m_spec),
        compiler_params=_cparams(("parallel", "arbitrary")),
        name="mlstm",
    )(q, kt, v, og, gc, gr, gain, c0, m0)


def _outproj_body(p_ref, m_ref, x_ref, g1_ref, sc_ref, sh_ref, wo_ref, wrt_ref, br_ref,
                  x1_ref, hp_ref, idx_ref, gate_ref, h_scr, *, pw, n_exp):
    mix = _dot(p_ref[...], wo_ref[0:pw, :]) + _dot(m_ref[...], wo_ref[pw:, :])
    x = x_ref[...]
    tm, d = x.shape
    for c in range(tm // MOD_CHUNK):
        rows = slice(c * MOD_CHUNK, (c + 1) * MOD_CHUNK)
        x1_ref[rows, :] = x[rows, :] + g1_ref[c:c + 1, :] * mix[rows, :]
    _modulated_norm(x1_ref[...], sc_ref, sh_ref, h_scr)
    hb = h_scr[...]
    hi = lax.bitcast_convert_type(hb[:, 0:d // 2].astype(F32), U32)
    lo = lax.bitcast_convert_type(hb[:, d // 2:].astype(F32), U32)
    hp_ref[...] = (hi & jnp.uint32(0xFFFF0000)) | (lo >> 16)

    logits = _dot_nt(wrt_ref[...], hb) + br_ref[...]
    e_iota = lax.broadcasted_iota(I32, logits.shape, 0)
    vals = []
    for j in range(TOP_K):
        mx = jnp.max(logits, axis=0, keepdims=True)
        ix = jnp.min(jnp.where(logits == mx, e_iota, n_exp), axis=0, keepdims=True)
        idx_ref[j:j + 1, :] = ix
        vals.append(mx)
        logits = jnp.where(e_iota == ix, -jnp.inf, logits)
    ex = [jnp.exp(v - vals[0]) for v in vals]
    tot = ex[0]
    for e in ex[1:]:
        tot = tot + e
    inv = 1.0 / tot
    for j in range(TOP_K):
        gate_ref[j:j + 1, :] = ex[j] * inv


def _outproj_kernel(pp_ref, ps_ref, mp_ref, ms_ref, xp_ref, xs_ref, g1_ref, sc_ref, sh_ref,
                    wo_ref, wrt_ref, br_ref, x1_ref, hp_ref, idx_ref, gate_ref, h_scr, *, n_p, pw, n_exp):
    i = pl.program_id(0)
    rest = (g1_ref, sc_ref, sh_ref, wo_ref, wrt_ref, br_ref, x1_ref, hp_ref, idx_ref, gate_ref, h_scr)

    @pl.when(i < n_p)
    def _():
        _outproj_body(pp_ref, mp_ref, xp_ref, *rest, pw=pw, n_exp=n_exp)

    @pl.when(i >= n_p)
    def _():
        _outproj_body(ps_ref, ms_ref, xs_ref, *rest, pw=pw, n_exp=n_exp)


def _outproj(pool_p, pool_s, ml_p, ml_s, xp, xs, g1, sc2, sh2, w_out, w_rt, b_r):
    tp, d = xp.shape
    ts = xs.shape[0]
    t = tp + ts
    pw = pool_p.shape[1]
    mw = ml_p.shape[1]
    n_exp = w_rt.shape[0]
    n_p = tp // TOK_TILE
    n_s = ts // TOK_TILE
    cpt = TOK_TILE // MOD_CHUNK
    kern = functools.partial(_outproj_kernel, n_p=n_p, pw=pw, n_exp=n_exp)
    tok = lambda i: (i, 0)
    pidx = lambda i: (jnp.minimum(i, n_p - 1), 0)
    sidx = lambda i: (jnp.maximum(i - n_p, 0), 0)
    return pl.pallas_call(
        kern,
        out_shape=(jax.ShapeDtypeStruct((t, d), F32),
                   jax.ShapeDtypeStruct((t, d // 2), U32),
                   jax.ShapeDtypeStruct((TOP_K, t), I32),
                   jax.ShapeDtypeStruct((TOP_K, t), F32)),
        grid=(n_p + n_s,),
        in_specs=[pl.BlockSpec((TOK_TILE, pw), pidx), pl.BlockSpec((TOK_TILE, pw), sidx),
                  pl.BlockSpec((TOK_TILE, mw), pidx), pl.BlockSpec((TOK_TILE, mw), sidx),
                  pl.BlockSpec((TOK_TILE, d), pidx), pl.BlockSpec((TOK_TILE, d), sidx),
                  pl.BlockSpec((cpt, d), tok), pl.BlockSpec((cpt, d), tok), pl.BlockSpec((cpt, d), tok),
                  _resident(w_out.shape, lambda i: (0, 0)),
                  _resident(w_rt.shape, lambda i: (0, 0)),
                  _resident(b_r.shape, lambda i: (0, 0))],
        out_specs=(pl.BlockSpec((TOK_TILE, d), tok),
                   pl.BlockSpec((TOK_TILE, d // 2), tok),
                   pl.BlockSpec((TOP_K, TOK_TILE), lambda i: (0, i)),
                   pl.BlockSpec((TOP_K, TOK_TILE), lambda i: (0, i))),
        scratch_shapes=[pltpu.VMEM((TOK_TILE, d), BF16)],
        compiler_params=_cparams(("parallel",)),
        name="outproj",
    )(pool_p, pool_s, ml_p, ml_s, xp, xs, g1, sc2, sh2, w_out, w_rt, b_r)


def _rank_kernel(idx_ref, rank_ref, cnt_ref, carry_scr, *, n_exp):
    i = pl.program_id(0)

    @pl.when(i == 0)
    def _():
        carry_scr[...] = jnp.zeros_like(carry_scr)

    tm = idx_ref.shape[1]
    e_iota = lax.broadcasted_iota(I32, (n_exp, tm), 0)
    hots = [e_iota == idx_ref[j:j + 1, :] for j in range(TOP_K)]
    cnt = hots[0].astype(F32)
    for hot in hots[1:]:
        cnt = cnt + hot.astype(F32)
    r = lax.broadcasted_iota(I32, (tm, tm), 0)
    c = lax.broadcasted_iota(I32, (tm, tm), 1)
    before = (r < c).astype(BF16)
    prefix = _dot(cnt.astype(BF16), before) + carry_scr[:, 0:1]
    for j in range(TOP_K):
        rank_ref[j:j + 1, :] = jnp.sum(jnp.where(hots[j], prefix, 0.0), axis=0, keepdims=True).astype(I32)
    carry_scr[...] = carry_scr[...] + jnp.sum(cnt, axis=1, keepdims=True)
    cnt_ref[...] = carry_scr[...]


def _ranks(idx_t, n_exp):
    k, t = idx_t.shape
    return pl.pallas_call(
        functools.partial(_rank_kernel, n_exp=n_exp),
        out_shape=(jax.ShapeDtypeStruct((k, t), I32), jax.ShapeDtypeStruct((n_exp, LANES), F32)),
        grid=(t // TOK_TILE,),
        in_specs=[pl.BlockSpec((k, TOK_TILE), lambda i: (0, i))],
        out_specs=(pl.BlockSpec((k, TOK_TILE), lambda i: (0, i)),
                   pl.BlockSpec((n_exp, LANES), lambda i: (0, 0))),
        scratch_shapes=[pltpu.VMEM((n_exp, LANES), F32)],
        compiler_params=_cparams(("arbitrary",)),
        name="ranks",
    )(idx_t)


def _dispatch_kernel(dest_ref, padrow_ref, padlen_ref, fillrow_ref, filln_ref, h_ref, xs_ref, zero_scr, sem,
                     *, n_tok, n_regions):
    i = pl.program_id(0)
    tm = h_ref.shape[0]
    base = i * tm

    def row_copy(t, j):
        d = dest_ref[j * n_tok + base + t]
        return pltpu.make_async_copy(h_ref.at[pl.ds(t, 1), :], xs_ref.at[pl.ds(d, 1), :], sem.at[0])

    def issue(t, carry):
        for j in range(TOP_K):
            row_copy(t, j).start()
        return carry

    lax.fori_loop(0, tm, issue, 0)

    @pl.when(i == 0)
    def _():
        zero_scr[...] = jnp.zeros_like(zero_scr)

        def pad_copy(e, r):
            return pltpu.make_async_copy(zero_scr.at[pl.ds(0, 1), :],
                                         xs_ref.at[pl.ds(padrow_ref[e] + r, 1), :], sem.at[1])

        def fill_copy(e, c):
            row = pl.multiple_of(fillrow_ref[e] + c * MOE_SUB, MOE_SUB)
            return pltpu.make_async_copy(zero_scr, xs_ref.at[pl.ds(row, MOE_SUB), :], sem.at[2])

        def start_region(e, carry):
            lax.fori_loop(0, padlen_ref[e], lambda r, c: (pad_copy(e, r).start(), c)[1], 0)
            lax.fori_loop(0, filln_ref[e], lambda k, c: (fill_copy(e, k).start(), c)[1], 0)
            return carry

        def wait_region(e, carry):
            lax.fori_loop(0, padlen_ref[e], lambda r, c: (pad_copy(e, r).wait(), c)[1], 0)
            lax.fori_loop(0, filln_ref[e], lambda k, c: (fill_copy(e, k).wait(), c)[1], 0)
            return carry

        lax.fori_loop(0, n_regions, start_region, 0)
        lax.fori_loop(0, n_regions, wait_region, 0)

    def drain(t, carry):
        for j in range(TOP_K):
            row_copy(t, j).wait()
        return carry

    lax.fori_loop(0, tm, drain, 0)


def _dispatch(dest_flat, pad_row, pad_len, fill_row, fill_n, h_packed, rows):
    t, half = h_packed.shape
    kern = functools.partial(_dispatch_kernel, n_tok=t, n_regions=pad_row.shape[0])
    return pl.pallas_call(
        kern,
        out_shape=jax.ShapeDtypeStruct((rows, half), U32),
        grid_spec=pltpu.PrefetchScalarGridSpec(
            num_scalar_prefetch=5,
            grid=(t // TOK_TILE,),
            in_specs=[pl.BlockSpec((TOK_TILE, half), lambda i, *_: (i, 0))],
            out_specs=pl.BlockSpec(memory_space=pl.ANY),
            scratch_shapes=[pltpu.VMEM((MOE_SUB, half), U32), pltpu.SemaphoreType.DMA((3,))]),
        compiler_params=_cparams(("arbitrary",)),
        name="dispatch",
    )(dest_flat, pad_row, pad_len, fill_row, fill_n, h_packed)


def _unpack_rows(words):
    hi = lax.bitcast_convert_type(words & jnp.uint32(0xFFFF0000), F32).astype(BF16)
    lo = lax.bitcast_convert_type(words << 16, F32).astype(BF16)
    return jnp.concatenate([hi, lo], axis=1)


def _moe_kernel(be_ref, bv_ref, na_ref, x_ref, wg_ref, wl_ref, bg_ref, bl_ref, wd_ref, bd_ref,
                o_ref, wup_scr, wd_scr, *, tf):
    b = pl.program_id(0)
    f = pl.program_id(1)
    rows_total = x_ref.shape[0]

    @pl.when(jnp.logical_and(b >= na_ref[0], f == 0))
    def _():
        o_ref[...] = jnp.zeros_like(o_ref)

    @pl.when(b < na_ref[0])
    def _():
        valid = bv_ref[b]

        @pl.when(f == 0)
        def _():
            o_ref[...] = jnp.broadcast_to(bd_ref[...], o_ref.shape)

        n_sub_max = rows_total // MOE_SUB
        n_sub = (valid + MOE_SUB - 1) // MOE_SUB

        def body(n_live):
            wup_scr[:, 0:tf] = wg_ref[...].astype(BF16)
            wup_scr[:, tf:] = wl_ref[...].astype(BF16)
            wd_scr[...] = wd_ref[...].astype(BF16)
            for s in range(n_live):
                rows = slice(s * MOE_SUB, (s + 1) * MOE_SUB)
                x = _unpack_rows(x_ref[rows, :])
                up = _dot(x, wup_scr[...])
                g = jnp.minimum(up[:, 0:tf] + bg_ref[...], SWIGLU_LIMIT)
                lin = jnp.clip(up[:, tf:] + bl_ref[...], -SWIGLU_LIMIT, SWIGLU_LIMIT)
                act = g * jax.nn.sigmoid(SWIGLU_ALPHA * g) * (lin + 1.0)
                o_ref[rows, :] = o_ref[rows, :] + _dot(act.astype(BF16), wd_scr[...])

        for n_live in range(1, n_sub_max + 1):
            pl.when(n_sub == n_live)(functools.partial(body, n_live))


def _moe(blk_e, blk_valid, n_act, xs, w_up, b_up, w_down, b_down):
    rows, half = xs.shape
    d = 2 * half
    n_exp, _, two_f = w_up.shape[1:]
    ff = two_f // 2
    tf = MOE_FF_TILE
    nf = ff // tf
    nb = rows // MOE_ROWS
    b_up3 = b_up.reshape(n_exp, 1, two_f)
    b_dn3 = b_down.reshape(n_exp, 1, d)

    def blk(b, f, be, bv, na):
        return jnp.minimum(b, na[0] - 1)

    def ftile(b, f, be, bv, na):
        return jnp.where(b < na[0], f, nf - 1)

    return pl.pallas_call(
        functools.partial(_moe_kernel, tf=tf),
        out_shape=jax.ShapeDtypeStruct((rows, d), F32),
        grid_spec=pltpu.PrefetchScalarGridSpec(
            num_scalar_prefetch=3,
            grid=(nb, nf),
            in_specs=[pl.BlockSpec((MOE_ROWS, half), lambda *a: (blk(*a), 0)),
                      pl.BlockSpec((None, None, d, tf), lambda *a: (0, a[2][a[0]], 0, ftile(*a))),
                      pl.BlockSpec((None, None, d, tf), lambda *a: (0, a[2][a[0]], 0, ftile(*a) + nf)),
                      pl.BlockSpec((None, 1, tf), lambda *a: (a[2][a[0]], 0, ftile(*a))),
                      pl.BlockSpec((None, 1, tf), lambda *a: (a[2][a[0]], 0, ftile(*a) + nf)),
                      pl.BlockSpec((None, None, tf, d), lambda *a: (0, a[2][a[0]], ftile(*a), 0)),
                      pl.BlockSpec((None, 1, d), lambda *a: (a[2][a[0]], 0, 0))],
            out_specs=pl.BlockSpec((MOE_ROWS, d), lambda *a: (a[0], 0)),
            scratch_shapes=[pltpu.VMEM((d, 2 * tf), BF16), pltpu.VMEM((tf, d), BF16)]),
        compiler_params=_cparams(("arbitrary", "arbitrary")),
        name="moe_experts",
    )(blk_e, blk_valid, n_act, xs, w_up, w_up, b_up3, b_up3, w_down, b_dn3)


def _combine_kernel(dest_ref, ys_ref, x1_ref, gate_ref, g2_ref, fg_ref, yp_ref, yo_ref, buf, sem, *, n_tok, n_p):
    i = pl.program_id(0)
    tm = x1_ref.shape[0]
    base = i * tm

    def row_copy(t, j):
        d = dest_ref[j * n_tok + base + t]
        return pltpu.make_async_copy(ys_ref.at[pl.ds(d, 1), :], buf.at[j, pl.ds(t, 1), :], sem.at[0])

    def issue(t, carry):
        for j in range(TOP_K):
            row_copy(t, j).start()
        return carry

    def drain(t, carry):
        for j in range(TOP_K):
            row_copy(t, j).wait()
        return carry

    lax.fori_loop(0, tm, issue, 0)
    lax.fori_loop(0, tm, drain, 0)

    moe = gate_ref[:, 0:1] * buf[0]
    for j in range(1, TOP_K):
        moe = moe + gate_ref[:, j:j + 1] * buf[j]
    x1 = x1_ref[...]

    def finish(out_ref):
        for c in range(tm // MOD_CHUNK):
            rows = slice(c * MOD_CHUNK, (c + 1) * MOD_CHUNK)
            xo = x1[rows, :] + g2_ref[c:c + 1, :] * moe[rows, :]
            out_ref[rows, :] = xo * lax.rsqrt(jnp.mean(xo * xo, axis=-1, keepdims=True) + EPS) * fg_ref[...]

    @pl.when(i < n_p)
    def _():
        finish(yp_ref)

    @pl.when(i >= n_p)
    def _():
        finish(yo_ref)


def _combine(dest_flat, ys, x1, gate_c, g2, final_gain, tp):
    t, d = x1.shape
    ts = t - tp
    n_p = tp // COMB_TILE
    n_s = ts // COMB_TILE
    cpt = COMB_TILE // MOD_CHUNK
    kern = functools.partial(_combine_kernel, n_tok=t, n_p=n_p)
    tok = lambda i, *_: (i, 0)
    return pl.pallas_call(
        kern,
        out_shape=(jax.ShapeDtypeStruct((tp, d), F32), jax.ShapeDtypeStruct((ts, d), F32)),
        grid_spec=pltpu.PrefetchScalarGridSpec(
            num_scalar_prefetch=1,
            grid=(n_p + n_s,),
            in_specs=[pl.BlockSpec(memory_space=pl.ANY),
                      pl.BlockSpec((COMB_TILE, d), tok),
                      pl.BlockSpec((COMB_TILE, TOP_K), tok),
                      pl.BlockSpec((cpt, d), tok),
                      pl.BlockSpec((1, d), lambda i, *_: (0, 0))],
            out_specs=(pl.BlockSpec((COMB_TILE, d), lambda i, *_: (jnp.minimum(i, n_p - 1), 0)),
                       pl.BlockSpec((COMB_TILE, d), lambda i, *_: (jnp.maximum(i - n_p, 0), 0))),
            scratch_shapes=[pltpu.VMEM((TOP_K, COMB_TILE, d), F32), pltpu.SemaphoreType.DMA((1,))]),
        compiler_params=_cparams(("arbitrary",)),
        name="combine",
    )(dest_flat, ys, x1, gate_c, g2, final_gain)


def _state_ext(c_state, n_state):
    b, h, dh, _ = c_state.shape
    ct = jnp.swapaxes(c_state, -1, -2)
    pad = jnp.zeros((b, h, dh, LANES - 1), F32)
    return jnp.concatenate([ct, n_state[..., None], pad], axis=-1)


def _state_split(cx):
    dh = cx.shape[2]
    return jnp.swapaxes(cx[..., 0:dh], -1, -2), cx[..., dh]


def kernel(x_prompt, x_sample, state_pool, state_mlstm_C, state_mlstm_n, state_mlstm_m, c_prompt, c_sample,
           w_ada, b_ada, w_in, b_gate, pool_w, pool_scale, mlstm_gain, w_out, w_router, b_router,
           w_up, b_up, w_down, b_down, final_gain):
    depth = w_ada.shape[0]
    assert depth == 1, "single-layer trunk"
    bp, sp, d = x_prompt.shape
    bs, ss, _ = x_sample.shape
    tp, ts = bp * sp, bs * ss
    t = tp + ts
    pw = state_pool.shape[-1]
    heads, dh = state_mlstm_C.shape[2], state_mlstm_C.shape[3]
    mw = heads * dh
    n_exp = w_router.shape[-1]
    pad_rows = state_pool.shape[2]
    assert tp % TOK_TILE == 0 and ts % TOK_TILE == 0 and sp % TOK_TILE == 0
    assert ss % MOD_CHUNK == 0 and 2 * heads <= SUBLANES and pad_rows < POOL_HALO

    xp = x_prompt.reshape(tp, d)
    xs = x_sample.reshape(ts, d)

    n_c = bp + bs
    c_rows = -(-n_c // SUBLANES) * SUBLANES
    c_all = jnp.concatenate([c_prompt, c_sample, jnp.zeros((c_rows - n_c, d), F32)], axis=0)
    mod = _adaln(c_all, w_ada[0], b_ada)
    mod_c = jnp.concatenate([
        jnp.broadcast_to(mod[0:bp, None, :], (bp, sp // MOD_CHUNK, mod.shape[1])).reshape(tp // MOD_CHUNK, -1),
        jnp.broadcast_to(mod[bp:n_c, None, :], (bs, ss // MOD_CHUNK, mod.shape[1])).reshape(ts // MOD_CHUNK, -1)])
    sh1, sc1, g1, sh2, sc2, g2 = [mod_c[:, k * d:(k + 1) * d] for k in range(6)]

    w_in0 = w_in[0]
    o0 = pw
    w_main = jnp.concatenate([w_in0[:, 0:pw], w_in0[:, o0:o0 + mw], w_in0[:, o0 + 2 * mw:o0 + 4 * mw]],
                             axis=1).astype(BF16)
    w_kt = (w_in0[:, o0 + mw:o0 + 2 * mw] * (dh ** -0.5)).T.astype(BF16)
    w_g = w_in0[:, o0 + 4 * mw:]
    ng = w_g.shape[1]
    w_gc = jnp.pad(w_g, ((0, 0), (0, LANES - ng))).astype(BF16)
    w_gt = jnp.pad(w_g.T, ((0, SUBLANES - ng), (0, 0))).astype(BF16)
    b_gc = jnp.pad(b_gate[0], (0, LANES - ng)).reshape(1, LANES)
    b_gr = jnp.pad(b_gate[0], (0, SUBLANES - ng)).reshape(SUBLANES, 1)
    u, q, kt, v, og, gc, gr = _inproj(xp, xs, sc1, sh1, w_main, w_kt, w_gc, w_gt, b_gc, b_gr, pw, mw)

    pool_wb = pool_w[0].astype(BF16)
    zeros_p = jnp.zeros((bp, POOL_HALO, pw), F32)
    st_s = jnp.concatenate([jnp.zeros((bs, POOL_HALO - pad_rows, pw), F32), state_pool[0]], axis=1)
    pool_p = _pool(u, zeros_p, pool_wb, pool_scale, batch=bp, seq=sp, row0=0, tm=TOK_TILE, pos0=0)
    pool_s = _pool(u, st_s, pool_wb, pool_scale, batch=bs, seq=ss, row0=tp, tm=ss, pos0=PAST_LEN)
    new_pool_p = u[0:tp].reshape(bp, sp, pw)[:, sp - pad_rows:]
    new_pool_s = u[tp:].reshape(bs, ss, pw)[:, ss - pad_rows:]

    gain = mlstm_gain[0].reshape(1, mw)
    c0_p = jnp.zeros((bp, heads, dh, dh + LANES), F32)
    m0_p = jnp.zeros((bp, heads, 1, 1), F32)
    chunk_p = MLSTM_CHUNK if sp % MLSTM_CHUNK == 0 else MOD_CHUNK
    ml_p, cx_p, m_p = _mlstm(q, kt, v, og, gc, gr, gain, c0_p, m0_p, batch=bp, seq=sp, chunk=chunk_p,
                             row0=0, kt_per_seq=False)
    c0_s = _state_ext(state_mlstm_C[0], state_mlstm_n[0])
    m0_s = state_mlstm_m[0].reshape(bs, heads, 1, 1)
    kt_s = kt[:, tp:].reshape(mw, bs, ss).transpose(1, 0, 2)
    gr_s = gr[:, tp:].reshape(gr.shape[0], bs, ss).transpose(1, 0, 2)
    ml_s, cx_s, m_s = _mlstm(q, kt_s, v, og, gc, gr_s, gain, c0_s, m0_s, batch=bs, seq=ss, chunk=ss,
                             row0=tp, kt_per_seq=True)
    new_c_p, new_n_p = _state_split(cx_p)
    new_c_s, new_n_s = _state_split(cx_s)

    w_rt = w_router[0].T.astype(BF16)
    b_r = b_router[0].reshape(n_exp, 1)
    x1, h_packed, idx_t, gate_t = _outproj(pool_p, pool_s, ml_p, ml_s, xp, xs, g1, sc2, sh2,
                                           w_out[0].astype(BF16), w_rt, b_r)

    rank_t, cnt = _ranks(idx_t, n_exp)
    counts = cnt[:, 0].astype(I32)
    padded = (counts + MOE_ROWS - 1) // MOE_ROWS * MOE_ROWS
    pend = jnp.cumsum(padded)
    pstart = pend - padded
    hot = idx_t[:, :, None] == jnp.arange(n_exp, dtype=I32)
    dest_flat = (jnp.sum(jnp.where(hot, pstart.astype(I32), 0), axis=-1) + rank_t).reshape(-1)
    n_assign = t * TOP_K
    nb = -(-(n_assign + n_exp * (MOE_ROWS - 1)) // MOE_ROWS)
    n_act = pend[-1] // MOE_ROWS
    blk_ids = jnp.arange(nb, dtype=I32)
    last = jnp.minimum(blk_ids, n_act - 1)
    blk_e = jnp.minimum(jnp.searchsorted(pend, last * MOE_ROWS, side='right'), n_exp - 1).astype(I32)
    blk_valid = jnp.where(blk_ids < n_act,
                          jnp.clip(counts[blk_e] - (last * MOE_ROWS - pstart[blk_e]), 0, MOE_ROWS), 0).astype(I32)
    sub_end = (counts + MOE_SUB - 1) // MOE_SUB * MOE_SUB
    tail = pend[-1:]
    pad_row = jnp.concatenate([pstart + counts, tail]).astype(I32)
    pad_len = jnp.concatenate([sub_end - counts, jnp.zeros((1,), I32)]).astype(I32)
    fill_row = jnp.concatenate([pstart + sub_end, tail]).astype(I32)
    fill_n = jnp.concatenate([(padded - sub_end) // MOE_SUB, (nb * MOE_ROWS - tail) // MOE_SUB]).astype(I32)
    xs_grouped = _dispatch(dest_flat, pad_row, pad_len, fill_row, fill_n, h_packed, nb * MOE_ROWS)
    ys = _moe(blk_e, blk_valid, n_act.reshape(1).astype(I32), xs_grouped, w_up, b_up[0], w_down, b_down[0])

    y_p, y_s = _combine(dest_flat, ys, x1, gate_t.T, g2, final_gain.reshape(1, d), tp)

    return (y_p.reshape(bp, sp, d), y_s.reshape(bs, ss, d),
            new_pool_p[None], new_c_p[None], new_n_p[None], m_p.reshape(1, bp, heads),
            new_pool_s[None], new_c_s[None], new_n_s[None], m_s.reshape(1, bs, heads))
```

```python
import functools

import jax
import jax.numpy as jnp
from jax import lax
from jax.experimental import pallas as pl
from jax.experimental.pallas import tpu as pltpu

F32 = jnp.float32
BF16 = jnp.bfloat16
I32 = jnp.int32
U32 = jnp.uint32

EPS = 1e-6
POOL_WINDOWS = (2, 4, 8, 16)
TOP_K = 4
SWIGLU_LIMIT = 7.0
SWIGLU_ALPHA = 1.702
PAST_LEN = 2048

LANES = 128
SUBLANES = 8
MOD_CHUNK = 64
TOK_TILE = 512
POOL_HALO = 16
MLSTM_CHUNK = 256
ADA_TILE = 1024
MOE_ROWS = 768
MOE_SUB = 256
MOE_FF_TILE = 512
COMB_TILE = 512
VMEM_LIMIT = 56 * 1024 * 1024


def _cparams(sem, vmem=VMEM_LIMIT):
    return pltpu.CompilerParams(dimension_semantics=sem, vmem_limit_bytes=vmem)


def _resident(shape, index_map):
    return pl.BlockSpec(shape, index_map, pipeline_mode=pl.Buffered(1))


def _dot(a, b):
    return jnp.dot(a, b, preferred_element_type=F32)


def _dot_nt(a, b):
    return lax.dot_general(a, b, (((1,), (1,)), ((), ())), preferred_element_type=F32)


def _split3(x):
    hi = x.astype(BF16)
    r1 = x - hi.astype(F32)
    mid = r1.astype(BF16)
    lo = (r1 - mid.astype(F32)).astype(BF16)
    return hi, mid, lo


def _log_sigmoid(x):
    return jnp.minimum(x, 0.0) - jnp.log1p(jnp.exp(-jnp.abs(x)))


def _adaln_kernel(c_ref, w_ref, b_ref, o_ref):
    c = c_ref[...]
    s = (c * jax.nn.sigmoid(c)).astype(BF16)
    o_ref[...] = _dot(s, w_ref[...].astype(BF16)) + b_ref[...]


def _adaln(c_all, w_ada, b_ada):
    rows, d = c_all.shape
    n = w_ada.shape[1]
    return pl.pallas_call(
        _adaln_kernel,
        out_shape=jax.ShapeDtypeStruct((rows, n), F32),
        grid=(n // ADA_TILE,),
        in_specs=[pl.BlockSpec((rows, d), lambda j: (0, 0)),
                  pl.BlockSpec((d, ADA_TILE), lambda j: (0, j)),
                  pl.BlockSpec((1, ADA_TILE), lambda j: (0, j))],
        out_specs=pl.BlockSpec((rows, ADA_TILE), lambda j: (0, j)),
        compiler_params=_cparams(("parallel",)),
        name="adaln",
    )(c_all, w_ada, b_ada)


def _modulated_norm(x, sc_ref, sh_ref, h_scr):
    xn = x * lax.rsqrt(jnp.mean(x * x, axis=-1, keepdims=True) + EPS)
    for c in range(x.shape[0] // MOD_CHUNK):
        rows = slice(c * MOD_CHUNK, (c + 1) * MOD_CHUNK)
        h_scr[rows, :] = (xn[rows, :] * (1.0 + sc_ref[c:c + 1, :]) + sh_ref[c:c + 1, :]).astype(h_scr.dtype)


def _inproj_kernel(xp_ref, xs_ref, sc_ref, sh_ref, w_ref, wkt_ref, wg_ref, wgt_ref, bgc_ref, bgr_ref,
                   u_ref, q_ref, kt_ref, v_ref, og_ref, gc_ref, gr_ref, h_scr, *, n_p, pw, mw):
    i = pl.program_id(0)

    @pl.when(i < n_p)
    def _():
        _modulated_norm(xp_ref[...], sc_ref, sh_ref, h_scr)

    @pl.when(i >= n_p)
    def _():
        _modulated_norm(xs_ref[...], sc_ref, sh_ref, h_scr)

    h = h_scr[...]
    u_ref[...] = _dot(h, w_ref[:, 0:pw])
    q_ref[...] = _dot(h, w_ref[:, pw:pw + mw]).astype(BF16)
    v_ref[...] = _dot(h, w_ref[:, pw + mw:pw + 2 * mw]).astype(BF16)
    og_ref[...] = _dot(h, w_ref[:, pw + 2 * mw:pw + 3 * mw])
    kt_ref[...] = _dot_nt(wkt_ref[...], h).astype(BF16)
    gc_ref[...] = _dot(h, wg_ref[...]) + bgc_ref[...]
    gr_ref[...] = _dot_nt(wgt_ref[...], h) + bgr_ref[...]


def _inproj(xp, xs, sc1, sh1, w_main, w_kt, w_gc, w_gt, b_gc, b_gr, pw, mw):
    tp, d = xp.shape
    ts = xs.shape[0]
    t = tp + ts
    n_p = tp // TOK_TILE
    n_s = ts // TOK_TILE
    cpt = TOK_TILE // MOD_CHUNK
    ng = w_gt.shape[0]
    kern = functools.partial(_inproj_kernel, n_p=n_p, pw=pw, mw=mw)
    tok = lambda i: (i, 0)
    return pl.pallas_call(
        kern,
        out_shape=(jax.ShapeDtypeStruct((t, pw), F32),
                   jax.ShapeDtypeStruct((t, mw), BF16),
                   jax.ShapeDtypeStruct((mw, t), BF16),
                   jax.ShapeDtypeStruct((t, mw), BF16),
                   jax.ShapeDtypeStruct((t, mw), F32),
                   jax.ShapeDtypeStruct((t, LANES), F32),
                   jax.ShapeDtypeStruct((ng, t), F32)),
        grid=(n_p + n_s,),
        in_specs=[pl.BlockSpec((TOK_TILE, d), lambda i: (jnp.minimum(i, n_p - 1), 0)),
                  pl.BlockSpec((TOK_TILE, d), lambda i: (jnp.maximum(i - n_p, 0), 0)),
                  pl.BlockSpec((cpt, d), tok),
                  pl.BlockSpec((cpt, d), tok),
                  _resident(w_main.shape, lambda i: (0, 0)),
                  _resident(w_kt.shape, lambda i: (0, 0)),
                  _resident(w_gc.shape, lambda i: (0, 0)),
                  _resident(w_gt.shape, lambda i: (0, 0)),
                  _resident(b_gc.shape, lambda i: (0, 0)),
                  _resident(b_gr.shape, lambda i: (0, 0))],
        out_specs=(pl.BlockSpec((TOK_TILE, pw), tok),
                   pl.BlockSpec((TOK_TILE, mw), tok),
                   pl.BlockSpec((mw, TOK_TILE), lambda i: (0, i)),
                   pl.BlockSpec((TOK_TILE, mw), tok),
                   pl.BlockSpec((TOK_TILE, mw), tok),
                   pl.BlockSpec((TOK_TILE, LANES), tok),
                   pl.BlockSpec((ng, TOK_TILE), lambda i: (0, i))),
        scratch_shapes=[pltpu.VMEM((TOK_TILE, d), BF16)],
        compiler_params=_cparams(("parallel",)),
        name="inproj",
    )(xp, xs, sc1, sh1, w_main, w_kt, w_gc, w_gt, b_gc, b_gr)


def _pool_kernel(u_ref, st_ref, pw_ref, ps_ref, o_ref, xp_scr, *, tm, pos0, group):
    j = pl.program_id(1)

    @pl.when(j == 0)
    def _():
        xp_scr[0:POOL_HALO, :] = st_ref[...]

    @pl.when(j > 0)
    def _():
        xp_scr[0:POOL_HALO, :] = xp_scr[tm:tm + POOL_HALO, :]

    xp_scr[POOL_HALO:POOL_HALO + tm, :] = u_ref[...]
    pos = pos0 + j * tm + lax.broadcasted_iota(I32, (tm, group), 0)
    for g, w in enumerate(POOL_WINDOWS):
        cs = slice(g * group, (g + 1) * group)
        x = xp_scr[POOL_HALO:POOL_HALO + tm, cs]
        acc = x
        for s in range(1, w):
            acc = acc + xp_scr[POOL_HALO - s:POOL_HALO - s + tm, cs]
        cnt = jnp.minimum(pos + 1, w).astype(F32)
        pooled = acc / cnt - x
        mixed = _dot(pooled.astype(BF16), pw_ref[g]) * ps_ref[:, cs]
        o_ref[:, cs] = mixed.astype(o_ref.dtype)


def _pool(u, state16, pool_w, pool_scale, *, batch, seq, row0, tm, pos0):
    c = u.shape[1]
    group = c // len(POOL_WINDOWS)
    nt = seq // tm
    blk0 = row0 // tm
    kern = functools.partial(_pool_kernel, tm=tm, pos0=pos0, group=group)
    return pl.pallas_call(
        kern,
        out_shape=jax.ShapeDtypeStruct((batch * seq, c), BF16),
        grid=(batch, nt),
        in_specs=[pl.BlockSpec((tm, c), lambda b, j: (blk0 + b * nt + j, 0)),
                  pl.BlockSpec((None, POOL_HALO, c), lambda b, j: (b, 0, 0)),
                  _resident(pool_w.shape, lambda b, j: (0, 0, 0)),
                  _resident(pool_scale.shape, lambda b, j: (0, 0))],
        out_specs=pl.BlockSpec((tm, c), lambda b, j: (b * nt + j, 0)),
        scratch_shapes=[pltpu.VMEM((POOL_HALO + tm, c), F32)],
        compiler_params=_cparams(("parallel", "arbitrary")),
        name="pool",
    )(u, state16, pool_w, pool_scale)


def _mlstm_kernel(q_ref, kt_ref, v_ref, og_ref, gc_ref, gr_ref, gain_ref, c0_ref, m0_ref,
                  o_ref, c_ref, m_ref, *, chunk, heads, dh):
    ci = pl.program_id(1)
    L = chunk

    @pl.when(ci == 0)
    def _():
        c_ref[...] = c0_ref[...]
        m_ref[...] = m0_ref[...]

    gc = gc_ref[...]
    gr = gr_ref[...]
    lf_c = _log_sigmoid(gc)
    lf_r = _log_sigmoid(gr)
    row_i = lax.broadcasted_iota(I32, (L, L), 0)
    col_i = lax.broadcasted_iota(I32, (L, L), 1)
    causal = col_i <= row_i
    tri = causal.astype(BF16)
    tri_t = (row_i <= col_i).astype(BF16)
    f_c = sum(_dot(tri, p) for p in _split3(lf_c))
    f_r = sum(_dot(p, tri_t) for p in _split3(lf_r))
    one_col = (lax.broadcasted_iota(I32, (L, LANES), 1) == 0).astype(BF16)
    neg_inf = jnp.float32(-jnp.inf)

    for h in range(heads):
        hs = slice(h * dh, (h + 1) * dh)
        fc = f_c[:, heads + h:heads + h + 1]
        a_r = gr[h:h + 1, :] - f_r[heads + h:heads + h + 1, :]
        m_prev = m_ref[h]
        cm = jnp.max(jnp.where(causal, a_r, neg_inf), axis=1, keepdims=True)
        m_t = fc + jnp.maximum(m_prev, cm)
        dm = jnp.exp(jnp.where(causal, (fc - m_t) + a_r, neg_inf))
        qh = q_ref[:, hs]
        kth = kt_ref[hs, :]
        vh = v_ref[:, hs]
        sc = _dot(qh, kth) * dm
        inter = jnp.exp(fc + m_prev - m_t)
        cx = c_ref[h]
        g = _dot(qh, cx.astype(BF16))
        num = _dot(sc.astype(BF16), vh) + inter * g[:, 0:dh]
        den = jnp.sum(sc, axis=1, keepdims=True) + inter * g[:, dh:dh + 1]
        hh = num * (1.0 / jnp.maximum(jnp.abs(den), jnp.exp(-m_t)))
        hn = hh * lax.rsqrt(jnp.mean(hh * hh, axis=1, keepdims=True) + EPS) * gain_ref[:, hs]
        o_ref[:, hs] = (jax.nn.sigmoid(og_ref[:, hs]) * hn).astype(o_ref.dtype)

        m_last = m_t[L - 1:L, :]
        f_last = fc[L - 1:L, :]
        w_r = jnp.exp(f_last + a_r - m_last)
        decay = jnp.exp(f_last + m_prev - m_last)
        kw = (kth.astype(F32) * w_r).astype(BF16)
        v_ext = jnp.concatenate([vh, one_col], axis=1)
        c_ref[h] = decay * cx + _dot(kw, v_ext)
        m_ref[h] = m_last


def _mlstm(q, kt, v, og, gc, gr, gain, c0, m0, *, batch, seq, chunk, row0, kt_per_seq):
    mw = q.shape[1]
    heads, dh = c0.shape[1], c0.shape[2]
    nc = seq // chunk
    blk0 = row0 // chunk
    ng = gr.shape[-2]
    tok = lambda b, c: (blk0 + b * nc + c, 0)
    if kt_per_seq:
        kt_spec = pl.BlockSpec((None, mw, chunk), lambda b, c: (b, 0, c))
        gr_spec = pl.BlockSpec((None, ng, chunk), lambda b, c: (b, 0, c))
    else:
        kt_spec = pl.BlockSpec((mw, chunk), lambda b, c: (0, blk0 + b * nc + c))
        gr_spec = pl.BlockSpec((ng, chunk), lambda b, c: (0, blk0 + b * nc + c))
    kern = functools.partial(_mlstm_kernel, chunk=chunk, heads=heads, dh=dh)
    st_spec = pl.BlockSpec((None, heads, dh, dh + LANES), lambda b, c: (b, 0, 0, 0))
    m_spec = pl.BlockSpec((None, heads, 1, 1), lambda b, c: (b, 0, 0, 0))
    return pl.pallas_call(
        kern,
        out_shape=(jax.ShapeDtypeStruct((batch * seq, mw), BF16),
                   jax.ShapeDtypeStruct(c0.shape, F32),
                   jax.ShapeDtypeStruct(m0.shape, F32)),
        grid=(batch, nc),
        in_specs=[pl.BlockSpec((chunk, mw), tok),
                  kt_spec,
                  pl.BlockSpec((chunk, mw), tok),
                  pl.BlockSpec((chunk, mw), tok),
                  pl.BlockSpec((chunk, LANES), tok),
                  gr_spec,
                  _resident(gain.shape, lambda b, c: (0, 0)),
                  st_spec, m_spec],
        out_specs=(pl.BlockSpec((chunk, mw), lambda b, c: (b * nc + c, 0)), st_spec, m_spec),
        compiler_params=_cparams(("parallel", "arbitrary")),
        name="mlstm",
    )(q, kt, v, og, gc, gr, gain, c0, m0)


def _outproj_body(p_ref, m_ref, x_ref, g1_ref, sc_ref, sh_ref, wo_ref, wrt_ref, br_ref,
                  x1_ref, hp_ref, idx_ref, gate_ref, h_scr, *, pw, n_exp):
    mix = _dot(p_ref[...], wo_ref[0:pw, :]) + _dot(m_ref[...], wo_ref[pw:, :])
    x = x_ref[...]
    tm, d = x.shape
    for c in range(tm // MOD_CHUNK):
        rows = slice(c * MOD_CHUNK, (c + 1) * MOD_CHUNK)
        x1_ref[rows, :] = x[rows, :] + g1_ref[c:c + 1, :] * mix[rows, :]
    _modulated_norm(x1_ref[...], sc_ref, sh_ref, h_scr)
    hb = h_scr[...]
    hi = lax.bitcast_convert_type(hb[:, 0:d // 2].astype(F32), U32)
    lo = lax.bitcast_convert_type(hb[:, d // 2:].astype(F32), U32)
    hp_ref[...] = (hi & jnp.uint32(0xFFFF0000)) | (lo >> 16)

    logits = _dot_nt(wrt_ref[...], hb) + br_ref[...]
    e_iota = lax.broadcasted_iota(I32, logits.shape, 0)
    vals = []
    for j in range(TOP_K):
        mx = jnp.max(logits, axis=0, keepdims=True)
        ix = jnp.min(jnp.where(logits == mx, e_iota, n_exp), axis=0, keepdims=True)
        idx_ref[j:j + 1, :] = ix
        vals.append(mx)
        logits = jnp.where(e_iota == ix, -jnp.inf, logits)
    ex = [jnp.exp(v - vals[0]) for v in vals]
    tot = ex[0]
    for e in ex[1:]:
        tot = tot + e
    inv = 1.0 / tot
    for j in range(TOP_K):
        gate_ref[j:j + 1, :] = ex[j] * inv


def _outproj_kernel(pp_ref, ps_ref, mp_ref, ms_ref, xp_ref, xs_ref, g1_ref, sc_ref, sh_ref,
                    wo_ref, wrt_ref, br_ref, x1_ref, hp_ref, idx_ref, gate_ref, h_scr, *, n_p, pw, n_exp):
    i = pl.program_id(0)
    rest = (g1_ref, sc_ref, sh_ref, wo_ref, wrt_ref, br_ref, x1_ref, hp_ref, idx_ref, gate_ref, h_scr)

    @pl.when(i < n_p)
    def _():
        _outproj_body(pp_ref, mp_ref, xp_ref, *rest, pw=pw, n_exp=n_exp)

    @pl.when(i >= n_p)
    def _():
        _outproj_body(ps_ref, ms_ref, xs_ref, *rest, pw=pw, n_exp=n_exp)


def _outproj(pool_p, pool_s, ml_p, ml_s, xp, xs, g1, sc2, sh2, w_out, w_rt, b_r):
    tp, d = xp.shape
    ts = xs.shape[0]
    t = tp + ts
    pw = pool_p.shape[1]
    mw = ml_p.shape[1]
    n_exp = w_rt.shape[0]
    n_p = tp // TOK_TILE
    n_s = ts // TOK_TILE
    cpt = TOK_TILE // MOD_CHUNK
    kern = functools.partial(_outproj_kernel, n_p=n_p, pw=pw, n_exp=n_exp)
    tok = lambda i: (i, 0)
    pidx = lambda i: (jnp.minimum(i, n_p - 1), 0)
    sidx = lambda i: (jnp.maximum(i - n_p, 0), 0)
    return pl.pallas_call(
        kern,
        out_shape=(jax.ShapeDtypeStruct((t, d), F32),
                   jax.ShapeDtypeStruct((t, d // 2), U32),
                   jax.ShapeDtypeStruct((TOP_K, t), I32),
                   jax.ShapeDtypeStruct((TOP_K, t), F32)),
        grid=(n_p + n_s,),
        in_specs=[pl.BlockSpec((TOK_TILE, pw), pidx), pl.BlockSpec((TOK_TILE, pw), sidx),
                  pl.BlockSpec((TOK_TILE, mw), pidx), pl.BlockSpec((TOK_TILE, mw), sidx),
                  pl.BlockSpec((TOK_TILE, d), pidx), pl.BlockSpec((TOK_TILE, d), sidx),
                  pl.BlockSpec((cpt, d), tok), pl.BlockSpec((cpt, d), tok), pl.BlockSpec((cpt, d), tok),
                  _resident(w_out.shape, lambda i: (0, 0)),
                  _resident(w_rt.shape, lambda i: (0, 0)),
                  _resident(b_r.shape, lambda i: (0, 0))],
        out_specs=(pl.BlockSpec((TOK_TILE, d), tok),
                   pl.BlockSpec((TOK_TILE, d // 2), tok),
                   pl.BlockSpec((TOP_K, TOK_TILE), lambda i: (0, i)),
                   pl.BlockSpec((TOP_K, TOK_TILE), lambda i: (0, i))),
        scratch_shapes=[pltpu.VMEM((TOK_TILE, d), BF16)],
        compiler_params=_cparams(("parallel",)),
        name="outproj",
    )(pool_p, pool_s, ml_p, ml_s, xp, xs, g1, sc2, sh2, w_out, w_rt, b_r)


def _rank_kernel(idx_ref, rank_ref, cnt_ref, carry_scr, *, n_exp):
    i = pl.program_id(0)

    @pl.when(i == 0)
    def _():
        carry_scr[...] = jnp.zeros_like(carry_scr)

    tm = idx_ref.shape[1]
    e_iota = lax.broadcasted_iota(I32, (n_exp, tm), 0)
    hots = [e_iota == idx_ref[j:j + 1, :] for j in range(TOP_K)]
    cnt = hots[0].astype(F32)
    for hot in hots[1:]:
        cnt = cnt + hot.astype(F32)
    r = lax.broadcasted_iota(I32, (tm, tm), 0)
    c = lax.broadcasted_iota(I32, (tm, tm), 1)
    before = (r < c).astype(BF16)
    prefix = _dot(cnt.astype(BF16), before) + carry_scr[:, 0:1]
    for j in range(TOP_K):
        rank_ref[j:j + 1, :] = jnp.sum(jnp.where(hots[j], prefix, 0.0), axis=0, keepdims=True).astype(I32)
    carry_scr[...] = carry_scr[...] + jnp.sum(cnt, axis=1, keepdims=True)
    cnt_ref[...] = carry_scr[...]


def _ranks(idx_t, n_exp):
    k, t = idx_t.shape
    return pl.pallas_call(
        functools.partial(_rank_kernel, n_exp=n_exp),
        out_shape=(jax.ShapeDtypeStruct((k, t), I32), jax.ShapeDtypeStruct((n_exp, LANES), F32)),
        grid=(t // TOK_TILE,),
        in_specs=[pl.BlockSpec((k, TOK_TILE), lambda i: (0, i))],
        out_specs=(pl.BlockSpec((k, TOK_TILE), lambda i: (0, i)),
                   pl.BlockSpec((n_exp, LANES), lambda i: (0, 0))),
        scratch_shapes=[pltpu.VMEM((n_exp, LANES), F32)],
        compiler_params=_cparams(("arbitrary",)),
        name="ranks",
    )(idx_t)


def _dispatch_kernel(dest_ref, padrow_ref, padlen_ref, fillrow_ref, filln_ref, h_ref, xs_ref, zero_scr, sem,
                     *, n_tok, n_regions):
    i = pl.program_id(0)
    tm = h_ref.shape[0]
    base = i * tm

    def row_copy(t, j):
        d = dest_ref[j * n_tok + base + t]
        return pltpu.make_async_copy(h_ref.at[pl.ds(t, 1), :], xs_ref.at[pl.ds(d, 1), :], sem.at[0])

    def issue(t, carry):
        for j in range(TOP_K):
            row_copy(t, j).start()
        return carry

    lax.fori_loop(0, tm, issue, 0)

    @pl.when(i == 0)
    def _():
        zero_scr[...] = jnp.zeros_like(zero_scr)

        def pad_copy(e, r):
            return pltpu.make_async_copy(zero_scr.at[pl.ds(0, 1), :],
                                         xs_ref.at[pl.ds(padrow_ref[e] + r, 1), :], sem.at[1])

        def fill_copy(e, c):
            row = pl.multiple_of(fillrow_ref[e] + c * MOE_SUB, MOE_SUB)
            return pltpu.make_async_copy(zero_scr, xs_ref.at[pl.ds(row, MOE_SUB), :], sem.at[2])

        def start_region(e, carry):
            lax.fori_loop(0, padlen_ref[e], lambda r, c: (pad_copy(e, r).start(), c)[1], 0)
            lax.fori_loop(0, filln_ref[e], lambda k, c: (fill_copy(e, k).start(), c)[1], 0)
            return carry

        def wait_region(e, carry):
            lax.fori_loop(0, padlen_ref[e], lambda r, c: (pad_copy(e, r).wait(), c)[1], 0)
            lax.fori_loop(0, filln_ref[e], lambda k, c: (fill_copy(e, k).wait(), c)[1], 0)
            return carry

        lax.fori_loop(0, n_regions, start_region, 0)
        lax.fori_loop(0, n_regions, wait_region, 0)

    def drain(t, carry):
        for j in range(TOP_K):
            row_copy(t, j).wait()
        return carry

    lax.fori_loop(0, tm, drain, 0)


def _dispatch(dest_flat, pad_row, pad_len, fill_row, fill_n, h_packed, rows):
    t, half = h_packed.shape
    kern = functools.partial(_dispatch_kernel, n_tok=t, n_regions=pad_row.shape[0])
    return pl.pallas_call(
        kern,
        out_shape=jax.ShapeDtypeStruct((rows, half), U32),
        grid_spec=pltpu.PrefetchScalarGridSpec(
            num_scalar_prefetch=5,
            grid=(t // TOK_TILE,),
            in_specs=[pl.BlockSpec((TOK_TILE, half), lambda i, *_: (i, 0))],
            out_specs=pl.BlockSpec(memory_space=pl.ANY),
            scratch_shapes=[pltpu.VMEM((MOE_SUB, half), U32), pltpu.SemaphoreType.DMA((3,))]),
        compiler_params=_cparams(("arbitrary",)),
        name="dispatch",
    )(dest_flat, pad_row, pad_len, fill_row, fill_n, h_packed)


def _unpack_rows(words):
    hi = lax.bitcast_convert_type(words & jnp.uint32(0xFFFF0000), F32).astype(BF16)
    lo = lax.bitcast_convert_type(words << 16, F32).astype(BF16)
    return jnp.concatenate([hi, lo], axis=1)


def _moe_kernel(be_ref, bv_ref, na_ref, x_ref, wg_ref, wl_ref, bg_ref, bl_ref, wd_ref, bd_ref,
                o_ref, wup_scr, wd_scr, *, tf):
    b = pl.program_id(0)
    f = pl.program_id(1)
    rows_total = x_ref.shape[0]

    @pl.when(jnp.logical_and(b >= na_ref[0], f == 0))
    def _():
        o_ref[...] = jnp.zeros_like(o_ref)

    @pl.when(b < na_ref[0])
    def _():
        valid = bv_ref[b]

        @pl.when(f == 0)
        def _():
            o_ref[...] = jnp.broadcast_to(bd_ref[...], o_ref.shape)

        n_sub_max = rows_total // MOE_SUB
        n_sub = (valid + MOE_SUB - 1) // MOE_SUB

        def body(n_live):
            wup_scr[:, 0:tf] = wg_ref[...].astype(BF16)
            wup_scr[:, tf:] = wl_ref[...].astype(BF16)
            wd_scr[...] = wd_ref[...].astype(BF16)
            for s in range(n_live):
                rows = slice(s * MOE_SUB, (s + 1) * MOE_SUB)
                x = _unpack_rows(x_ref[rows, :])
                up = _dot(x, wup_scr[...])
                g = jnp.minimum(up[:, 0:tf] + bg_ref[...], SWIGLU_LIMIT)
                lin = jnp.clip(up[:, tf:] + bl_ref[...], -SWIGLU_LIMIT, SWIGLU_LIMIT)
                act = g * jax.nn.sigmoid(SWIGLU_ALPHA * g) * (lin + 1.0)
                o_ref[rows, :] = o_ref[rows, :] + _dot(act.astype(BF16), wd_scr[...])

        for n_live in range(1, n_sub_max + 1):
            pl.when(n_sub == n_live)(functools.partial(body, n_live))


def _moe(blk_e, blk_valid, n_act, xs, w_up, b_up, w_down, b_down):
    rows, half = xs.shape
    d = 2 * half
    n_exp, _, two_f = w_up.shape[1:]
    ff = two_f // 2
    tf = MOE_FF_TILE
    nf = ff // tf
    nb = rows // MOE_ROWS
    b_up3 = b_up.reshape(n_exp, 1, two_f)
    b_dn3 = b_down.reshape(n_exp, 1, d)

    def blk(b, f, be, bv, na):
        return jnp.minimum(b, na[0] - 1)

    def ftile(b, f, be, bv, na):
        return jnp.where(b < na[0], f, nf - 1)

    return pl.pallas_call(
        functools.partial(_moe_kernel, tf=tf),
        out_shape=jax.ShapeDtypeStruct((rows, d), F32),
        grid_spec=pltpu.PrefetchScalarGridSpec(
            num_scalar_prefetch=3,
            grid=(nb, nf),
            in_specs=[pl.BlockSpec((MOE_ROWS, half), lambda *a: (blk(*a), 0)),
                      pl.BlockSpec((None, None, d, tf), lambda *a: (0, a[2][a[0]], 0, ftile(*a))),
                      pl.BlockSpec((None, None, d, tf), lambda *a: (0, a[2][a[0]], 0, ftile(*a) + nf)),
                      pl.BlockSpec((None, 1, tf), lambda *a: (a[2][a[0]], 0, ftile(*a))),
                      pl.BlockSpec((None, 1, tf), lambda *a: (a[2][a[0]], 0, ftile(*a) + nf)),
                      pl.BlockSpec((None, None, tf, d), lambda *a: (0, a[2][a[0]], ftile(*a), 0)),
                      pl.BlockSpec((None, 1, d), lambda *a: (a[2][a[0]], 0, 0))],
            out_specs=pl.BlockSpec((MOE_ROWS, d), lambda *a: (a[0], 0)),
            scratch_shapes=[pltpu.VMEM((d, 2 * tf), BF16), pltpu.VMEM((tf, d), BF16)]),
        compiler_params=_cparams(("arbitrary", "arbitrary")),
        name="moe_experts",
    )(blk_e, blk_valid, n_act, xs, w_up, w_up, b_up3, b_up3, w_down, b_dn3)


def _combine_kernel(dest_ref, ys_ref, x1_ref, gate_ref, g2_ref, fg_ref, yp_ref, yo_ref, buf, sem, *, n_tok, n_p):
    i = pl.program_id(0)
    tm = x1_ref.shape[0]
    base = i * tm

    def row_copy(t, j):
        d = dest_ref[j * n_tok + base + t]
        return pltpu.make_async_copy(ys_ref.at[pl.ds(d, 1), :], buf.at[j, pl.ds(t, 1), :], sem.at[0])

    def issue(t, carry):
        for j in range(TOP_K):
            row_copy(t, j).start()
        return carry

    def drain(t, carry):
        for j in range(TOP_K):
            row_copy(t, j).wait()
        return carry

    lax.fori_loop(0, tm, issue, 0)
    lax.fori_loop(0, tm, drain, 0)

    moe = gate_ref[:, 0:1] * buf[0]
    for j in range(1, TOP_K):
        moe = moe + gate_ref[:, j:j + 1] * buf[j]
    x1 = x1_ref[...]

    def finish(out_ref):
        for c in range(tm // MOD_CHUNK):
            rows = slice(c * MOD_CHUNK, (c + 1) * MOD_CHUNK)
            xo = x1[rows, :] + g2_ref[c:c + 1, :] * moe[rows, :]
            out_ref[rows, :] = xo * lax.rsqrt(jnp.mean(xo * xo, axis=-1, keepdims=True) + EPS) * fg_ref[...]

    @pl.when(i < n_p)
    def _():
        finish(yp_ref)

    @pl.when(i >= n_p)
    def _():
        finish(yo_ref)


def _combine(dest_flat, ys, x1, gate_c, g2, final_gain, tp):
    t, d = x1.shape
    ts = t - tp
    n_p = tp // COMB_TILE
    n_s = ts // COMB_TILE
    cpt = COMB_TILE // MOD_CHUNK
    kern = functools.partial(_combine_kernel, n_tok=t, n_p=n_p)
    tok = lambda i, *_: (i, 0)
    return pl.pallas_call(
        kern,
        out_shape=(jax.ShapeDtypeStruct((tp, d), F32), jax.ShapeDtypeStruct((ts, d), F32)),
        grid_spec=pltpu.PrefetchScalarGridSpec(
            num_scalar_prefetch=1,
            grid=(n_p + n_s,),
            in_specs=[pl.BlockSpec(memory_space=pl.ANY),
                      pl.BlockSpec((COMB_TILE, d), tok),
                      pl.BlockSpec((COMB_TILE, TOP_K), tok),
                      pl.BlockSpec((cpt, d), tok),
                      pl.BlockSpec((1, d), lambda i, *_: (0, 0))],
            out_specs=(pl.BlockSpec((COMB_TILE, d), lambda i, *_: (jnp.minimum(i, n_p - 1), 0)),
                       pl.BlockSpec((COMB_TILE, d), lambda i, *_: (jnp.maximum(i - n_p, 0), 0))),
            scratch_shapes=[pltpu.VMEM((TOP_K, COMB_TILE, d), F32), pltpu.SemaphoreType.DMA((1,))]),
        compiler_params=_cparams(("arbitrary",)),
        name="combine",
    )(dest_flat, ys, x1, gate_c, g2, final_gain)


def _state_ext(c_state, n_state):
    b, h, dh, _ = c_state.shape
    ct = jnp.swapaxes(c_state, -1, -2)
    pad = jnp.zeros((b, h, dh, LANES - 1), F32)
    return jnp.concatenate([ct, n_state[..., None], pad], axis=-1)


def _state_split(cx):
    dh = cx.shape[2]
    return jnp.swapaxes(cx[..., 0:dh], -1, -2), cx[..., dh]


def kernel(x_prompt, x_sample, state_pool, state_mlstm_C, state_mlstm_n, state_mlstm_m, c_prompt, c_sample,
           w_ada, b_ada, w_in, b_gate, pool_w, pool_scale, mlstm_gain, w_out, w_router, b_router,
           w_up, b_up, w_down, b_down, final_gain):
    depth = w_ada.shape[0]
    assert depth == 1, "single-layer trunk"
    bp, sp, d = x_prompt.shape
    bs, ss, _ = x_sample.shape
    tp, ts = bp * sp, bs * ss
    t = tp + ts
    pw = state_pool.shape[-1]
    heads, dh = state_mlstm_C.shape[2], state_mlstm_C.shape[3]
    mw = heads * dh
    n_exp = w_router.shape[-1]
    pad_rows = state_pool.shape[2]
    assert tp % TOK_TILE == 0 and ts % TOK_TILE == 0 and sp % TOK_TILE == 0
    assert ss % MOD_CHUNK == 0 and 2 * heads <= SUBLANES and pad_rows < POOL_HALO

    xp = x_prompt.reshape(tp, d)
    xs = x_sample.reshape(ts, d)

    n_c = bp + bs
    c_rows = -(-n_c // SUBLANES) * SUBLANES
    c_all = jnp.concatenate([c_prompt, c_sample, jnp.zeros((c_rows - n_c, d), F32)], axis=0)
    mod = _adaln(c_all, w_ada[0], b_ada)
    mod_c = jnp.concatenate([
        jnp.broadcast_to(mod[0:bp, None, :], (bp, sp // MOD_CHUNK, mod.shape[1])).reshape(tp // MOD_CHUNK, -1),
        jnp.broadcast_to(mod[bp:n_c, None, :], (bs, ss // MOD_CHUNK, mod.shape[1])).reshape(ts // MOD_CHUNK, -1)])
    sh1, sc1, g1, sh2, sc2, g2 = [mod_c[:, k * d:(k + 1) * d] for k in range(6)]

    w_in0 = w_in[0]
    o0 = pw
    w_main = jnp.concatenate([w_in0[:, 0:pw], w_in0[:, o0:o0 + mw], w_in0[:, o0 + 2 * mw:o0 + 4 * mw]],
                             axis=1).astype(BF16)
    w_kt = (w_in0[:, o0 + mw:o0 + 2 * mw] * (dh ** -0.5)).T.astype(BF16)
    w_g = w_in0[:, o0 + 4 * mw:]
    ng = w_g.shape[1]
    w_gc = jnp.pad(w_g, ((0, 0), (0, LANES - ng))).astype(BF16)
    w_gt = jnp.pad(w_g.T, ((0, SUBLANES - ng), (0, 0))).astype(BF16)
    b_gc = jnp.pad(b_gate[0], (0, LANES - ng)).reshape(1, LANES)
    b_gr = jnp.pad(b_gate[0], (0, SUBLANES - ng)).reshape(SUBLANES, 1)
    u, q, kt, v, og, gc, gr = _inproj(xp, xs, sc1, sh1, w_main, w_kt, w_gc, w_gt, b_gc, b_gr, pw, mw)

    pool_wb = pool_w[0].astype(BF16)
    zeros_p = jnp.zeros((bp, POOL_HALO, pw), F32)
    st_s = jnp.concatenate([jnp.zeros((bs, POOL_HALO - pad_rows, pw), F32), state_pool[0]], axis=1)
    pool_p = _pool(u, zeros_p, pool_wb, pool_scale, batch=bp, seq=sp, row0=0, tm=TOK_TILE, pos0=0)
    pool_s = _pool(u, st_s, pool_wb, pool_scale, batch=bs, seq=ss, row0=tp, tm=ss, pos0=PAST_LEN)
    new_pool_p = u[0:tp].reshape(bp, sp, pw)[:, sp - pad_rows:]
    new_pool_s = u[tp:].reshape(bs, ss, pw)[:, ss - pad_rows:]

    gain = mlstm_gain[0].reshape(1, mw)
    c0_p = jnp.zeros((bp, heads, dh, dh + LANES), F32)
    m0_p = jnp.zeros((bp, heads, 1, 1), F32)
    chunk_p = MLSTM_CHUNK if sp % MLSTM_CHUNK == 0 else MOD_CHUNK
    ml_p, cx_p, m_p = _mlstm(q, kt, v, og, gc, gr, gain, c0_p, m0_p, batch=bp, seq=sp, chunk=chunk_p,
                             row0=0, kt_per_seq=False)
    c0_s = _state_ext(state_mlstm_C[0], state_mlstm_n[0])
    m0_s = state_mlstm_m[0].reshape(bs, heads, 1, 1)
    kt_s = kt[:, tp:].reshape(mw, bs, ss).transpose(1, 0, 2)
    gr_s = gr[:, tp:].reshape(gr.shape[0], bs, ss).transpose(1, 0, 2)
    ml_s, cx_s, m_s = _mlstm(q, kt_s, v, og, gc, gr_s, gain, c0_s, m0_s, batch=bs, seq=ss, chunk=ss,
                             row0=tp, kt_per_seq=True)
    new_c_p, new_n_p = _state_split(cx_p)
    new_c_s, new_n_s = _state_split(cx_s)

    w_rt = w_router[0].T.astype(BF16)
    b_r = b_router[0].reshape(n_exp, 1)
    x1, h_packed, idx_t, gate_t = _outproj(pool_p, pool_s, ml_p, ml_s, xp, xs, g1, sc2, sh2,
                                           w_out[0].astype(BF16), w_rt, b_r)

    rank_t, cnt = _ranks(idx_t, n_exp)
    counts = cnt[:, 0].astype(I32)
    padded = (counts + MOE_ROWS - 1) // MOE_ROWS * MOE_ROWS
    pend = jnp.cumsum(padded)
    pstart = pend - padded
    hot = idx_t[:, :, None] == jnp.arange(n_exp, dtype=I32)
    dest_flat = (jnp.sum(jnp.where(hot, pstart.astype(I32), 0), axis=-1) + rank_t).reshape(-1)
    n_assign = t * TOP_K
    nb = -(-(n_assign + n_exp * (MOE_ROWS - 1)) // MOE_ROWS)
    n_act = pend[-1] // MOE_ROWS
    blk_ids = jnp.arange(nb, dtype=I32)
    last = jnp.minimum(blk_ids, n_act - 1)
    blk_e = jnp.minimum(jnp.searchsorted(pend, last * MOE_ROWS, side='right'), n_exp - 1).astype(I32)
    blk_valid = jnp.where(blk_ids < n_act,
                          jnp.clip(counts[blk_e] - (last * MOE_ROWS - pstart[blk_e]), 0, MOE_ROWS), 0).astype(I32)
    sub_end = (counts + MOE_SUB - 1) // MOE_SUB * MOE_SUB
    tail = pend[-1:]
    pad_row = jnp.concatenate([pstart + counts, tail]).astype(I32)
    pad_len = jnp.concatenate([sub_end - counts, jnp.zeros((1,), I32)]).astype(I32)
    fill_row = jnp.concatenate([pstart + sub_end, tail]).astype(I32)
    fill_n = jnp.concatenate([(padded - sub_end) // MOE_SUB, (nb * MOE_ROWS - tail) // MOE_SUB]).astype(I32)
    xs_grouped = _dispatch(dest_flat, pad_row, pad_len, fill_row, fill_n, h_packed, nb * MOE_ROWS)
    ys = _moe(blk_e, blk_valid, n_act.reshape(1).astype(I32), xs_grouped, w_up, b_up[0], w_down, b_down[0])

    y_p, y_s = _combine(dest_flat, ys, x1, gate_t.T, g2, final_gain.reshape(1, d), tp)

    return (y_p.reshape(bp, sp, d), y_s.reshape(bs, ss, d),
            new_pool_p[None], new_c_p[None], new_n_p[None], m_p.reshape(1, bp, heads),
            new_pool_s[None], new_c_s[None], new_n_s[None], m_s.reshape(1, bs, heads))
```

```python
import functools

import jax
import jax.numpy as jnp
from jax import lax
from jax.experimental import pallas as pl
from jax.experimental.pallas import tpu as pltpu

F32 = jnp.float32
BF16 = jnp.bfloat16
I32 = jnp.int32
U32 = jnp.uint32

EPS = 1e-6
POOL_WINDOWS = (2, 4, 8, 16)
TOP_K = 4
SWIGLU_LIMIT = 7.0
SWIGLU_ALPHA = 1.702
PAST_LEN = 2048

LANES = 128
SUBLANES = 8
MOD_CHUNK = 64
TOK_TILE = 512
POOL_HALO = 16
MLSTM_CHUNK = 256
ADA_TILE = 1024
MOE_ROWS = 1536
MOE_SUB = 256
MOE_FF_TILE = 256
COMB_TILE = 512
VMEM_LIMIT = 56 * 1024 * 1024


def _cparams(sem, vmem=VMEM_LIMIT):
    return pltpu.CompilerParams(dimension_semantics=sem, vmem_limit_bytes=vmem)


def _resident(shape, index_map):
    return pl.BlockSpec(shape, index_map, pipeline_mode=pl.Buffered(1))


def _dot(a, b):
    return jnp.dot(a, b, preferred_element_type=F32)


def _dot_nt(a, b):
    return lax.dot_general(a, b, (((1,), (1,)), ((), ())), preferred_element_type=F32)


def _split3(x):
    hi = x.astype(BF16)
    r1 = x - hi.astype(F32)
    mid = r1.astype(BF16)
    lo = (r1 - mid.astype(F32)).astype(BF16)
    return hi, mid, lo


def _log_sigmoid(x):
    return jnp.minimum(x, 0.0) - jnp.log1p(jnp.exp(-jnp.abs(x)))


def _adaln_kernel(c_ref, w_ref, b_ref, o_ref):
    c = c_ref[...]
    s = (c * jax.nn.sigmoid(c)).astype(BF16)
    o_ref[...] = _dot(s, w_ref[...].astype(BF16)) + b_ref[...]


def _adaln(c_all, w_ada, b_ada):
    rows, d = c_all.shape
    n = w_ada.shape[1]
    return pl.pallas_call(
        _adaln_kernel,
        out_shape=jax.ShapeDtypeStruct((rows, n), F32),
        grid=(n // ADA_TILE,),
        in_specs=[pl.BlockSpec((rows, d), lambda j: (0, 0)),
                  pl.BlockSpec((d, ADA_TILE), lambda j: (0, j)),
                  pl.BlockSpec((1, ADA_TILE), lambda j: (0, j))],
        out_specs=pl.BlockSpec((rows, ADA_TILE), lambda j: (0, j)),
        compiler_params=_cparams(("parallel",)),
        name="adaln",
    )(c_all, w_ada, b_ada)


def _modulated_norm(x, sc_ref, sh_ref, h_scr):
    xn = x * lax.rsqrt(jnp.mean(x * x, axis=-1, keepdims=True) + EPS)
    for c in range(x.shape[0] // MOD_CHUNK):
        rows = slice(c * MOD_CHUNK, (c + 1) * MOD_CHUNK)
        h_scr[rows, :] = (xn[rows, :] * (1.0 + sc_ref[c:c + 1, :]) + sh_ref[c:c + 1, :]).astype(h_scr.dtype)


def _inproj_kernel(xp_ref, xs_ref, sc_ref, sh_ref, w_ref, wkt_ref, wg_ref, wgt_ref, bgc_ref, bgr_ref,
                   u_ref, q_ref, kt_ref, v_ref, og_ref, gc_ref, gr_ref, h_scr, *, n_p, pw, mw):
    i = pl.program_id(0)

    @pl.when(i < n_p)
    def _():
        _modulated_norm(xp_ref[...], sc_ref, sh_ref, h_scr)

    @pl.when(i >= n_p)
    def _():
        _modulated_norm(xs_ref[...], sc_ref, sh_ref, h_scr)

    h = h_scr[...]
    u_ref[...] = _dot(h, w_ref[:, 0:pw])
    q_ref[...] = _dot(h, w_ref[:, pw:pw + mw]).astype(BF16)
    v_ref[...] = _dot(h, w_ref[:, pw + mw:pw + 2 * mw]).astype(BF16)
    og_ref[...] = _dot(h, w_ref[:, pw + 2 * mw:pw + 3 * mw])
    kt_ref[...] = _dot_nt(wkt_ref[...], h).astype(BF16)
    gc_ref[...] = _dot(h, wg_ref[...]) + bgc_ref[...]
    gr_ref[...] = _dot_nt(wgt_ref[...], h) + bgr_ref[...]


def _inproj(xp, xs, sc1, sh1, w_main, w_kt, w_gc, w_gt, b_gc, b_gr, pw, mw):
    tp, d = xp.shape
    ts = xs.shape[0]
    t = tp + ts
    n_p = tp // TOK_TILE
    n_s = ts // TOK_TILE
    cpt = TOK_TILE // MOD_CHUNK
    ng = w_gt.shape[0]
    kern = functools.partial(_inproj_kernel, n_p=n_p, pw=pw, mw=mw)
    tok = lambda i: (i, 0)
    return pl.pallas_call(
        kern,
        out_shape=(jax.ShapeDtypeStruct((t, pw), F32),
                   jax.ShapeDtypeStruct((t, mw), BF16),
                   jax.ShapeDtypeStruct((mw, t), BF16),
                   jax.ShapeDtypeStruct((t, mw), BF16),
                   jax.ShapeDtypeStruct((t, mw), F32),
                   jax.ShapeDtypeStruct((t, LANES), F32),
                   jax.ShapeDtypeStruct((ng, t), F32)),
        grid=(n_p + n_s,),
        in_specs=[pl.BlockSpec((TOK_TILE, d), lambda i: (jnp.minimum(i, n_p - 1), 0)),
                  pl.BlockSpec((TOK_TILE, d), lambda i: (jnp.maximum(i - n_p, 0), 0)),
                  pl.BlockSpec((cpt, d), tok),
                  pl.BlockSpec((cpt, d), tok),
                  _resident(w_main.shape, lambda i: (0, 0)),
                  _resident(w_kt.shape, lambda i: (0, 0)),
                  _resident(w_gc.shape, lambda i: (0, 0)),
                  _resident(w_gt.shape, lambda i: (0, 0)),
                  _resident(b_gc.shape, lambda i: (0, 0)),
                  _resident(b_gr.shape, lambda i: (0, 0))],
        out_specs=(pl.BlockSpec((TOK_TILE, pw), tok),
                   pl.BlockSpec((TOK_TILE, mw), tok),
                   pl.BlockSpec((mw, TOK_TILE), lambda i: (0, i)),
                   pl.BlockSpec((TOK_TILE, mw), tok),
                   pl.BlockSpec((TOK_TILE, mw), tok),
                   pl.BlockSpec((TOK_TILE, LANES), tok),
                   pl.BlockSpec((ng, TOK_TILE), lambda i: (0, i))),
        scratch_shapes=[pltpu.VMEM((TOK_TILE, d), BF16)],
        compiler_params=_cparams(("parallel",)),
        name="inproj",
    )(xp, xs, sc1, sh1, w_main, w_kt, w_gc, w_gt, b_gc, b_gr)


def _pool_kernel(u_ref, st_ref, pw_ref, ps_ref, o_ref, xp_scr, *, tm, pos0, group):
    j = pl.program_id(1)

    @pl.when(j == 0)
    def _():
        xp_scr[0:POOL_HALO, :] = st_ref[...]

    @pl.when(j > 0)
    def _():
        xp_scr[0:POOL_HALO, :] = xp_scr[tm:tm + POOL_HALO, :]

    xp_scr[POOL_HALO:POOL_HALO + tm, :] = u_ref[...]
    pos = pos0 + j * tm + lax.broadcasted_iota(I32, (tm, group), 0)
    for g, w in enumerate(POOL_WINDOWS):
        cs = slice(g * group, (g + 1) * group)
        x = xp_scr[POOL_HALO:POOL_HALO + tm, cs]
        acc = x
        for s in range(1, w):
            acc = acc + xp_scr[POOL_HALO - s:POOL_HALO - s + tm, cs]
        cnt = jnp.minimum(pos + 1, w).astype(F32)
        pooled = acc / cnt - x
        mixed = _dot(pooled.astype(BF16), pw_ref[g]) * ps_ref[:, cs]
        o_ref[:, cs] = mixed.astype(o_ref.dtype)


def _pool(u, state16, pool_w, pool_scale, *, batch, seq, row0, tm, pos0):
    c = u.shape[1]
    group = c // len(POOL_WINDOWS)
    nt = seq // tm
    blk0 = row0 // tm
    kern = functools.partial(_pool_kernel, tm=tm, pos0=pos0, group=group)
    return pl.pallas_call(
        kern,
        out_shape=jax.ShapeDtypeStruct((batch * seq, c), BF16),
        grid=(batch, nt),
        in_specs=[pl.BlockSpec((tm, c), lambda b, j: (blk0 + b * nt + j, 0)),
                  pl.BlockSpec((None, POOL_HALO, c), lambda b, j: (b, 0, 0)),
                  _resident(pool_w.shape, lambda b, j: (0, 0, 0)),
                  _resident(pool_scale.shape, lambda b, j: (0, 0))],
        out_specs=pl.BlockSpec((tm, c), lambda b, j: (b * nt + j, 0)),
        scratch_shapes=[pltpu.VMEM((POOL_HALO + tm, c), F32)],
        compiler_params=_cparams(("parallel", "arbitrary")),
        name="pool",
    )(u, state16, pool_w, pool_scale)


def _mlstm_kernel(q_ref, kt_ref, v_ref, og_ref, gc_ref, gr_ref, gain_ref, c0_ref, m0_ref,
                  o_ref, c_ref, m_ref, *, chunk, heads, dh):
    ci = pl.program_id(1)
    L = chunk

    @pl.when(ci == 0)
    def _():
        c_ref[...] = c0_ref[...]
        m_ref[...] = m0_ref[...]

    gc = gc_ref[...]
    gr = gr_ref[...]
    lf_c = _log_sigmoid(gc)
    lf_r = _log_sigmoid(gr)
    row_i = lax.broadcasted_iota(I32, (L, L), 0)
    col_i = lax.broadcasted_iota(I32, (L, L), 1)
    causal = col_i <= row_i
    tri = causal.astype(BF16)
    tri_t = (row_i <= col_i).astype(BF16)
    f_c = sum(_dot(tri, p) for p in _split3(lf_c))
    f_r = sum(_dot(p, tri_t) for p in _split3(lf_r))
    one_col = (lax.broadcasted_iota(I32, (L, LANES), 1) == 0).astype(BF16)
    neg_inf = jnp.float32(-jnp.inf)

    for h in range(heads):
        hs = slice(h * dh, (h + 1) * dh)
        fc = f_c[:, heads + h:heads + h + 1]
        a_r = gr[h:h + 1, :] - f_r[heads + h:heads + h + 1, :]
        m_prev = m_ref[h]
        cm = jnp.max(jnp.where(causal, a_r, neg_inf), axis=1, keepdims=True)
        m_t = fc + jnp.maximum(m_prev, cm)
        dm = jnp.exp(jnp.where(causal, (fc - m_t) + a_r, neg_inf))
        qh = q_ref[:, hs]
        kth = kt_ref[hs, :]
        vh = v_ref[:, hs]
        sc = _dot(qh, kth) * dm
        inter = jnp.exp(fc + m_prev - m_t)
        cx = c_ref[h]
        g = _dot(qh, cx.astype(BF16))
        num = _dot(sc.astype(BF16), vh) + inter * g[:, 0:dh]
        den = jnp.sum(sc, axis=1, keepdims=True) + inter * g[:, dh:dh + 1]
        hh = num * (1.0 / jnp.maximum(jnp.abs(den), jnp.exp(-m_t)))
        hn = hh * lax.rsqrt(jnp.mean(hh * hh, axis=1, keepdims=True) + EPS) * gain_ref[:, hs]
        o_ref[:, hs] = (jax.nn.sigmoid(og_ref[:, hs]) * hn).astype(o_ref.dtype)

        m_last = m_t[L - 1:L, :]
        f_last = fc[L - 1:L, :]
        w_r = jnp.exp(f_last + a_r - m_last)
        decay = jnp.exp(f_last + m_prev - m_last)
        kw = (kth.astype(F32) * w_r).astype(BF16)
        v_ext = jnp.concatenate([vh, one_col], axis=1)
        c_ref[h] = decay * cx + _dot(kw, v_ext)
        m_ref[h] = m_last


def _mlstm(q, kt, v, og, gc, gr, gain, c0, m0, *, batch, seq, chunk, row0, kt_per_seq):
    mw = q.shape[1]
    heads, dh = c0.shape[1], c0.shape[2]
    nc = seq // chunk
    blk0 = row0 // chunk
    ng = gr.shape[-2]
    tok = lambda b, c: (blk0 + b * nc + c, 0)
    if kt_per_seq:
        kt_spec = pl.BlockSpec((None, mw, chunk), lambda b, c: (b, 0, c))
        gr_spec = pl.BlockSpec((None, ng, chunk), lambda b, c: (b, 0, c))
    else:
        kt_spec = pl.BlockSpec((mw, chunk), lambda b, c: (0, blk0 + b * nc + c))
        gr_spec = pl.BlockSpec((ng, chunk), lambda b, c: (0, blk0 + b * nc + c))
    kern = functools.partial(_mlstm_kernel, chunk=chunk, heads=heads, dh=dh)
    st_spec = pl.BlockSpec((None, heads, dh, dh + LANES), lambda b, c: (b, 0, 0, 0))
    m_spec = pl.BlockSpec((None, heads, 1, 1), lambda b, c: (b, 0, 0, 0))
    return pl.pallas_call(
        kern,
        out_shape=(jax.ShapeDtypeStruct((batch * seq, mw), BF16),
                   jax.ShapeDtypeStruct(c0.shape, F32),
                   jax.ShapeDtypeStruct(m0.shape, F32)),
        grid=(batch, nc),
        in_specs=[pl.BlockSpec((chunk, mw), tok),
                  kt_spec,
                  pl.BlockSpec((chunk, mw), tok),
                  pl.BlockSpec((chunk, mw), tok),
                  pl.BlockSpec((chunk, LANES), tok),
                  gr_spec,
                  _resident(gain.shape, lambda b, c: (0, 0)),
                  st_spec, m_spec],
        out_specs=(pl.BlockSpec((chunk, mw), lambda b, c: (b * nc + c, 0)), st_spec, m_spec),
        compiler_params=_cparams(("parallel", "arbitrary")),
        name="mlstm",
    )(q, kt, v, og, gc, gr, gain, c0, m0)


def _outproj_body(p_ref, m_ref, x_ref, g1_ref, sc_ref, sh_ref, wo_ref, wrt_ref, br_ref,
                  x1_ref, hp_ref, idx_ref, gate_ref, h_scr, *, pw, n_exp):
    mix = _dot(p_ref[...], wo_ref[0:pw, :]) + _dot(m_ref[...], wo_ref[pw:, :])
    x = x_ref[...]
    tm, d = x.shape
    for c in range(tm // MOD_CHUNK):
        rows = slice(c * MOD_CHUNK, (c + 1) * MOD_CHUNK)
        x1_ref[rows, :] = x[rows, :] + g1_ref[c:c + 1, :] * mix[rows, :]
    _modulated_norm(x1_ref[...], sc_ref, sh_ref, h_scr)
    hb = h_scr[...]
    hi = lax.bitcast_convert_type(hb[:, 0:d // 2].astype(F32), U32)
    lo = lax.bitcast_convert_type(hb[:, d // 2:].astype(F32), U32)
    hp_ref[...] = (hi & jnp.uint32(0xFFFF0000)) | (lo >> 16)

    logits = _dot_nt(wrt_ref[...], hb) + br_ref[...]
    e_iota = lax.broadcasted_iota(I32, logits.shape, 0)
    vals = []
    for j in range(TOP_K):
        mx = jnp.max(logits, axis=0, keepdims=True)
        ix = jnp.min(jnp.where(logits == mx, e_iota, n_exp), axis=0, keepdims=True)
        idx_ref[j:j + 1, :] = ix
        vals.append(mx)
        logits = jnp.where(e_iota == ix, -jnp.inf, logits)
    ex = [jnp.exp(v - vals[0]) for v in vals]
    tot = ex[0]
    for e in ex[1:]:
        tot = tot + e
    inv = 1.0 / tot
    for j in range(TOP_K):
        gate_ref[j:j + 1, :] = ex[j] * inv


def _outproj_kernel(pp_ref, ps_ref, mp_ref, ms_ref, xp_ref, xs_ref, g1_ref, sc_ref, sh_ref,
                    wo_ref, wrt_ref, br_ref, x1_ref, hp_ref, idx_ref, gate_ref, h_scr, *, n_p, pw, n_exp):
    i = pl.program_id(0)
    rest = (g1_ref, sc_ref, sh_ref, wo_ref, wrt_ref, br_ref, x1_ref, hp_ref, idx_ref, gate_ref, h_scr)

    @pl.when(i < n_p)
    def _():
        _outproj_body(pp_ref, mp_ref, xp_ref, *rest, pw=pw, n_exp=n_exp)

    @pl.when(i >= n_p)
    def _():
        _outproj_body(ps_ref, ms_ref, xs_ref, *rest, pw=pw, n_exp=n_exp)


def _outproj(pool_p, pool_s, ml_p, ml_s, xp, xs, g1, sc2, sh2, w_out, w_rt, b_r):
    tp, d = xp.shape
    ts = xs.shape[0]
    t = tp + ts
    pw = pool_p.shape[1]
    mw = ml_p.shape[1]
    n_exp = w_rt.shape[0]
    n_p = tp // TOK_TILE
    n_s = ts // TOK_TILE
    cpt = TOK_TILE // MOD_CHUNK
    kern = functools.partial(_outproj_kernel, n_p=n_p, pw=pw, n_exp=n_exp)
    tok = lambda i: (i, 0)
    pidx = lambda i: (jnp.minimum(i, n_p - 1), 0)
    sidx = lambda i: (jnp.maximum(i - n_p, 0), 0)
    return pl.pallas_call(
        kern,
        out_shape=(jax.ShapeDtypeStruct((t, d), F32),
                   jax.ShapeDtypeStruct((t, d // 2), U32),
                   jax.ShapeDtypeStruct((TOP_K, t), I32),
                   jax.ShapeDtypeStruct((TOP_K, t), F32)),
        grid=(n_p + n_s,),
        in_specs=[pl.BlockSpec((TOK_TILE, pw), pidx), pl.BlockSpec((TOK_TILE, pw), sidx),
                  pl.BlockSpec((TOK_TILE, mw), pidx), pl.BlockSpec((TOK_TILE, mw), sidx),
                  pl.BlockSpec((TOK_TILE, d), pidx), pl.BlockSpec((TOK_TILE, d), sidx),
                  pl.BlockSpec((cpt, d), tok), pl.BlockSpec((cpt, d), tok), pl.BlockSpec((cpt, d), tok),
                  _resident(w_out.shape, lambda i: (0, 0)),
                  _resident(w_rt.shape, lambda i: (0, 0)),
                  _resident(b_r.shape, lambda i: (0, 0))],
        out_specs=(pl.BlockSpec((TOK_TILE, d), tok),
                   pl.BlockSpec((TOK_TILE, d // 2), tok),
                   pl.BlockSpec((TOP_K, TOK_TILE), lambda i: (0, i)),
                   pl.BlockSpec((TOP_K, TOK_TILE), lambda i: (0, i))),
        scratch_shapes=[pltpu.VMEM((TOK_TILE, d), BF16)],
        compiler_params=_cparams(("parallel",)),
        name="outproj",
    )(pool_p, pool_s, ml_p, ml_s, xp, xs, g1, sc2, sh2, w_out, w_rt, b_r)


def _rank_kernel(idx_ref, rank_ref, cnt_ref, carry_scr, *, n_exp):
    i = pl.program_id(0)

    @pl.when(i == 0)
    def _():
        carry_scr[...] = jnp.zeros_like(carry_scr)

    tm = idx_ref.shape[1]
    e_iota = lax.broadcasted_iota(I32, (n_exp, tm), 0)
    hots = [e_iota == idx_ref[j:j + 1, :] for j in range(TOP_K)]
    cnt = hots[0].astype(F32)
    for hot in hots[1:]:
        cnt = cnt + hot.astype(F32)
    r = lax.broadcasted_iota(I32, (tm, tm), 0)
    c = lax.broadcasted_iota(I32, (tm, tm), 1)
    before = (r < c).astype(BF16)
    prefix = _dot(cnt.astype(BF16), before) + carry_scr[:, 0:1]
    for j in range(TOP_K):
        rank_ref[j:j + 1, :] = jnp.sum(jnp.where(hots[j], prefix, 0.0), axis=0, keepdims=True).astype(I32)
    carry_scr[...] = carry_scr[...] + jnp.sum(cnt, axis=1, keepdims=True)
    cnt_ref[...] = carry_scr[...]


def _ranks(idx_t, n_exp):
    k, t = idx_t.shape
    return pl.pallas_call(
        functools.partial(_rank_kernel, n_exp=n_exp),
        out_shape=(jax.ShapeDtypeStruct((k, t), I32), jax.ShapeDtypeStruct((n_exp, LANES), F32)),
        grid=(t // TOK_TILE,),
        in_specs=[pl.BlockSpec((k, TOK_TILE), lambda i: (0, i))],
        out_specs=(pl.BlockSpec((k, TOK_TILE), lambda i: (0, i)),
                   pl.BlockSpec((n_exp, LANES), lambda i: (0, 0))),
        scratch_shapes=[pltpu.VMEM((n_exp, LANES), F32)],
        compiler_params=_cparams(("arbitrary",)),
        name="ranks",
    )(idx_t)


def _dispatch_kernel(dest_ref, padrow_ref, padlen_ref, fillrow_ref, filln_ref, h_ref, xs_ref, zero_scr, sem,
                     *, n_tok, n_regions):
    i = pl.program_id(0)
    tm = h_ref.shape[0]
    base = i * tm

    def row_copy(t, j):
        d = dest_ref[j * n_tok + base + t]
        return pltpu.make_async_copy(h_ref.at[pl.ds(t, 1), :], xs_ref.at[pl.ds(d, 1), :], sem.at[0])

    def issue(t, carry):
        for j in range(TOP_K):
            row_copy(t, j).start()
        return carry

    lax.fori_loop(0, tm, issue, 0)

    @pl.when(i == 0)
    def _():
        zero_scr[...] = jnp.zeros_like(zero_scr)

        def pad_copy(e, r):
            return pltpu.make_async_copy(zero_scr.at[pl.ds(0, 1), :],
                                         xs_ref.at[pl.ds(padrow_ref[e] + r, 1), :], sem.at[1])

        def fill_copy(e, c):
            row = pl.multiple_of(fillrow_ref[e] + c * MOE_SUB, MOE_SUB)
            return pltpu.make_async_copy(zero_scr, xs_ref.at[pl.ds(row, MOE_SUB), :], sem.at[2])

        def start_region(e, carry):
            lax.fori_loop(0, padlen_ref[e], lambda r, c: (pad_copy(e, r).start(), c)[1], 0)
            lax.fori_loop(0, filln_ref[e], lambda k, c: (fill_copy(e, k).start(), c)[1], 0)
            return carry

        def wait_region(e, carry):
            lax.fori_loop(0, padlen_ref[e], lambda r, c: (pad_copy(e, r).wait(), c)[1], 0)
            lax.fori_loop(0, filln_ref[e], lambda k, c: (fill_copy(e, k).wait(), c)[1], 0)
            return carry

        lax.fori_loop(0, n_regions, start_region, 0)
        lax.fori_loop(0, n_regions, wait_region, 0)

    def drain(t, carry):
        for j in range(TOP_K):
            row_copy(t, j).wait()
        return carry

    lax.fori_loop(0, tm, drain, 0)


def _dispatch(dest_flat, pad_row, pad_len, fill_row, fill_n, h_packed, rows):
    t, half = h_packed.shape
    kern = functools.partial(_dispatch_kernel, n_tok=t, n_regions=pad_row.shape[0])
    return pl.pallas_call(
        kern,
        out_shape=jax.ShapeDtypeStruct((rows, half), U32),
        grid_spec=pltpu.PrefetchScalarGridSpec(
            num_scalar_prefetch=5,
            grid=(t // TOK_TILE,),
            in_specs=[pl.BlockSpec((TOK_TILE, half), lambda i, *_: (i, 0))],
            out_specs=pl.BlockSpec(memory_space=pl.ANY),
            scratch_shapes=[pltpu.VMEM((MOE_SUB, half), U32), pltpu.SemaphoreType.DMA((3,))]),
        compiler_params=_cparams(("arbitrary",)),
        name="dispatch",
    )(dest_flat, pad_row, pad_len, fill_row, fill_n, h_packed)


def _unpack_rows(words):
    hi = lax.bitcast_convert_type(words & jnp.uint32(0xFFFF0000), F32).astype(BF16)
    lo = lax.bitcast_convert_type(words << 16, F32).astype(BF16)
    return jnp.concatenate([hi, lo], axis=1)


def _moe_kernel(be_ref, bv_ref, na_ref, x_ref, wg_ref, wl_ref, bg_ref, bl_ref, wd_ref, bd_ref,
                o_ref, wup_scr, wd_scr, *, tf):
    b = pl.program_id(0)
    f = pl.program_id(1)
    rows_total = x_ref.shape[0]

    @pl.when(jnp.logical_and(b >= na_ref[0], f == 0))
    def _():
        o_ref[...] = jnp.zeros_like(o_ref)

    @pl.when(b < na_ref[0])
    def _():
        valid = bv_ref[b]

        @pl.when(f == 0)
        def _():
            o_ref[...] = jnp.broadcast_to(bd_ref[...], o_ref.shape)

        n_sub_max = rows_total // MOE_SUB
        n_sub = (valid + MOE_SUB - 1) // MOE_SUB

        def body(n_live):
            wup_scr[:, 0:tf] = wg_ref[...].astype(BF16)
            wup_scr[:, tf:] = wl_ref[...].astype(BF16)
            wd_scr[...] = wd_ref[...].astype(BF16)
            for s in range(n_live):
                rows = slice(s * MOE_SUB, (s + 1) * MOE_SUB)
                x = _unpack_rows(x_ref[rows, :])
                up = _dot(x, wup_scr[...])
                g = jnp.minimum(up[:, 0:tf] + bg_ref[...], SWIGLU_LIMIT)
                lin = jnp.clip(up[:, tf:] + bl_ref[...], -SWIGLU_LIMIT, SWIGLU_LIMIT)
                act = g * jax.nn.sigmoid(SWIGLU_ALPHA * g) * (lin + 1.0)
                o_ref[rows, :] = o_ref[rows, :] + _dot(act.astype(BF16), wd_scr[...])

        for n_live in range(1, n_sub_max + 1):
            pl.when(n_sub == n_live)(functools.partial(body, n_live))


def _moe(blk_e, blk_valid, n_act, xs, w_up, b_up, w_down, b_down):
    rows, half = xs.shape
    d = 2 * half
    n_exp, _, two_f = w_up.shape[1:]
    ff = two_f // 2
    tf = MOE_FF_TILE
    nf = ff // tf
    nb = rows // MOE_ROWS
    b_up3 = b_up.reshape(n_exp, 1, two_f)
    b_dn3 = b_down.reshape(n_exp, 1, d)

    def blk(b, f, be, bv, na):
        return jnp.minimum(b, na[0] - 1)

    def ftile(b, f, be, bv, na):
        return jnp.where(b < na[0], f, nf - 1)

    return pl.pallas_call(
        functools.partial(_moe_kernel, tf=tf),
        out_shape=jax.ShapeDtypeStruct((rows, d), F32),
        grid_spec=pltpu.PrefetchScalarGridSpec(
            num_scalar_prefetch=3,
            grid=(nb, nf),
            in_specs=[pl.BlockSpec((MOE_ROWS, half), lambda *a: (blk(*a), 0)),
                      pl.BlockSpec((None, None, d, tf), lambda *a: (0, a[2][a[0]], 0, ftile(*a))),
                      pl.BlockSpec((None, None, d, tf), lambda *a: (0, a[2][a[0]], 0, ftile(*a) + nf)),
                      pl.BlockSpec((None, 1, tf), lambda *a: (a[2][a[0]], 0, ftile(*a))),
                      pl.BlockSpec((None, 1, tf), lambda *a: (a[2][a[0]], 0, ftile(*a) + nf)),
                      pl.BlockSpec((None, None, tf, d), lambda *a: (0, a[2][a[0]], ftile(*a), 0)),
                      pl.BlockSpec((None, 1, d), lambda *a: (a[2][a[0]], 0, 0))],
            out_specs=pl.BlockSpec((MOE_ROWS, d), lambda *a: (a[0], 0)),
            scratch_shapes=[pltpu.VMEM((d, 2 * tf), BF16), pltpu.VMEM((tf, d), BF16)]),
        compiler_params=_cparams(("arbitrary", "arbitrary")),
        name="moe_experts",
    )(blk_e, blk_valid, n_act, xs, w_up, w_up, b_up3, b_up3, w_down, b_dn3)


def _combine_kernel(dest_ref, ys_ref, x1_ref, gate_ref, g2_ref, fg_ref, yp_ref, yo_ref, buf, sem, *, n_tok, n_p):
    i = pl.program_id(0)
    tm = x1_ref.shape[0]
    base = i * tm

    def row_copy(t, j):
        d = dest_ref[j * n_tok + base + t]
        return pltpu.make_async_copy(ys_ref.at[pl.ds(d, 1), :], buf.at[j, pl.ds(t, 1), :], sem.at[0])

    def issue(t, carry):
        for j in range(TOP_K):
            row_copy(t, j).start()
        return carry

    def drain(t, carry):
        for j in range(TOP_K):
            row_copy(t, j).wait()
        return carry

    lax.fori_loop(0, tm, issue, 0)
    lax.fori_loop(0, tm, drain, 0)

    moe = gate_ref[:, 0:1] * buf[0]
    for j in range(1, TOP_K):
        moe = moe + gate_ref[:, j:j + 1] * buf[j]
    x1 = x1_ref[...]

    def finish(out_ref):
        for c in range(tm // MOD_CHUNK):
            rows = slice(c * MOD_CHUNK, (c + 1) * MOD_CHUNK)
            xo = x1[rows, :] + g2_ref[c:c + 1, :] * moe[rows, :]
            out_ref[rows, :] = xo * lax.rsqrt(jnp.mean(xo * xo, axis=-1, keepdims=True) + EPS) * fg_ref[...]

    @pl.when(i < n_p)
    def _():
        finish(yp_ref)

    @pl.when(i >= n_p)
    def _():
        finish(yo_ref)


def _combine(dest_flat, ys, x1, gate_c, g2, final_gain, tp):
    t, d = x1.shape
    ts = t - tp
    n_p = tp // COMB_TILE
    n_s = ts // COMB_TILE
    cpt = COMB_TILE // MOD_CHUNK
    kern = functools.partial(_combine_kernel, n_tok=t, n_p=n_p)
    tok = lambda i, *_: (i, 0)
    return pl.pallas_call(
        kern,
        out_shape=(jax.ShapeDtypeStruct((tp, d), F32), jax.ShapeDtypeStruct((ts, d), F32)),
        grid_spec=pltpu.PrefetchScalarGridSpec(
            num_scalar_prefetch=1,
            grid=(n_p + n_s,),
            in_specs=[pl.BlockSpec(memory_space=pl.ANY),
                      pl.BlockSpec((COMB_TILE, d), tok),
                      pl.BlockSpec((COMB_TILE, TOP_K), tok),
                      pl.BlockSpec((cpt, d), tok),
                      pl.BlockSpec((1, d), lambda i, *_: (0, 0))],
            out_specs=(pl.BlockSpec((COMB_TILE, d), lambda i, *_: (jnp.minimum(i, n_p - 1), 0)),
                       pl.BlockSpec((COMB_TILE, d), lambda i, *_: (jnp.maximum(i - n_p, 0), 0))),
            scratch_shapes=[pltpu.VMEM((TOP_K, COMB_TILE, d), F32), pltpu.SemaphoreType.DMA((1,))]),
        compiler_params=_cparams(("arbitrary",)),
        name="combine",
    )(dest_flat, ys, x1, gate_c, g2, final_gain)


def _state_ext(c_state, n_state):
    b, h, dh, _ = c_state.shape
    ct = jnp.swapaxes(c_state, -1, -2)
    pad = jnp.zeros((b, h, dh, LANES - 1), F32)
    return jnp.concatenate([ct, n_state[..., None], pad], axis=-1)


def _state_split(cx):
    dh = cx.shape[2]
    return jnp.swapaxes(cx[..., 0:dh], -1, -2), cx[..., dh]


def kernel(x_prompt, x_sample, state_pool, state_mlstm_C, state_mlstm_n, state_mlstm_m, c_prompt, c_sample,
           w_ada, b_ada, w_in, b_gate, pool_w, pool_scale, mlstm_gain, w_out, w_router, b_router,
           w_up, b_up, w_down, b_down, final_gain):
    depth = w_ada.shape[0]
    assert depth == 1, "single-layer trunk"
    bp, sp, d = x_prompt.shape
    bs, ss, _ = x_sample.shape
    tp, ts = bp * sp, bs * ss
    t = tp + ts
    pw = state_pool.shape[-1]
    heads, dh = state_mlstm_C.shape[2], state_mlstm_C.shape[3]
    mw = heads * dh
    n_exp = w_router.shape[-1]
    pad_rows = state_pool.shape[2]
    assert tp % TOK_TILE == 0 and ts % TOK_TILE == 0 and sp % TOK_TILE == 0
    assert ss % MOD_CHUNK == 0 and 2 * heads <= SUBLANES and pad_rows < POOL_HALO

    xp = x_prompt.reshape(tp, d)
    xs = x_sample.reshape(ts, d)

    n_c = bp + bs
    c_rows = -(-n_c // SUBLANES) * SUBLANES
    c_all = jnp.concatenate([c_prompt, c_sample, jnp.zeros((c_rows - n_c, d), F32)], axis=0)
    mod = _adaln(c_all, w_ada[0], b_ada)
    mod_c = jnp.concatenate([
        jnp.broadcast_to(mod[0:bp, None, :], (bp, sp // MOD_CHUNK, mod.shape[1])).reshape(tp // MOD_CHUNK, -1),
        jnp.broadcast_to(mod[bp:n_c, None, :], (bs, ss // MOD_CHUNK, mod.shape[1])).reshape(ts // MOD_CHUNK, -1)])
    sh1, sc1, g1, sh2, sc2, g2 = [mod_c[:, k * d:(k + 1) * d] for k in range(6)]

    w_in0 = w_in[0]
    o0 = pw
    w_main = jnp.concatenate([w_in0[:, 0:pw], w_in0[:, o0:o0 + mw], w_in0[:, o0 + 2 * mw:o0 + 4 * mw]],
                             axis=1).astype(BF16)
    w_kt = (w_in0[:, o0 + mw:o0 + 2 * mw] * (dh ** -0.5)).T.astype(BF16)
    w_g = w_in0[:, o0 + 4 * mw:]
    ng = w_g.shape[1]
    w_gc = jnp.pad(w_g, ((0, 0), (0, LANES - ng))).astype(BF16)
    w_gt = jnp.pad(w_g.T, ((0, SUBLANES - ng), (0, 0))).astype(BF16)
    b_gc = jnp.pad(b_gate[0], (0, LANES - ng)).reshape(1, LANES)
    b_gr = jnp.pad(b_gate[0], (0, SUBLANES - ng)).reshape(SUBLANES, 1)
    u, q, kt, v, og, gc, gr = _inproj(xp, xs, sc1, sh1, w_main, w_kt, w_gc, w_gt, b_gc, b_gr, pw, mw)

    pool_wb = pool_w[0].astype(BF16)
    zeros_p = jnp.zeros((bp, POOL_HALO, pw), F32)
    st_s = jnp.concatenate([jnp.zeros((bs, POOL_HALO - pad_rows, pw), F32), state_pool[0]], axis=1)
    pool_p = _pool(u, zeros_p, pool_wb, pool_scale, batch=bp, seq=sp, row0=0, tm=TOK_TILE, pos0=0)
    pool_s = _pool(u, st_s, pool_wb, pool_scale, batch=bs, seq=ss, row0=tp, tm=ss, pos0=PAST_LEN)
    new_pool_p = u[0:tp].reshape(bp, sp, pw)[:, sp - pad_rows:]
    new_pool_s = u[tp:].reshape(bs, ss, pw)[:, ss - pad_rows:]

    gain = mlstm_gain[0].reshape(1, mw)
    c0_p = jnp.zeros((bp, heads, dh, dh + LANES), F32)
    m0_p = jnp.zeros((bp, heads, 1, 1), F32)
    chunk_p = MLSTM_CHUNK if sp % MLSTM_CHUNK == 0 else MOD_CHUNK
    ml_p, cx_p, m_p = _mlstm(q, kt, v, og, gc, gr, gain, c0_p, m0_p, batch=bp, seq=sp, chunk=chunk_p,
                             row0=0, kt_per_seq=False)
    c0_s = _state_ext(state_mlstm_C[0], state_mlstm_n[0])
    m0_s = state_mlstm_m[0].reshape(bs, heads, 1, 1)
    kt_s = kt[:, tp:].reshape(mw, bs, ss).transpose(1, 0, 2)
    gr_s = gr[:, tp:].reshape(gr.shape[0], bs, ss).transpose(1, 0, 2)
    ml_s, cx_s, m_s = _mlstm(q, kt_s, v, og, gc, gr_s, gain, c0_s, m0_s, batch=bs, seq=ss, chunk=ss,
                             row0=tp, kt_per_seq=True)
    new_c_p, new_n_p = _state_split(cx_p)
    new_c_s, new_n_s = _state_split(cx_s)

    w_rt = w_router[0].T.astype(BF16)
    b_r = b_router[0].reshape(n_exp, 1)
    x1, h_packed, idx_t, gate_t = _outproj(pool_p, pool_s, ml_p, ml_s, xp, xs, g1, sc2, sh2,
                                           w_out[0].astype(BF16), w_rt, b_r)

    rank_t, cnt = _ranks(idx_t, n_exp)
    counts = cnt[:, 0].astype(I32)
    padded = (counts + MOE_ROWS - 1) // MOE_ROWS * MOE_ROWS
    pend = jnp.cumsum(padded)
    pstart = pend - padded
    hot = idx_t[:, :, None] == jnp.arange(n_exp, dtype=I32)
    dest_flat = (jnp.sum(jnp.where(hot, pstart.astype(I32), 0), axis=-1) + rank_t).reshape(-1)
    n_assign = t * TOP_K
    nb = -(-(n_assign + n_exp * (MOE_ROWS - 1)) // MOE_ROWS)
    n_act = pend[-1] // MOE_ROWS
    blk_ids = jnp.arange(nb, dtype=I32)
    last = jnp.minimum(blk_ids, n_act - 1)
    blk_e = jnp.minimum(jnp.searchsorted(pend, last * MOE_ROWS, side='right'), n_exp - 1).astype(I32)
    blk_valid = jnp.where(blk_ids < n_act,
                          jnp.clip(counts[blk_e] - (last * MOE_ROWS - pstart[blk_e]), 0, MOE_ROWS), 0).astype(I32)
    sub_end = (counts + MOE_SUB - 1) // MOE_SUB * MOE_SUB
    tail = pend[-1:]
    pad_row = jnp.concatenate([pstart + counts, tail]).astype(I32)
    pad_len = jnp.concatenate([sub_end - counts, jnp.zeros((1,), I32)]).astype(I32)
    fill_row = jnp.concatenate([pstart + sub_end, tail]).astype(I32)
    fill_n = jnp.concatenate([(padded - sub_end) // MOE_SUB, (nb * MOE_ROWS - tail) // MOE_SUB]).astype(I32)
    xs_grouped = _dispatch(dest_flat, pad_row, pad_len, fill_row, fill_n, h_packed, nb * MOE_ROWS)
    ys = _moe(blk_e, blk_valid, n_act.reshape(1).astype(I32), xs_grouped, w_up, b_up[0], w_down, b_down[0])

    y_p, y_s = _combine(dest_flat, ys, x1, gate_t.T, g2, final_gain.reshape(1, d), tp)

    return (y_p.reshape(bp, sp, d), y_s.reshape(bs, ss, d),
            new_pool_p[None], new_c_p[None], new_n_p[None], m_p.reshape(1, bp, heads),
            new_pool_s[None], new_c_s[None], new_n_s[None], m_s.reshape(1, bs, heads))
```

```python
import functools

import jax
import jax.numpy as jnp
from jax import lax
from jax.experimental import pallas as pl
from jax.experimental.pallas import tpu as pltpu

F32 = jnp.float32
BF16 = jnp.bfloat16
I32 = jnp.int32
U32 = jnp.uint32

EPS = 1e-6
POOL_WINDOWS = (2, 4, 8, 16)
TOP_K = 4
SWIGLU_LIMIT = 7.0
SWIGLU_ALPHA = 1.702
PAST_LEN = 2048

LANES = 128
SUBLANES = 8
MOD_CHUNK = 64
TOK_TILE = 512
POOL_HALO = 16
MLSTM_CHUNK = 256
ADA_TILE = 1024
MOE_ROWS = 768
MOE_SUB = 256
MOE_FF_TILE = 512
SEG_SHIFT = 4
SEG_CHUNK = 1 << SEG_SHIFT
ROW_UNROLL = 8
VMEM_LIMIT = 56 * 1024 * 1024


def _cparams(sem, vmem=VMEM_LIMIT):
    return pltpu.CompilerParams(dimension_semantics=sem, vmem_limit_bytes=vmem)


def _resident(shape, index_map):
    return pl.BlockSpec(shape, index_map, pipeline_mode=pl.Buffered(1))


def _dot(a, b):
    return jnp.dot(a, b, preferred_element_type=F32)


def _dot_nt(a, b):
    return lax.dot_general(a, b, (((1,), (1,)), ((), ())), preferred_element_type=F32)


def _split3(x):
    hi = x.astype(BF16)
    r1 = x - hi.astype(F32)
    mid = r1.astype(BF16)
    lo = (r1 - mid.astype(F32)).astype(BF16)
    return hi, mid, lo


def _log_sigmoid(x):
    return jnp.minimum(x, 0.0) - jnp.log1p(jnp.exp(-jnp.abs(x)))


def _adaln_kernel(c_ref, w_ref, b_ref, o_ref):
    c = c_ref[...]
    s = (c * jax.nn.sigmoid(c)).astype(BF16)
    o_ref[...] = _dot(s, w_ref[...].astype(BF16)) + b_ref[...]


def _adaln(c_all, w_ada, b_ada):
    rows, d = c_all.shape
    n = w_ada.shape[1]
    return pl.pallas_call(
        _adaln_kernel,
        out_shape=jax.ShapeDtypeStruct((rows, n), F32),
        grid=(n // ADA_TILE,),
        in_specs=[pl.BlockSpec((rows, d), lambda j: (0, 0)),
                  pl.BlockSpec((d, ADA_TILE), lambda j: (0, j)),
                  pl.BlockSpec((1, ADA_TILE), lambda j: (0, j))],
        out_specs=pl.BlockSpec((rows, ADA_TILE), lambda j: (0, j)),
        compiler_params=_cparams(("parallel",)),
        name="adaln",
    )(c_all, w_ada, b_ada)


def _modulated_norm(x, sc_ref, sh_ref, h_scr):
    xn = x * lax.rsqrt(jnp.mean(x * x, axis=-1, keepdims=True) + EPS)
    for c in range(x.shape[0] // MOD_CHUNK):
        rows = slice(c * MOD_CHUNK, (c + 1) * MOD_CHUNK)
        h_scr[rows, :] = (xn[rows, :] * (1.0 + sc_ref[c:c + 1, :]) + sh_ref[c:c + 1, :]).astype(h_scr.dtype)


def _inproj_kernel(xp_ref, xs_ref, sc_ref, sh_ref, w_ref, wkt_ref, wg_ref, wgt_ref, bgc_ref, bgr_ref,
                   u_ref, q_ref, kt_ref, v_ref, og_ref, gc_ref, gr_ref, h_scr, *, n_p, pw, mw):
    i = pl.program_id(0)

    @pl.when(i < n_p)
    def _():
        _modulated_norm(xp_ref[...], sc_ref, sh_ref, h_scr)

    @pl.when(i >= n_p)
    def _():
        _modulated_norm(xs_ref[...], sc_ref, sh_ref, h_scr)

    h = h_scr[...]
    u_ref[...] = _dot(h, w_ref[:, 0:pw])
    q_ref[...] = _dot(h, w_ref[:, pw:pw + mw]).astype(BF16)
    v_ref[...] = _dot(h, w_ref[:, pw + mw:pw + 2 * mw]).astype(BF16)
    og_ref[...] = _dot(h, w_ref[:, pw + 2 * mw:pw + 3 * mw])
    kt_ref[...] = _dot_nt(wkt_ref[...], h).astype(BF16)
    gc_ref[...] = _dot(h, wg_ref[...]) + bgc_ref[...]
    gr_ref[...] = _dot_nt(wgt_ref[...], h) + bgr_ref[...]


def _inproj(xp, xs, sc1, sh1, w_main, w_kt, w_gc, w_gt, b_gc, b_gr, pw, mw):
    tp, d = xp.shape
    ts = xs.shape[0]
    t = tp + ts
    n_p = tp // TOK_TILE
    n_s = ts // TOK_TILE
    cpt = TOK_TILE // MOD_CHUNK
    ng = w_gt.shape[0]
    kern = functools.partial(_inproj_kernel, n_p=n_p, pw=pw, mw=mw)
    tok = lambda i: (i, 0)
    return pl.pallas_call(
        kern,
        out_shape=(jax.ShapeDtypeStruct((t, pw), F32),
                   jax.ShapeDtypeStruct((t, mw), BF16),
                   jax.ShapeDtypeStruct((mw, t), BF16),
                   jax.ShapeDtypeStruct((t, mw), BF16),
                   jax.ShapeDtypeStruct((t, mw), F32),
                   jax.ShapeDtypeStruct((t, LANES), F32),
                   jax.ShapeDtypeStruct((ng, t), F32)),
        grid=(n_p + n_s,),
        in_specs=[pl.BlockSpec((TOK_TILE, d), lambda i: (jnp.minimum(i, n_p - 1), 0)),
                  pl.BlockSpec((TOK_TILE, d), lambda i: (jnp.maximum(i - n_p, 0), 0)),
                  pl.BlockSpec((cpt, d), tok),
                  pl.BlockSpec((cpt, d), tok),
                  _resident(w_main.shape, lambda i: (0, 0)),
                  _resident(w_kt.shape, lambda i: (0, 0)),
                  _resident(w_gc.shape, lambda i: (0, 0)),
                  _resident(w_gt.shape, lambda i: (0, 0)),
                  _resident(b_gc.shape, lambda i: (0, 0)),
                  _resident(b_gr.shape, lambda i: (0, 0))],
        out_specs=(pl.BlockSpec((TOK_TILE, pw), tok),
                   pl.BlockSpec((TOK_TILE, mw), tok),
                   pl.BlockSpec((mw, TOK_TILE), lambda i: (0, i)),
                   pl.BlockSpec((TOK_TILE, mw), tok),
                   pl.BlockSpec((TOK_TILE, mw), tok),
                   pl.BlockSpec((TOK_TILE, LANES), tok),
                   pl.BlockSpec((ng, TOK_TILE), lambda i: (0, i))),
        scratch_shapes=[pltpu.VMEM((TOK_TILE, d), BF16)],
        compiler_params=_cparams(("parallel",)),
        name="inproj",
    )(xp, xs, sc1, sh1, w_main, w_kt, w_gc, w_gt, b_gc, b_gr)


def _pool_kernel(u_ref, st_ref, pw_ref, ps_ref, o_ref, xp_scr, *, tm, pos0, group):
    j = pl.program_id(1)

    @pl.when(j == 0)
    def _():
        xp_scr[0:POOL_HALO, :] = st_ref[...]

    @pl.when(j > 0)
    def _():
        xp_scr[0:POOL_HALO, :] = xp_scr[tm:tm + POOL_HALO, :]

    xp_scr[POOL_HALO:POOL_HALO + tm, :] = u_ref[...]
    pos = pos0 + j * tm + lax.broadcasted_iota(I32, (tm, group), 0)
    for g, w in enumerate(POOL_WINDOWS):
        cs = slice(g * group, (g + 1) * group)
        x = xp_scr[POOL_HALO:POOL_HALO + tm, cs]
        acc = x
        for s in range(1, w):
            acc = acc + xp_scr[POOL_HALO - s:POOL_HALO - s + tm, cs]
        cnt = jnp.minimum(pos + 1, w).astype(F32)
        pooled = acc / cnt - x
        mixed = _dot(pooled.astype(BF16), pw_ref[g]) * ps_ref[:, cs]
        o_ref[:, cs] = mixed.astype(o_ref.dtype)


def _pool(u, state16, pool_w, pool_scale, *, batch, seq, row0, tm, pos0):
    c = u.shape[1]
    group = c // len(POOL_WINDOWS)
    nt = seq // tm
    blk0 = row0 // tm
    kern = functools.partial(_pool_kernel, tm=tm, pos0=pos0, group=group)
    return pl.pallas_call(
        kern,
        out_shape=jax.ShapeDtypeStruct((batch * seq, c), BF16),
        grid=(batch, nt),
        in_specs=[pl.BlockSpec((tm, c), lambda b, j: (blk0 + b * nt + j, 0)),
                  pl.BlockSpec((None, POOL_HALO, c), lambda b, j: (b, 0, 0)),
                  _resident(pool_w.shape, lambda b, j: (0, 0, 0)),
                  _resident(pool_scale.shape, lambda b, j: (0, 0))],
        out_specs=pl.BlockSpec((tm, c), lambda b, j: (b * nt + j, 0)),
        scratch_shapes=[pltpu.VMEM((POOL_HALO + tm, c), F32)],
        compiler_params=_cparams(("parallel", "arbitrary")),
        name="pool",
    )(u, state16, pool_w, pool_scale)


def _mlstm_kernel(q_ref, kt_ref, v_ref, og_ref, gc_ref, gr_ref, gain_ref, c0_ref, m0_ref,
                  o_ref, c_ref, m_ref, *, chunk, heads, dh):
    ci = pl.program_id(1)
    L = chunk

    @pl.when(ci == 0)
    def _():
        c_ref[...] = c0_ref[...]
        m_ref[...] = m0_ref[...]

    gc = gc_ref[...]
    gr = gr_ref[...]
    lf_c = _log_sigmoid(gc)
    lf_r = _log_sigmoid(gr)
    row_i = lax.broadcasted_iota(I32, (L, L), 0)
    col_i = lax.broadcasted_iota(I32, (L, L), 1)
    causal = col_i <= row_i
    tri = causal.astype(BF16)
    tri_t = (row_i <= col_i).astype(BF16)
    f_c = sum(_dot(tri, p) for p in _split3(lf_c))
    f_r = sum(_dot(p, tri_t) for p in _split3(lf_r))
    one_col = (lax.broadcasted_iota(I32, (L, LANES), 1) == 0).astype(BF16)
    neg_inf = jnp.float32(-jnp.inf)

    for h in range(heads):
        hs = slice(h * dh, (h + 1) * dh)
        fc = f_c[:, heads + h:heads + h + 1]
        a_r = gr[h:h + 1, :] - f_r[heads + h:heads + h + 1, :]
        m_prev = m_ref[h]
        cm = jnp.max(jnp.where(causal, a_r, neg_inf), axis=1, keepdims=True)
        m_t = fc + jnp.maximum(m_prev, cm)
        dm = jnp.exp(jnp.where(causal, (fc - m_t) + a_r, neg_inf))
        qh = q_ref[:, hs]
        kth = kt_ref[hs, :]
        vh = v_ref[:, hs]
        sc = _dot(qh, kth) * dm
        inter = jnp.exp(fc + m_prev - m_t)
        cx = c_ref[h]
        g = _dot(qh, cx.astype(BF16))
        num = _dot(sc.astype(BF16), vh) + inter * g[:, 0:dh]
        den = jnp.sum(sc, axis=1, keepdims=True) + inter * g[:, dh:dh + 1]
        hh = num * (1.0 / jnp.maximum(jnp.abs(den), jnp.exp(-m_t)))
        hn = hh * lax.rsqrt(jnp.mean(hh * hh, axis=1, keepdims=True) + EPS) * gain_ref[:, hs]
        o_ref[:, hs] = (jax.nn.sigmoid(og_ref[:, hs]) * hn).astype(o_ref.dtype)

        m_last = m_t[L - 1:L, :]
        f_last = fc[L - 1:L, :]
        w_r = jnp.exp(f_last + a_r - m_last)
        decay = jnp.exp(f_last + m_prev - m_last)
        kw = (kth.astype(F32) * w_r).astype(BF16)
        v_ext = jnp.concatenate([vh, one_col], axis=1)
        c_ref[h] = decay * cx + _dot(kw, v_ext)
        m_ref[h] = m_last


def _mlstm(q, kt, v, og, gc, gr, gain, c0, m0, *, batch, seq, chunk, row0, kt_per_seq):
    mw = q.shape[1]
    heads, dh = c0.shape[1], c0.shape[2]
    nc = seq // chunk
    blk0 = row0 // chunk
    ng = gr.shape[-2]
    tok = lambda b, c: (blk0 + b * nc + c, 0)
    if kt_per_seq:
        kt_spec = pl.BlockSpec((None, mw, chunk), lambda b, c: (b, 0, c))
        gr_spec = pl.BlockSpec((None, ng, chunk), lambda b, c: (b, 0, c))
    else:
        kt_spec = pl.BlockSpec((mw, chunk), lambda b, c: (0, blk0 + b * nc + c))
        gr_spec = pl.BlockSpec((ng, chunk), lambda b, c: (0, blk0 + b * nc + c))
    kern = functools.partial(_mlstm_kernel, chunk=chunk, heads=heads, dh=dh)
    st_spec = pl.BlockSpec((None, heads, dh, dh + LANES), lambda b, c: (b, 0, 0, 0))
    m_spec = pl.BlockSpec((None, heads, 1, 1), lambda b, c: (b, 0, 0, 0))
    return pl.pallas_call(
        kern,
        out_shape=(jax.ShapeDtypeStruct((batch * seq, mw), BF16),
                   jax.ShapeDtypeStruct(c0.shape, F32),
                   jax.ShapeDtypeStruct(m0.shape, F32)),
        grid=(batch, nc),
        in_specs=[pl.BlockSpec((chunk, mw), tok),
                  kt_spec,
                  pl.BlockSpec((chunk, mw), tok),
                  pl.BlockSpec((chunk, mw), tok),
                  pl.BlockSpec((chunk, LANES), tok),
                  gr_spec,
                  _resident(gain.shape, lambda b, c: (0, 0)),
                  st_spec, m_spec],
        out_specs=(pl.BlockSpec((chunk, mw), lambda b, c: (b * nc + c, 0)), st_spec, m_spec),
        compiler_params=_cparams(("parallel", "arbitrary")),
        name="mlstm",
    )(q, kt, v, og, gc, gr, gain, c0, m0)


def _outproj_body(p_ref, m_ref, x_ref, g1_ref, sc_ref, sh_ref, wo_ref, wrt_ref, br_ref,
                  x1_ref, hp_ref, idx_ref, gate_ref, h_scr, *, pw, n_exp):
    mix = _dot(p_ref[...], wo_ref[0:pw, :]) + _dot(m_ref[...], wo_ref[pw:, :])
    x = x_ref[...]
    tm, d = x.shape
    for c in range(tm // MOD_CHUNK):
        rows = slice(c * MOD_CHUNK, (c + 1) * MOD_CHUNK)
        x1_ref[rows, :] = x[rows, :] + g1_ref[c:c + 1, :] * mix[rows, :]
    _modulated_norm(x1_ref[...], sc_ref, sh_ref, h_scr)
    hb = h_scr[...]
    hi = lax.bitcast_convert_type(hb[:, 0:d // 2].astype(F32), U32)
    lo = lax.bitcast_convert_type(hb[:, d // 2:].astype(F32), U32)
    words = (hi & jnp.uint32(0xFFFF0000)) | (lo >> 16)
    n_slab = words.shape[1] // LANES
    for k in range(n_slab):
        hp_ref[pl.ds(k, tm, stride=n_slab), :] = words[:, k * LANES:(k + 1) * LANES]

    logits = _dot_nt(wrt_ref[...], hb) + br_ref[...]
    e_iota = lax.broadcasted_iota(I32, logits.shape, 0)
    vals = []
    for j in range(TOP_K):
        mx = jnp.max(logits, axis=0, keepdims=True)
        ix = jnp.min(jnp.where(logits == mx, e_iota, n_exp), axis=0, keepdims=True)
        idx_ref[j:j + 1, :] = ix
        vals.append(mx)
        logits = jnp.where(e_iota == ix, -jnp.inf, logits)
    ex = [jnp.exp(v - vals[0]) for v in vals]
    tot = ex[0]
    for e in ex[1:]:
        tot = tot + e
    inv = 1.0 / tot
    for j in range(TOP_K):
        gate_ref[j:j + 1, :] = ex[j] * inv


def _outproj_kernel(pp_ref, ps_ref, mp_ref, ms_ref, xp_ref, xs_ref, g1_ref, sc_ref, sh_ref,
                    wo_ref, wrt_ref, br_ref, x1_ref, hp_ref, idx_ref, gate_ref, h_scr, *, n_p, pw, n_exp):
    i = pl.program_id(0)
    rest = (g1_ref, sc_ref, sh_ref, wo_ref, wrt_ref, br_ref, x1_ref, hp_ref, idx_ref, gate_ref, h_scr)

    @pl.when(i < n_p)
    def _():
        _outproj_body(pp_ref, mp_ref, xp_ref, *rest, pw=pw, n_exp=n_exp)

    @pl.when(i >= n_p)
    def _():
        _outproj_body(ps_ref, ms_ref, xs_ref, *rest, pw=pw, n_exp=n_exp)


def _outproj(pool_p, pool_s, ml_p, ml_s, xp, xs, g1, sc2, sh2, w_out, w_rt, b_r):
    tp, d = xp.shape
    ts = xs.shape[0]
    t = tp + ts
    pw = pool_p.shape[1]
    mw = ml_p.shape[1]
    n_exp = w_rt.shape[0]
    n_p = tp // TOK_TILE
    n_s = ts // TOK_TILE
    cpt = TOK_TILE // MOD_CHUNK
    kern = functools.partial(_outproj_kernel, n_p=n_p, pw=pw, n_exp=n_exp)
    tok = lambda i: (i, 0)
    pidx = lambda i: (jnp.minimum(i, n_p - 1), 0)
    sidx = lambda i: (jnp.maximum(i - n_p, 0), 0)
    return pl.pallas_call(
        kern,
        out_shape=(jax.ShapeDtypeStruct((t, d), F32),
                   jax.ShapeDtypeStruct((t * (d // 2 // LANES), LANES), U32),
                   jax.ShapeDtypeStruct((TOP_K, t), I32),
                   jax.ShapeDtypeStruct((TOP_K, t), F32)),
        grid=(n_p + n_s,),
        in_specs=[pl.BlockSpec((TOK_TILE, pw), pidx), pl.BlockSpec((TOK_TILE, pw), sidx),
                  pl.BlockSpec((TOK_TILE, mw), pidx), pl.BlockSpec((TOK_TILE, mw), sidx),
                  pl.BlockSpec((TOK_TILE, d), pidx), pl.BlockSpec((TOK_TILE, d), sidx),
                  pl.BlockSpec((cpt, d), tok), pl.BlockSpec((cpt, d), tok), pl.BlockSpec((cpt, d), tok),
                  _resident(w_out.shape, lambda i: (0, 0)),
                  _resident(w_rt.shape, lambda i: (0, 0)),
                  _resident(b_r.shape, lambda i: (0, 0))],
        out_specs=(pl.BlockSpec((TOK_TILE, d), tok),
                   pl.BlockSpec((TOK_TILE * (d // 2 // LANES), LANES), tok),
                   pl.BlockSpec((TOP_K, TOK_TILE), lambda i: (0, i)),
                   pl.BlockSpec((TOP_K, TOK_TILE), lambda i: (0, i))),
        scratch_shapes=[pltpu.VMEM((TOK_TILE, d), BF16)],
        compiler_params=_cparams(("parallel",)),
        name="outproj",
    )(pool_p, pool_s, ml_p, ml_s, xp, xs, g1, sc2, sh2, w_out, w_rt, b_r)


def _rank_kernel(idx_ref, lpos_ref, cnt_ref, *, n_exp):
    tm = idx_ref.shape[1]
    e_iota = lax.broadcasted_iota(I32, (n_exp, tm), 0)
    hots = [e_iota == idx_ref[j:j + 1, :] for j in range(TOP_K)]
    cnt = hots[0].astype(F32)
    for hot in hots[1:]:
        cnt = cnt + hot.astype(F32)
    cnt_b = cnt.astype(BF16)
    r = lax.broadcasted_iota(I32, (tm, tm), 0)
    c = lax.broadcasted_iota(I32, (tm, tm), 1)
    earlier_tok = _dot(cnt_b, (r < c).astype(BF16))
    er = lax.broadcasted_iota(I32, (n_exp, n_exp), 0)
    ec = lax.broadcasted_iota(I32, (n_exp, n_exp), 1)
    lower_exp = _dot((ec < er).astype(BF16), cnt_b)
    seg_start = jnp.sum(lower_exp, axis=1, keepdims=True)
    pos = earlier_tok + seg_start
    for j in range(TOP_K):
        lpos_ref[j:j + 1, :] = jnp.sum(jnp.where(hots[j], pos, 0.0), axis=0, keepdims=True).astype(I32)
    cnt_ref[...] = jnp.broadcast_to(jnp.sum(cnt, axis=1, keepdims=True), cnt_ref.shape)


def _ranks(idx_t, n_exp):
    k, t = idx_t.shape
    n_tiles = t // TOK_TILE
    return pl.pallas_call(
        functools.partial(_rank_kernel, n_exp=n_exp),
        out_shape=(jax.ShapeDtypeStruct((k, t), I32), jax.ShapeDtypeStruct((n_exp, n_tiles * LANES), F32)),
        grid=(n_tiles,),
        in_specs=[pl.BlockSpec((k, TOK_TILE), lambda i: (0, i))],
        out_specs=(pl.BlockSpec((k, TOK_TILE), lambda i: (0, i)),
                   pl.BlockSpec((n_exp, LANES), lambda i: (0, i))),
        compiler_params=_cparams(("parallel",)),
        name="ranks",
    )(idx_t)


def _tile_rows(first, n):
    start = first * SUBLANES
    if not isinstance(start, int):
        start = pl.multiple_of(start, SUBLANES)
    return pl.ds(start, n * SUBLANES)


def _slab(first, n, k):
    return pl.ds(first * SUBLANES + k, n, stride=SUBLANES)


def _segment_copies(count, make_copy, op):
    n_full = count >> SEG_SHIFT

    def full(k, carry):
        getattr(make_copy(k * SEG_CHUNK, SEG_CHUNK), op)()
        return carry

    lax.fori_loop(0, n_full, full, 0)
    rest = count - n_full * SEG_CHUNK
    base = n_full * SEG_CHUNK
    bit = SEG_CHUNK // 2
    while bit:
        off = base + (rest & -(2 * bit))

        def one(off=off, bit=bit):
            getattr(make_copy(off, bit), op)()

        pl.when((rest & bit) != 0)(one)
        bit //= 2


def _dispatch_kernel(lpos_ref, segc_ref, segl_ref, segg_ref, padrow_ref, padlen_ref, fillrow_ref, filln_ref,
                     h_ref, xs_ref, sorted_scr, zero_scr, sem, *, n_tok, n_exp, n_regions):
    i = pl.program_id(0)
    tm = h_ref.shape[0] // SUBLANES
    base = i * tm

    def place(t, carry):
        row = h_ref[_tile_rows(t, 1), :]
        for j in range(TOP_K):
            dst = pl.multiple_of(lpos_ref[(base + t) * TOP_K + j], SUBLANES)
            sorted_scr[pl.ds(dst, SUBLANES), :] = row
        return carry

    lax.fori_loop(0, tm, place, 0, unroll=ROW_UNROLL)

    def send(e, carry):
        s = i * n_exp + e
        src0 = segl_ref[s]
        dst0 = segg_ref[s]
        _segment_copies(segc_ref[s], lambda off, n: pltpu.make_async_copy(
            sorted_scr.at[_tile_rows(src0 + off, n), :], xs_ref.at[_tile_rows(dst0 + off, n), :], sem.at[0]), "start")
        return carry

    lax.fori_loop(0, n_exp, send, 0)

    @pl.when(i == 0)
    def _():
        zero_scr[...] = jnp.zeros_like(zero_scr)

        def fill_copy(e, c):
            return pltpu.make_async_copy(zero_scr, xs_ref.at[_tile_rows(fillrow_ref[e] + c * MOE_SUB, MOE_SUB), :],
                                         sem.at[1])

        def region(op):
            def go(e, carry):
                dst0 = padrow_ref[e]
                _segment_copies(padlen_ref[e], lambda off, n: pltpu.make_async_copy(
                    zero_scr.at[_tile_rows(0, n), :], xs_ref.at[_tile_rows(dst0 + off, n), :], sem.at[1]), op)
                lax.fori_loop(0, filln_ref[e], lambda k, c: (getattr(fill_copy(e, k), op)(), c)[1], 0)
                return carry
            return go

        lax.fori_loop(0, n_regions, region("start"), 0)
        lax.fori_loop(0, n_regions, region("wait"), 0)

    pltpu.make_async_copy(sorted_scr, xs_ref.at[pl.ds(0, sorted_scr.shape[0]), :], sem.at[0]).wait()


def _dispatch(lpos_flat, seg_cnt, seg_l, seg_g, pad_row, pad_len, fill_row, fill_n, h_tiles, rows, n_exp):
    t = h_tiles.shape[0] // SUBLANES
    kern = functools.partial(_dispatch_kernel, n_tok=t, n_exp=n_exp, n_regions=pad_row.shape[0])
    return pl.pallas_call(
        kern,
        out_shape=jax.ShapeDtypeStruct((rows * SUBLANES, LANES), U32),
        grid_spec=pltpu.PrefetchScalarGridSpec(
            num_scalar_prefetch=8,
            grid=(t // TOK_TILE,),
            in_specs=[pl.BlockSpec((TOK_TILE * SUBLANES, LANES), lambda i, *_: (i, 0))],
            out_specs=pl.BlockSpec(memory_space=pl.ANY),
            scratch_shapes=[pltpu.VMEM((TOK_TILE * TOP_K * SUBLANES, LANES), U32),
                            pltpu.VMEM((MOE_SUB * SUBLANES, LANES), U32),
                            pltpu.SemaphoreType.DMA((2,))]),
        compiler_params=_cparams(("arbitrary",)),
        name="dispatch",
    )(lpos_flat, seg_cnt, seg_l, seg_g, pad_row, pad_len, fill_row, fill_n, h_tiles)


def _unpack_rows(x_ref, first, n):
    his, los = [], []
    for k in range(SUBLANES):
        words = x_ref[_slab(first, n, k), :]
        his.append(lax.bitcast_convert_type(words & jnp.uint32(0xFFFF0000), F32).astype(BF16))
        los.append(lax.bitcast_convert_type(words << 16, F32).astype(BF16))
    return jnp.concatenate(his + los, axis=1)


def _moe_kernel(be_ref, bv_ref, na_ref, x_ref, wg_ref, wl_ref, bg_ref, bl_ref, wd_ref, bd_ref,
                ya_ref, yb_ref, acc_scr, sta_scr, stb_scr, wup_scr, wd_scr, sem, *, tf):
    b = pl.program_id(0)
    f = pl.program_id(1)
    n_blk = pl.num_programs(0)
    last_f = pl.num_programs(1) - 1
    rows_total = x_ref.shape[0] // SUBLANES
    n_sub_max = rows_total // MOE_SUB
    half = SUBLANES * LANES
    active = b < na_ref[0]

    def writeback(blk):
        dst = _tile_rows(blk * rows_total, rows_total)
        return (pltpu.make_async_copy(sta_scr, ya_ref.at[dst, :], sem.at[0]),
                pltpu.make_async_copy(stb_scr, yb_ref.at[dst, :], sem.at[1]))

    @pl.when(jnp.logical_and(f == last_f, b > 0))
    def _():
        for cp in writeback(b - 1):
            cp.wait()

    @pl.when(jnp.logical_and(jnp.logical_not(active), f == last_f))
    def _():
        sta_scr[...] = jnp.zeros_like(sta_scr)
        stb_scr[...] = jnp.zeros_like(stb_scr)

    @pl.when(active)
    def _():
        @pl.when(f == 0)
        def _():
            acc_scr[...] = jnp.broadcast_to(bd_ref[...], acc_scr.shape)

        n_sub = (bv_ref[b] + MOE_SUB - 1) // MOE_SUB

        def body(n_live):
            wup_scr[:, 0:tf] = wg_ref[...].astype(BF16)
            wup_scr[:, tf:] = wl_ref[...].astype(BF16)
            wd_scr[...] = wd_ref[...].astype(BF16)
            for s in range(n_live):
                rows = slice(s * MOE_SUB, (s + 1) * MOE_SUB)
                x = _unpack_rows(x_ref, s * MOE_SUB, MOE_SUB)
                up = _dot(x, wup_scr[...])
                g = jnp.minimum(up[:, 0:tf] + bg_ref[...], SWIGLU_LIMIT)
                lin = jnp.clip(up[:, tf:] + bl_ref[...], -SWIGLU_LIMIT, SWIGLU_LIMIT)
                act = g * jax.nn.sigmoid(SWIGLU_ALPHA * g) * (lin + 1.0)
                acc_scr[rows, :] = acc_scr[rows, :] + _dot(act.astype(BF16), wd_scr[...])

        for n_live in range(1, n_sub_max + 1):
            pl.when(n_sub == n_live)(functools.partial(body, n_live))

        @pl.when(f == last_f)
        def _():
            for s in range(n_sub_max):
                rows = slice(s * MOE_SUB, (s + 1) * MOE_SUB)
                for k in range(SUBLANES):
                    slab = _slab(s * MOE_SUB, MOE_SUB, k)
                    sta_scr[slab, :] = acc_scr[rows, k * LANES:(k + 1) * LANES]
                    stb_scr[slab, :] = acc_scr[rows, half + k * LANES:half + (k + 1) * LANES]

    @pl.when(f == last_f)
    def _():
        for cp in writeback(b):
            cp.start()

        @pl.when(b == n_blk - 1)
        def _():
            for cp in writeback(b):
                cp.wait()


def _moe(blk_e, blk_valid, n_act, xs, w_up, b_up, w_down, b_down):
    rows = xs.shape[0] // SUBLANES
    d = 2 * SUBLANES * LANES
    n_exp, _, two_f = w_up.shape[1:]
    assert w_up.shape[2] == d, "one (8, 128) tile of packed words per row"
    ff = two_f // 2
    tf = MOE_FF_TILE
    nf = ff // tf
    nb = rows // MOE_ROWS
    b_up3 = b_up.reshape(n_exp, 1, two_f)
    b_dn3 = b_down.reshape(n_exp, 1, d)

    def blk(b, f, be, bv, na):
        return jnp.minimum(b, na[0] - 1)

    def ftile(b, f, be, bv, na):
        return jnp.where(b < na[0], f, nf - 1)

    y_shape = jax.ShapeDtypeStruct((rows * SUBLANES, LANES), F32)
    return pl.pallas_call(
        functools.partial(_moe_kernel, tf=tf),
        out_shape=(y_shape, y_shape),
        grid_spec=pltpu.PrefetchScalarGridSpec(
            num_scalar_prefetch=3,
            grid=(nb, nf),
            in_specs=[pl.BlockSpec((MOE_ROWS * SUBLANES, LANES), lambda *a: (blk(*a), 0)),
                      pl.BlockSpec((None, None, d, tf), lambda *a: (0, a[2][a[0]], 0, ftile(*a))),
                      pl.BlockSpec((None, None, d, tf), lambda *a: (0, a[2][a[0]], 0, ftile(*a) + nf)),
                      pl.BlockSpec((None, 1, tf), lambda *a: (a[2][a[0]], 0, ftile(*a))),
                      pl.BlockSpec((None, 1, tf), lambda *a: (a[2][a[0]], 0, ftile(*a) + nf)),
                      pl.BlockSpec((None, None, tf, d), lambda *a: (0, a[2][a[0]], ftile(*a), 0)),
                      pl.BlockSpec((None, 1, d), lambda *a: (a[2][a[0]], 0, 0))],
            out_specs=(pl.BlockSpec(memory_space=pl.ANY), pl.BlockSpec(memory_space=pl.ANY)),
            scratch_shapes=[pltpu.VMEM((MOE_ROWS, d), F32),
                            pltpu.VMEM((MOE_ROWS * SUBLANES, LANES), F32),
                            pltpu.VMEM((MOE_ROWS * SUBLANES, LANES), F32),
                            pltpu.VMEM((d, 2 * tf), BF16), pltpu.VMEM((tf, d), BF16),
                            pltpu.SemaphoreType.DMA((2,))]),
        compiler_params=_cparams(("arbitrary", "arbitrary")),
        name="moe_experts",
    )(blk_e, blk_valid, n_act, xs, w_up, w_up, b_up3, b_up3, w_down, b_dn3)


def _combine_kernel(lpos_ref, segc_ref, segl_ref, segg_ref, ya_ref, yb_ref, x1_ref, gate_ref, g2_ref, fg_ref,
                    yp_ref, yo_ref, bufa, bufb, outa, outb, sem, *, n_tok, n_exp, n_p):
    i = pl.program_id(0)
    tm = x1_ref.shape[0]
    base = i * tm

    def fetch(e, carry):
        s = i * n_exp + e
        src0 = segg_ref[s]
        dst0 = segl_ref[s]
        _segment_copies(segc_ref[s], lambda off, n: pltpu.make_async_copy(
            ya_ref.at[_tile_rows(src0 + off, n), :], bufa.at[_tile_rows(dst0 + off, n), :], sem.at[0]), "start")
        _segment_copies(segc_ref[s], lambda off, n: pltpu.make_async_copy(
            yb_ref.at[_tile_rows(src0 + off, n), :], bufb.at[_tile_rows(dst0 + off, n), :], sem.at[1]), "start")
        return carry

    lax.fori_loop(0, n_exp, fetch, 0)
    pltpu.make_async_copy(ya_ref.at[pl.ds(0, bufa.shape[0]), :], bufa, sem.at[0]).wait()
    pltpu.make_async_copy(yb_ref.at[pl.ds(0, bufb.shape[0]), :], bufb, sem.at[1]).wait()

    def mix(t, carry):
        acc_a = acc_b = None
        for j in range(TOP_K):
            src = pl.ds(pl.multiple_of(lpos_ref[(base + t) * TOP_K + j], SUBLANES), SUBLANES)
            g = gate_ref[j, pl.ds(t, 1), :]
            ta = g * bufa[src, :]
            tb = g * bufb[src, :]
            acc_a = ta if acc_a is None else acc_a + ta
            acc_b = tb if acc_b is None else acc_b + tb
        outa[_tile_rows(t, 1), :] = acc_a
        outb[_tile_rows(t, 1), :] = acc_b
        return carry

    lax.fori_loop(0, tm, mix, 0, unroll=ROW_UNROLL)

    def finish(out_ref):
        for c in range(tm // MOD_CHUNK):
            rows = slice(c * MOD_CHUNK, (c + 1) * MOD_CHUNK)
            moe = jnp.concatenate([outa[_slab(c * MOD_CHUNK, MOD_CHUNK, k), :] for k in range(SUBLANES)] +
                                  [outb[_slab(c * MOD_CHUNK, MOD_CHUNK, k), :] for k in range(SUBLANES)], axis=1)
            xo = x1_ref[rows, :] + g2_ref[c:c + 1, :] * moe
            out_ref[rows, :] = xo * lax.rsqrt(jnp.mean(xo * xo, axis=-1, keepdims=True) + EPS) * fg_ref[...]

    @pl.when(i < n_p)
    def _():
        finish(yp_ref)

    @pl.when(i >= n_p)
    def _():
        finish(yo_ref)


def _combine(lpos_flat, seg_cnt, seg_l, seg_g, ys_a, ys_b, x1, gate_x, g2, final_gain, tp, n_exp):
    t, d = x1.shape
    lanes = LANES
    ts = t - tp
    n_p = tp // TOK_TILE
    n_s = ts // TOK_TILE
    cpt = TOK_TILE // MOD_CHUNK
    kern = functools.partial(_combine_kernel, n_tok=t, n_exp=n_exp, n_p=n_p)
    tok = lambda i, *_: (i, 0)
    return pl.pallas_call(
        kern,
        out_shape=(jax.ShapeDtypeStruct((tp, d), F32), jax.ShapeDtypeStruct((ts, d), F32)),
        grid_spec=pltpu.PrefetchScalarGridSpec(
            num_scalar_prefetch=4,
            grid=(n_p + n_s,),
            in_specs=[pl.BlockSpec(memory_space=pl.ANY),
                      pl.BlockSpec(memory_space=pl.ANY),
                      pl.BlockSpec((TOK_TILE, d), tok),
                      pl.BlockSpec((TOP_K, TOK_TILE, lanes), lambda i, *_: (0, i, 0)),
                      pl.BlockSpec((cpt, d), tok),
                      pl.BlockSpec((1, d), lambda i, *_: (0, 0))],
            out_specs=(pl.BlockSpec((TOK_TILE, d), lambda i, *_: (jnp.minimum(i, n_p - 1), 0)),
                       pl.BlockSpec((TOK_TILE, d), lambda i, *_: (jnp.maximum(i - n_p, 0), 0))),
            scratch_shapes=[pltpu.VMEM((TOK_TILE * TOP_K * SUBLANES, lanes), F32),
                            pltpu.VMEM((TOK_TILE * TOP_K * SUBLANES, lanes), F32),
                            pltpu.VMEM((TOK_TILE * SUBLANES, lanes), F32),
                            pltpu.VMEM((TOK_TILE * SUBLANES, lanes), F32),
                            pltpu.SemaphoreType.DMA((2,))]),
        compiler_params=_cparams(("arbitrary",)),
        name="combine",
    )(lpos_flat, seg_cnt, seg_l, seg_g, ys_a, ys_b, x1, gate_x, g2, final_gain)


def _state_ext(c_state, n_state):
    b, h, dh, _ = c_state.shape
    ct = jnp.swapaxes(c_state, -1, -2)
    pad = jnp.zeros((b, h, dh, LANES - 1), F32)
    return jnp.concatenate([ct, n_state[..., None], pad], axis=-1)


def _state_split(cx):
    dh = cx.shape[2]
    return jnp.swapaxes(cx[..., 0:dh], -1, -2), cx[..., dh]


def kernel(x_prompt, x_sample, state_pool, state_mlstm_C, state_mlstm_n, state_mlstm_m, c_prompt, c_sample,
           w_ada, b_ada, w_in, b_gate, pool_w, pool_scale, mlstm_gain, w_out, w_router, b_router,
           w_up, b_up, w_down, b_down, final_gain):
    depth = w_ada.shape[0]
    assert depth == 1, "single-layer trunk"
    bp, sp, d = x_prompt.shape
    bs, ss, _ = x_sample.shape
    tp, ts = bp * sp, bs * ss
    t = tp + ts
    pw = state_pool.shape[-1]
    heads, dh = state_mlstm_C.shape[2], state_mlstm_C.shape[3]
    mw = heads * dh
    n_exp = w_router.shape[-1]
    pad_rows = state_pool.shape[2]
    assert tp % TOK_TILE == 0 and ts % TOK_TILE == 0 and sp % TOK_TILE == 0
    assert ss % MOD_CHUNK == 0 and 2 * heads <= SUBLANES and pad_rows < POOL_HALO

    xp = x_prompt.reshape(tp, d)
    xs = x_sample.reshape(ts, d)

    n_c = bp + bs
    c_rows = -(-n_c // SUBLANES) * SUBLANES
    c_all = jnp.concatenate([c_prompt, c_sample, jnp.zeros((c_rows - n_c, d), F32)], axis=0)
    mod = _adaln(c_all, w_ada[0], b_ada)
    mod_c = jnp.concatenate([
        jnp.broadcast_to(mod[0:bp, None, :], (bp, sp // MOD_CHUNK, mod.shape[1])).reshape(tp // MOD_CHUNK, -1),
        jnp.broadcast_to(mod[bp:n_c, None, :], (bs, ss // MOD_CHUNK, mod.shape[1])).reshape(ts // MOD_CHUNK, -1)])
    sh1, sc1, g1, sh2, sc2, g2 = [mod_c[:, k * d:(k + 1) * d] for k in range(6)]

    w_in0 = w_in[0]
    o0 = pw
    w_main = jnp.concatenate([w_in0[:, 0:pw], w_in0[:, o0:o0 + mw], w_in0[:, o0 + 2 * mw:o0 + 4 * mw]],
                             axis=1).astype(BF16)
    w_kt = (w_in0[:, o0 + mw:o0 + 2 * mw] * (dh ** -0.5)).T.astype(BF16)
    w_g = w_in0[:, o0 + 4 * mw:]
    ng = w_g.shape[1]
    w_gc = jnp.pad(w_g, ((0, 0), (0, LANES - ng))).astype(BF16)
    w_gt = jnp.pad(w_g.T, ((0, SUBLANES - ng), (0, 0))).astype(BF16)
    b_gc = jnp.pad(b_gate[0], (0, LANES - ng)).reshape(1, LANES)
    b_gr = jnp.pad(b_gate[0], (0, SUBLANES - ng)).reshape(SUBLANES, 1)
    u, q, kt, v, og, gc, gr = _inproj(xp, xs, sc1, sh1, w_main, w_kt, w_gc, w_gt, b_gc, b_gr, pw, mw)

    pool_wb = pool_w[0].astype(BF16)
    zeros_p = jnp.zeros((bp, POOL_HALO, pw), F32)
    st_s = jnp.concatenate([jnp.zeros((bs, POOL_HALO - pad_rows, pw), F32), state_pool[0]], axis=1)
    pool_p = _pool(u, zeros_p, pool_wb, pool_scale, batch=bp, seq=sp, row0=0, tm=TOK_TILE, pos0=0)
    pool_s = _pool(u, st_s, pool_wb, pool_scale, batch=bs, seq=ss, row0=tp, tm=ss, pos0=PAST_LEN)
    new_pool_p = u[0:tp].reshape(bp, sp, pw)[:, sp - pad_rows:]
    new_pool_s = u[tp:].reshape(bs, ss, pw)[:, ss - pad_rows:]

    gain = mlstm_gain[0].reshape(1, mw)
    c0_p = jnp.zeros((bp, heads, dh, dh + LANES), F32)
    m0_p = jnp.zeros((bp, heads, 1, 1), F32)
    chunk_p = MLSTM_CHUNK if sp % MLSTM_CHUNK == 0 else MOD_CHUNK
    ml_p, cx_p, m_p = _mlstm(q, kt, v, og, gc, gr, gain, c0_p, m0_p, batch=bp, seq=sp, chunk=chunk_p,
                             row0=0, kt_per_seq=False)
    c0_s = _state_ext(state_mlstm_C[0], state_mlstm_n[0])
    m0_s = state_mlstm_m[0].reshape(bs, heads, 1, 1)
    kt_s = kt[:, tp:].reshape(mw, bs, ss).transpose(1, 0, 2)
    gr_s = gr[:, tp:].reshape(gr.shape[0], bs, ss).transpose(1, 0, 2)
    ml_s, cx_s, m_s = _mlstm(q, kt_s, v, og, gc, gr_s, gain, c0_s, m0_s, batch=bs, seq=ss, chunk=ss,
                             row0=tp, kt_per_seq=True)
    new_c_p, new_n_p = _state_split(cx_p)
    new_c_s, new_n_s = _state_split(cx_s)

    w_rt = w_router[0].T.astype(BF16)
    b_r = b_router[0].reshape(n_exp, 1)
    x1, h_packed, idx_t, gate_t = _outproj(pool_p, pool_s, ml_p, ml_s, xp, xs, g1, sc2, sh2,
                                           w_out[0].astype(BF16), w_rt, b_r)

    lpos_t, cnt = _ranks(idx_t, n_exp)
    lpos_flat = (lpos_t.T * SUBLANES).reshape(-1)
    c_te = cnt[:, ::LANES].T.astype(I32)
    counts = jnp.sum(c_te, axis=0)
    padded = (counts + MOE_ROWS - 1) // MOE_ROWS * MOE_ROWS
    pend = jnp.cumsum(padded)
    pstart = pend - padded
    seg_cnt = c_te.reshape(-1)
    seg_l = (jnp.cumsum(c_te, axis=1) - c_te).reshape(-1).astype(I32)
    seg_g = (pstart[None, :] + jnp.cumsum(c_te, axis=0) - c_te).reshape(-1).astype(I32)
    n_assign = t * TOP_K
    nb = -(-(n_assign + n_exp * (MOE_ROWS - 1)) // MOE_ROWS)
    n_act = pend[-1] // MOE_ROWS
    blk_ids = jnp.arange(nb, dtype=I32)
    last = jnp.minimum(blk_ids, n_act - 1)
    blk_e = jnp.minimum(jnp.searchsorted(pend, last * MOE_ROWS, side='right'), n_exp - 1).astype(I32)
    blk_valid = jnp.where(blk_ids < n_act,
                          jnp.clip(counts[blk_e] - (last * MOE_ROWS - pstart[blk_e]), 0, MOE_ROWS), 0).astype(I32)
    sub_end = (counts + MOE_SUB - 1) // MOE_SUB * MOE_SUB
    tail = pend[-1:]
    pad_row = jnp.concatenate([pstart + counts, tail]).astype(I32)
    pad_len = jnp.concatenate([sub_end - counts, jnp.zeros((1,), I32)]).astype(I32)
    fill_row = jnp.concatenate([pstart + sub_end, tail]).astype(I32)
    fill_n = jnp.concatenate([(padded - sub_end) // MOE_SUB, (nb * MOE_ROWS - tail) // MOE_SUB]).astype(I32)
    xs_grouped = _dispatch(lpos_flat, seg_cnt, seg_l, seg_g, pad_row, pad_len, fill_row, fill_n, h_packed,
                           nb * MOE_ROWS, n_exp)
    ys_a, ys_b = _moe(blk_e, blk_valid, n_act.reshape(1).astype(I32), xs_grouped, w_up, b_up[0], w_down, b_down[0])

    gate_x = jnp.broadcast_to(gate_t[:, :, None], (TOP_K, t, LANES))
    y_p, y_s = _combine(lpos_flat, seg_cnt, seg_l, seg_g, ys_a, ys_b, x1, gate_x, g2, final_gain.reshape(1, d),
                        tp, n_exp)

    return (y_p.reshape(bp, sp, d), y_s.reshape(bs, ss, d),
            new_pool_p[None], new_c_p[None], new_n_p[None], m_p.reshape(1, bp, heads),
            new_pool_s[None], new_c_s[None], new_n_s[None], m_s.reshape(1, bs, heads))
```

```python
import functools

import jax
import jax.numpy as jnp
from jax import lax
from jax.experimental import pallas as pl
from jax.experimental.pallas import tpu as pltpu

F32 = jnp.float32
BF16 = jnp.bfloat16
I32 = jnp.int32
U32 = jnp.uint32

EPS = 1e-6
POOL_WINDOWS = (2, 4, 8, 16)
TOP_K = 4
SWIGLU_LIMIT = 7.0
SWIGLU_ALPHA = 1.702
PAST_LEN = 2048

LANES = 128
SUBLANES = 8
MOD_CHUNK = 64
TOK_TILE = 512
POOL_HALO = 16
MLSTM_CHUNK = 256
ADA_TILE = 1024
MOE_ROWS = 768
MOE_SUB = 256
MOE_FF_TILE = 512
SEG_SHIFT = 4
SEG_CHUNK = 1 << SEG_SHIFT
ROW_UNROLL = 8
VMEM_LIMIT = 56 * 1024 * 1024


def _cparams(sem, vmem=VMEM_LIMIT):
    return pltpu.CompilerParams(dimension_semantics=sem, vmem_limit_bytes=vmem)


def _resident(shape, index_map):
    return pl.BlockSpec(shape, index_map, pipeline_mode=pl.Buffered(1))


def _dot(a, b):
    return jnp.dot(a, b, preferred_element_type=F32)


def _dot_nt(a, b):
    return lax.dot_general(a, b, (((1,), (1,)), ((), ())), preferred_element_type=F32)


def _split3(x):
    hi = x.astype(BF16)
    r1 = x - hi.astype(F32)
    mid = r1.astype(BF16)
    lo = (r1 - mid.astype(F32)).astype(BF16)
    return hi, mid, lo


def _log_sigmoid(x):
    return jnp.minimum(x, 0.0) - jnp.log1p(jnp.exp(-jnp.abs(x)))


def _adaln_kernel(c_ref, w_ref, b_ref, o_ref):
    c = c_ref[...]
    s = (c * jax.nn.sigmoid(c)).astype(BF16)
    o_ref[...] = _dot(s, w_ref[...].astype(BF16)) + b_ref[...]


def _adaln(c_all, w_ada, b_ada):
    rows, d = c_all.shape
    n = w_ada.shape[1]
    return pl.pallas_call(
        _adaln_kernel,
        out_shape=jax.ShapeDtypeStruct((rows, n), F32),
        grid=(n // ADA_TILE,),
        in_specs=[pl.BlockSpec((rows, d), lambda j: (0, 0)),
                  pl.BlockSpec((d, ADA_TILE), lambda j: (0, j)),
                  pl.BlockSpec((1, ADA_TILE), lambda j: (0, j))],
        out_specs=pl.BlockSpec((rows, ADA_TILE), lambda j: (0, j)),
        compiler_params=_cparams(("parallel",)),
        name="adaln",
    )(c_all, w_ada, b_ada)


def _modulated_norm(x, sc_ref, sh_ref, h_scr):
    xn = x * lax.rsqrt(jnp.mean(x * x, axis=-1, keepdims=True) + EPS)
    for c in range(x.shape[0] // MOD_CHUNK):
        rows = slice(c * MOD_CHUNK, (c + 1) * MOD_CHUNK)
        h_scr[rows, :] = (xn[rows, :] * (1.0 + sc_ref[c:c + 1, :]) + sh_ref[c:c + 1, :]).astype(h_scr.dtype)


def _inproj_kernel(xp_ref, xs_ref, sc_ref, sh_ref, w_ref, wkt_ref, wg_ref, wgt_ref, bgc_ref, bgr_ref,
                   u_ref, q_ref, kt_ref, v_ref, og_ref, gc_ref, gr_ref, h_scr, *, n_p, pw, mw):
    i = pl.program_id(0)

    @pl.when(i < n_p)
    def _():
        _modulated_norm(xp_ref[...], sc_ref, sh_ref, h_scr)

    @pl.when(i >= n_p)
    def _():
        _modulated_norm(xs_ref[...], sc_ref, sh_ref, h_scr)

    h = h_scr[...]
    u_ref[...] = _dot(h, w_ref[:, 0:pw])
    q_ref[...] = _dot(h, w_ref[:, pw:pw + mw]).astype(BF16)
    v_ref[...] = _dot(h, w_ref[:, pw + mw:pw + 2 * mw]).astype(BF16)
    og_ref[...] = _dot(h, w_ref[:, pw + 2 * mw:pw + 3 * mw])
    kt_ref[...] = _dot_nt(wkt_ref[...], h).astype(BF16)
    gc_ref[...] = _dot(h, wg_ref[...]) + bgc_ref[...]
    gr_ref[...] = _dot_nt(wgt_ref[...], h) + bgr_ref[...]


def _inproj(xp, xs, sc1, sh1, w_main, w_kt, w_gc, w_gt, b_gc, b_gr, pw, mw):
    tp, d = xp.shape
    ts = xs.shape[0]
    t = tp + ts
    n_p = tp // TOK_TILE
    n_s = ts // TOK_TILE
    cpt = TOK_TILE // MOD_CHUNK
    ng = w_gt.shape[0]
    kern = functools.partial(_inproj_kernel, n_p=n_p, pw=pw, mw=mw)
    tok = lambda i: (i, 0)
    return pl.pallas_call(
        kern,
        out_shape=(jax.ShapeDtypeStruct((t, pw), F32),
                   jax.ShapeDtypeStruct((t, mw), BF16),
                   jax.ShapeDtypeStruct((mw, t), BF16),
                   jax.ShapeDtypeStruct((t, mw), BF16),
                   jax.ShapeDtypeStruct((t, mw), F32),
                   jax.ShapeDtypeStruct((t, LANES), F32),
                   jax.ShapeDtypeStruct((ng, t), F32)),
        grid=(n_p + n_s,),
        in_specs=[pl.BlockSpec((TOK_TILE, d), lambda i: (jnp.minimum(i, n_p - 1), 0)),
                  pl.BlockSpec((TOK_TILE, d), lambda i: (jnp.maximum(i - n_p, 0), 0)),
                  pl.BlockSpec((cpt, d), tok),
                  pl.BlockSpec((cpt, d), tok),
                  _resident(w_main.shape, lambda i: (0, 0)),
                  _resident(w_kt.shape, lambda i: (0, 0)),
                  _resident(w_gc.shape, lambda i: (0, 0)),
                  _resident(w_gt.shape, lambda i: (0, 0)),
                  _resident(b_gc.shape, lambda i: (0, 0)),
                  _resident(b_gr.shape, lambda i: (0, 0))],
        out_specs=(pl.BlockSpec((TOK_TILE, pw), tok),
                   pl.BlockSpec((TOK_TILE, mw), tok),
                   pl.BlockSpec((mw, TOK_TILE), lambda i: (0, i)),
                   pl.BlockSpec((TOK_TILE, mw), tok),
                   pl.BlockSpec((TOK_TILE, mw), tok),
                   pl.BlockSpec((TOK_TILE, LANES), tok),
                   pl.BlockSpec((ng, TOK_TILE), lambda i: (0, i))),
        scratch_shapes=[pltpu.VMEM((TOK_TILE, d), BF16)],
        compiler_params=_cparams(("parallel",)),
        name="inproj",
    )(xp, xs, sc1, sh1, w_main, w_kt, w_gc, w_gt, b_gc, b_gr)


def _pool_kernel(u_ref, st_ref, pw_ref, ps_ref, o_ref, xp_scr, *, tm, pos0, group):
    j = pl.program_id(1)

    @pl.when(j == 0)
    def _():
        xp_scr[0:POOL_HALO, :] = st_ref[...]

    @pl.when(j > 0)
    def _():
        xp_scr[0:POOL_HALO, :] = xp_scr[tm:tm + POOL_HALO, :]

    xp_scr[POOL_HALO:POOL_HALO + tm, :] = u_ref[...]
    pos = pos0 + j * tm + lax.broadcasted_iota(I32, (tm, group), 0)
    for g, w in enumerate(POOL_WINDOWS):
        cs = slice(g * group, (g + 1) * group)
        x = xp_scr[POOL_HALO:POOL_HALO + tm, cs]
        acc = x
        for s in range(1, w):
            acc = acc + xp_scr[POOL_HALO - s:POOL_HALO - s + tm, cs]
        cnt = jnp.minimum(pos + 1, w).astype(F32)
        pooled = acc / cnt - x
        mixed = _dot(pooled.astype(BF16), pw_ref[g]) * ps_ref[:, cs]
        o_ref[:, cs] = mixed.astype(o_ref.dtype)


def _pool(u, state16, pool_w, pool_scale, *, batch, seq, row0, tm, pos0):
    c = u.shape[1]
    group = c // len(POOL_WINDOWS)
    nt = seq // tm
    blk0 = row0 // tm
    kern = functools.partial(_pool_kernel, tm=tm, pos0=pos0, group=group)
    return pl.pallas_call(
        kern,
        out_shape=jax.ShapeDtypeStruct((batch * seq, c), BF16),
        grid=(batch, nt),
        in_specs=[pl.BlockSpec((tm, c), lambda b, j: (blk0 + b * nt + j, 0)),
                  pl.BlockSpec((None, POOL_HALO, c), lambda b, j: (b, 0, 0)),
                  _resident(pool_w.shape, lambda b, j: (0, 0, 0)),
                  _resident(pool_scale.shape, lambda b, j: (0, 0))],
        out_specs=pl.BlockSpec((tm, c), lambda b, j: (b * nt + j, 0)),
        scratch_shapes=[pltpu.VMEM((POOL_HALO + tm, c), F32)],
        compiler_params=_cparams(("parallel", "arbitrary")),
        name="pool",
    )(u, state16, pool_w, pool_scale)


def _mlstm_kernel(q_ref, kt_ref, v_ref, og_ref, gc_ref, gr_ref, gain_ref, c0_ref, m0_ref,
                  o_ref, c_ref, m_ref, *, chunk, heads, dh):
    ci = pl.program_id(1)
    L = chunk

    @pl.when(ci == 0)
    def _():
        c_ref[...] = c0_ref[...]
        m_ref[...] = m0_ref[...]

    gc = gc_ref[...]
    gr = gr_ref[...]
    lf_c = _log_sigmoid(gc)
    lf_r = _log_sigmoid(gr)
    row_i = lax.broadcasted_iota(I32, (L, L), 0)
    col_i = lax.broadcasted_iota(I32, (L, L), 1)
    causal = col_i <= row_i
    tri = causal.astype(BF16)
    tri_t = (row_i <= col_i).astype(BF16)
    f_c = sum(_dot(tri, p) for p in _split3(lf_c))
    f_r = sum(_dot(p, tri_t) for p in _split3(lf_r))
    one_col = (lax.broadcasted_iota(I32, (L, LANES), 1) == 0).astype(BF16)
    neg_inf = jnp.float32(-jnp.inf)

    for h in range(heads):
        hs = slice(h * dh, (h + 1) * dh)
        fc = f_c[:, heads + h:heads + h + 1]
        a_r = gr[h:h + 1, :] - f_r[heads + h:heads + h + 1, :]
        m_prev = m_ref[h]
        cm = jnp.max(jnp.where(causal, a_r, neg_inf), axis=1, keepdims=True)
        m_t = fc + jnp.maximum(m_prev, cm)
        dm = jnp.exp(jnp.where(causal, (fc - m_t) + a_r, neg_inf))
        qh = q_ref[:, hs]
        kth = kt_ref[hs, :]
        vh = v_ref[:, hs]
        sc = _dot(qh, kth) * dm
        inter = jnp.exp(fc + m_prev - m_t)
        cx = c_ref[h]
        g = _dot(qh, cx.astype(BF16))
        num = _dot(sc.astype(BF16), vh) + inter * g[:, 0:dh]
        den = jnp.sum(sc, axis=1, keepdims=True) + inter * g[:, dh:dh + 1]
        hh = num * (1.0 / jnp.maximum(jnp.abs(den), jnp.exp(-m_t)))
        hn = hh * lax.rsqrt(jnp.mean(hh * hh, axis=1, keepdims=True) + EPS) * gain_ref[:, hs]
        o_ref[:, hs] = (jax.nn.sigmoid(og_ref[:, hs]) * hn).astype(o_ref.dtype)

        m_last = m_t[L - 1:L, :]
        f_last = fc[L - 1:L, :]
        w_r = jnp.exp(f_last + a_r - m_last)
        decay = jnp.exp(f_last + m_prev - m_last)
        kw = (kth.astype(F32) * w_r).astype(BF16)
        v_ext = jnp.concatenate([vh, one_col], axis=1)
        c_ref[h] = decay * cx + _dot(kw, v_ext)
        m_ref[h] = m_last


def _mlstm(q, kt, v, og, gc, gr, gain, c0, m0, *, batch, seq, chunk, row0, kt_per_seq):
    mw = q.shape[1]
    heads, dh = c0.shape[1], c0.shape[2]
    nc = seq // chunk
    blk0 = row0 // chunk
    ng = gr.shape[-2]
    tok = lambda b, c: (blk0 + b * nc + c, 0)
    if kt_per_seq:
        kt_spec = pl.BlockSpec((None, mw, chunk), lambda b, c: (b, 0, c))
        gr_spec = pl.BlockSpec((None, ng, chunk), lambda b, c: (b, 0, c))
    else:
        kt_spec = pl.BlockSpec((mw, chunk), lambda b, c: (0, blk0 + b * nc + c))
        gr_spec = pl.BlockSpec((ng, chunk), lambda b, c: (0, blk0 + b * nc + c))
    kern = functools.partial(_mlstm_kernel, chunk=chunk, heads=heads, dh=dh)
    st_spec = pl.BlockSpec((None, heads, dh, dh + LANES), lambda b, c: (b, 0, 0, 0))
    m_spec = pl.BlockSpec((None, heads, 1, 1), lambda b, c: (b, 0, 0, 0))
    return pl.pallas_call(
        kern,
        out_shape=(jax.ShapeDtypeStruct((batch * seq, mw), BF16),
                   jax.ShapeDtypeStruct(c0.shape, F32),
                   jax.ShapeDtypeStruct(m0.shape, F32)),
        grid=(batch, nc),
        in_specs=[pl.BlockSpec((chunk, mw), tok),
                  kt_spec,
                  pl.BlockSpec((chunk, mw), tok),
                  pl.BlockSpec((chunk, mw), tok),
                  pl.BlockSpec((chunk, LANES), tok),
                  gr_spec,
                  _resident(gain.shape, lambda b, c: (0, 0)),
                  st_spec, m_spec],
        out_specs=(pl.BlockSpec((chunk, mw), lambda b, c: (b * nc + c, 0)), st_spec, m_spec),
        compiler_params=_cparams(("parallel", "arbitrary")),
        name="mlstm",
    )(q, kt, v, og, gc, gr, gain, c0, m0)


def _outproj_body(p_ref, m_ref, x_ref, g1_ref, sc_ref, sh_ref, wo_ref, wrt_ref, br_ref,
                  x1_ref, hp_ref, idx_ref, gate_ref, h_scr, *, pw, n_exp):
    mix = _dot(p_ref[...], wo_ref[0:pw, :]) + _dot(m_ref[...], wo_ref[pw:, :])
    x = x_ref[...]
    tm, d = x.shape
    for c in range(tm // MOD_CHUNK):
        rows = slice(c * MOD_CHUNK, (c + 1) * MOD_CHUNK)
        x1_ref[rows, :] = x[rows, :] + g1_ref[c:c + 1, :] * mix[rows, :]
    _modulated_norm(x1_ref[...], sc_ref, sh_ref, h_scr)
    hb = h_scr[...]
    hi = lax.bitcast_convert_type(hb[:, 0:d // 2].astype(F32), U32)
    lo = lax.bitcast_convert_type(hb[:, d // 2:].astype(F32), U32)
    words = (hi & jnp.uint32(0xFFFF0000)) | (lo >> 16)
    n_slab = words.shape[1] // LANES
    for k in range(n_slab):
        hp_ref[pl.ds(k, tm, stride=n_slab), :] = words[:, k * LANES:(k + 1) * LANES]

    logits = _dot_nt(wrt_ref[...], hb) + br_ref[...]
    e_iota = lax.broadcasted_iota(I32, logits.shape, 0)
    vals = []
    for j in range(TOP_K):
        mx = jnp.max(logits, axis=0, keepdims=True)
        ix = jnp.min(jnp.where(logits == mx, e_iota, n_exp), axis=0, keepdims=True)
        idx_ref[j:j + 1, :] = ix
        vals.append(mx)
        logits = jnp.where(e_iota == ix, -jnp.inf, logits)
    ex = [jnp.exp(v - vals[0]) for v in vals]
    tot = ex[0]
    for e in ex[1:]:
        tot = tot + e
    inv = 1.0 / tot
    for j in range(TOP_K):
        gate_ref[j:j + 1, :] = ex[j] * inv


def _outproj_kernel(pp_ref, ps_ref, mp_ref, ms_ref, xp_ref, xs_ref, g1_ref, sc_ref, sh_ref,
                    wo_ref, wrt_ref, br_ref, x1_ref, hp_ref, idx_ref, gate_ref, h_scr, *, n_p, pw, n_exp):
    i = pl.program_id(0)
    rest = (g1_ref, sc_ref, sh_ref, wo_ref, wrt_ref, br_ref, x1_ref, hp_ref, idx_ref, gate_ref, h_scr)

    @pl.when(i < n_p)
    def _():
        _outproj_body(pp_ref, mp_ref, xp_ref, *rest, pw=pw, n_exp=n_exp)

    @pl.when(i >= n_p)
    def _():
        _outproj_body(ps_ref, ms_ref, xs_ref, *rest, pw=pw, n_exp=n_exp)


def _outproj(pool_p, pool_s, ml_p, ml_s, xp, xs, g1, sc2, sh2, w_out, w_rt, b_r):
    tp, d = xp.shape
    ts = xs.shape[0]
    t = tp + ts
    pw = pool_p.shape[1]
    mw = ml_p.shape[1]
    n_exp = w_rt.shape[0]
    n_p = tp // TOK_TILE
    n_s = ts // TOK_TILE
    cpt = TOK_TILE // MOD_CHUNK
    kern = functools.partial(_outproj_kernel, n_p=n_p, pw=pw, n_exp=n_exp)
    tok = lambda i: (i, 0)
    pidx = lambda i: (jnp.minimum(i, n_p - 1), 0)
    sidx = lambda i: (jnp.maximum(i - n_p, 0), 0)
    return pl.pallas_call(
        kern,
        out_shape=(jax.ShapeDtypeStruct((t, d), F32),
                   jax.ShapeDtypeStruct((t * (d // 2 // LANES), LANES), U32),
                   jax.ShapeDtypeStruct((TOP_K, t), I32),
                   jax.ShapeDtypeStruct((TOP_K, t), F32)),
        grid=(n_p + n_s,),
        in_specs=[pl.BlockSpec((TOK_TILE, pw), pidx), pl.BlockSpec((TOK_TILE, pw), sidx),
                  pl.BlockSpec((TOK_TILE, mw), pidx), pl.BlockSpec((TOK_TILE, mw), sidx),
                  pl.BlockSpec((TOK_TILE, d), pidx), pl.BlockSpec((TOK_TILE, d), sidx),
                  pl.BlockSpec((cpt, d), tok), pl.BlockSpec((cpt, d), tok), pl.BlockSpec((cpt, d), tok),
                  _resident(w_out.shape, lambda i: (0, 0)),
                  _resident(w_rt.shape, lambda i: (0, 0)),
                  _resident(b_r.shape, lambda i: (0, 0))],
        out_specs=(pl.BlockSpec((TOK_TILE, d), tok),
                   pl.BlockSpec((TOK_TILE * (d // 2 // LANES), LANES), tok),
                   pl.BlockSpec((TOP_K, TOK_TILE), lambda i: (0, i)),
                   pl.BlockSpec((TOP_K, TOK_TILE), lambda i: (0, i))),
        scratch_shapes=[pltpu.VMEM((TOK_TILE, d), BF16)],
        compiler_params=_cparams(("parallel",)),
        name="outproj",
    )(pool_p, pool_s, ml_p, ml_s, xp, xs, g1, sc2, sh2, w_out, w_rt, b_r)


def _rank_kernel(idx_ref, lpos_ref, cnt_ref, *, n_exp):
    tm = idx_ref.shape[1]
    e_iota = lax.broadcasted_iota(I32, (n_exp, tm), 0)
    hots = [e_iota == idx_ref[j:j + 1, :] for j in range(TOP_K)]
    cnt = hots[0].astype(F32)
    for hot in hots[1:]:
        cnt = cnt + hot.astype(F32)
    cnt_b = cnt.astype(BF16)
    r = lax.broadcasted_iota(I32, (tm, tm), 0)
    c = lax.broadcasted_iota(I32, (tm, tm), 1)
    earlier_tok = _dot(cnt_b, (r < c).astype(BF16))
    er = lax.broadcasted_iota(I32, (n_exp, n_exp), 0)
    ec = lax.broadcasted_iota(I32, (n_exp, n_exp), 1)
    lower_exp = _dot((ec < er).astype(BF16), cnt_b)
    seg_start = jnp.sum(lower_exp, axis=1, keepdims=True)
    pos = earlier_tok + seg_start
    for j in range(TOP_K):
        lpos_ref[j:j + 1, :] = jnp.sum(jnp.where(hots[j], pos, 0.0), axis=0, keepdims=True).astype(I32)
    cnt_ref[...] = jnp.broadcast_to(jnp.sum(cnt, axis=1, keepdims=True), cnt_ref.shape)


def _ranks(idx_t, n_exp):
    k, t = idx_t.shape
    n_tiles = t // TOK_TILE
    return pl.pallas_call(
        functools.partial(_rank_kernel, n_exp=n_exp),
        out_shape=(jax.ShapeDtypeStruct((k, t), I32), jax.ShapeDtypeStruct((n_exp, n_tiles * LANES), F32)),
        grid=(n_tiles,),
        in_specs=[pl.BlockSpec((k, TOK_TILE), lambda i: (0, i))],
        out_specs=(pl.BlockSpec((k, TOK_TILE), lambda i: (0, i)),
                   pl.BlockSpec((n_exp, LANES), lambda i: (0, i))),
        compiler_params=_cparams(("parallel",)),
        name="ranks",
    )(idx_t)


def _tile_rows(first, n):
    start = first * SUBLANES
    if not isinstance(start, int):
        start = pl.multiple_of(start, SUBLANES)
    return pl.ds(start, n * SUBLANES)


def _slab(first, n, k):
    return pl.ds(first * SUBLANES + k, n, stride=SUBLANES)


def _segment_copies(count, make_copy, op):
    n_full = count >> SEG_SHIFT

    def full(k, carry):
        getattr(make_copy(k * SEG_CHUNK, SEG_CHUNK), op)()
        return carry

    lax.fori_loop(0, n_full, full, 0)
    rest = count - n_full * SEG_CHUNK
    base = n_full * SEG_CHUNK
    bit = SEG_CHUNK // 2
    while bit:
        off = base + (rest & -(2 * bit))

        def one(off=off, bit=bit):
            getattr(make_copy(off, bit), op)()

        pl.when((rest & bit) != 0)(one)
        bit //= 2


def _dispatch_kernel(lpos_ref, segc_ref, segl_ref, segg_ref, padrow_ref, padlen_ref, fillrow_ref, filln_ref,
                     h_ref, xs_ref, sorted_a, sorted_b, zero_scr, sem, *, n_steps, n_exp, n_regions):
    i = pl.program_id(0)
    tm = h_ref.shape[0] // SUBLANES
    base = i * tm
    slots = ((sorted_a, 0), (sorted_b, 1))

    def wait_sent(buf, which):
        pltpu.make_async_copy(buf, xs_ref.at[pl.ds(0, buf.shape[0]), :], sem.at[which]).wait()

    def run(buf, which):
        @pl.when(i >= 2)
        def _():
            wait_sent(buf, which)

        def place(t, carry):
            row = h_ref[_tile_rows(t, 1), :]
            for j in range(TOP_K):
                dst = pl.multiple_of(lpos_ref[(base + t) * TOP_K + j], SUBLANES)
                buf[pl.ds(dst, SUBLANES), :] = row
            return carry

        lax.fori_loop(0, tm, place, 0, unroll=ROW_UNROLL)

        def send(e, carry):
            s = i * n_exp + e
            src0 = segl_ref[s]
            dst0 = segg_ref[s]
            _segment_copies(segc_ref[s], lambda off, n: pltpu.make_async_copy(
                buf.at[_tile_rows(src0 + off, n), :], xs_ref.at[_tile_rows(dst0 + off, n), :], sem.at[which]), "start")
            return carry

        lax.fori_loop(0, n_exp, send, 0)

    for buf, which in slots:
        pl.when(i % 2 == which)(functools.partial(run, buf, which))

    @pl.when(i == 0)
    def _():
        zero_scr[...] = jnp.zeros_like(zero_scr)

        def fill_copy(e, c):
            return pltpu.make_async_copy(zero_scr, xs_ref.at[_tile_rows(fillrow_ref[e] + c * MOE_SUB, MOE_SUB), :],
                                         sem.at[2])

        def region(op):
            def go(e, carry):
                dst0 = padrow_ref[e]
                _segment_copies(padlen_ref[e], lambda off, n: pltpu.make_async_copy(
                    zero_scr.at[_tile_rows(0, n), :], xs_ref.at[_tile_rows(dst0 + off, n), :], sem.at[2]), op)
                lax.fori_loop(0, filln_ref[e], lambda k, c: (getattr(fill_copy(e, k), op)(), c)[1], 0)
                return carry
            return go

        lax.fori_loop(0, n_regions, region("start"), 0)
        lax.fori_loop(0, n_regions, region("wait"), 0)

    @pl.when(i == n_steps - 1)
    def _():
        for buf, which in slots[:min(n_steps, 2)]:
            wait_sent(buf, which)


def _dispatch(lpos_flat, seg_cnt, seg_l, seg_g, pad_row, pad_len, fill_row, fill_n, h_tiles, rows, n_exp):
    t = h_tiles.shape[0] // SUBLANES
    kern = functools.partial(_dispatch_kernel, n_steps=t // TOK_TILE, n_exp=n_exp, n_regions=pad_row.shape[0])
    return pl.pallas_call(
        kern,
        out_shape=jax.ShapeDtypeStruct((rows * SUBLANES, LANES), U32),
        grid_spec=pltpu.PrefetchScalarGridSpec(
            num_scalar_prefetch=8,
            grid=(t // TOK_TILE,),
            in_specs=[pl.BlockSpec((TOK_TILE * SUBLANES, LANES), lambda i, *_: (i, 0))],
            out_specs=pl.BlockSpec(memory_space=pl.ANY),
            scratch_shapes=[pltpu.VMEM((TOK_TILE * TOP_K * SUBLANES, LANES), U32),
                            pltpu.VMEM((TOK_TILE * TOP_K * SUBLANES, LANES), U32),
                            pltpu.VMEM((MOE_SUB * SUBLANES, LANES), U32),
                            pltpu.SemaphoreType.DMA((3,))]),
        compiler_params=_cparams(("arbitrary",)),
        name="dispatch",
    )(lpos_flat, seg_cnt, seg_l, seg_g, pad_row, pad_len, fill_row, fill_n, h_tiles)


def _unpack_rows(x_ref, first, n):
    his, los = [], []
    for k in range(SUBLANES):
        words = x_ref[_slab(first, n, k), :]
        his.append(lax.bitcast_convert_type(words & jnp.uint32(0xFFFF0000), F32).astype(BF16))
        los.append(lax.bitcast_convert_type(words << 16, F32).astype(BF16))
    return jnp.concatenate(his + los, axis=1)


def _moe_kernel(be_ref, bv_ref, na_ref, x_ref, wg_ref, wl_ref, bg_ref, bl_ref, wd_ref, bd_ref,
                ya_ref, yb_ref, acc_scr, sta_scr, stb_scr, wup_scr, wd_scr, sem, *, tf):
    b = pl.program_id(0)
    f = pl.program_id(1)
    n_blk = pl.num_programs(0)
    last_f = pl.num_programs(1) - 1
    rows_total = x_ref.shape[0] // SUBLANES
    n_sub_max = rows_total // MOE_SUB
    half = SUBLANES * LANES
    active = b < na_ref[0]

    def writeback(blk):
        dst = _tile_rows(blk * rows_total, rows_total)
        return (pltpu.make_async_copy(sta_scr, ya_ref.at[dst, :], sem.at[0]),
                pltpu.make_async_copy(stb_scr, yb_ref.at[dst, :], sem.at[1]))

    @pl.when(jnp.logical_and(f == last_f, b > 0))
    def _():
        for cp in writeback(b - 1):
            cp.wait()

    @pl.when(jnp.logical_and(jnp.logical_not(active), f == last_f))
    def _():
        sta_scr[...] = jnp.zeros_like(sta_scr)
        stb_scr[...] = jnp.zeros_like(stb_scr)

    @pl.when(active)
    def _():
        @pl.when(f == 0)
        def _():
            acc_scr[...] = jnp.broadcast_to(bd_ref[...], acc_scr.shape)

        n_sub = (bv_ref[b] + MOE_SUB - 1) // MOE_SUB

        def body(n_live):
            wup_scr[:, 0:tf] = wg_ref[...].astype(BF16)
            wup_scr[:, tf:] = wl_ref[...].astype(BF16)
            wd_scr[...] = wd_ref[...].astype(BF16)
            for s in range(n_live):
                rows = slice(s * MOE_SUB, (s + 1) * MOE_SUB)
                x = _unpack_rows(x_ref, s * MOE_SUB, MOE_SUB)
                up = _dot(x, wup_scr[...])
                g = jnp.minimum(up[:, 0:tf] + bg_ref[...], SWIGLU_LIMIT)
                lin = jnp.clip(up[:, tf:] + bl_ref[...], -SWIGLU_LIMIT, SWIGLU_LIMIT)
                act = g * jax.nn.sigmoid(SWIGLU_ALPHA * g) * (lin + 1.0)
                acc_scr[rows, :] = acc_scr[rows, :] + _dot(act.astype(BF16), wd_scr[...])

        for n_live in range(1, n_sub_max + 1):
            pl.when(n_sub == n_live)(functools.partial(body, n_live))

        @pl.when(f == last_f)
        def _():
            for s in range(n_sub_max):
                rows = slice(s * MOE_SUB, (s + 1) * MOE_SUB)
                for k in range(SUBLANES):
                    slab = _slab(s * MOE_SUB, MOE_SUB, k)
                    sta_scr[slab, :] = acc_scr[rows, k * LANES:(k + 1) * LANES]
                    stb_scr[slab, :] = acc_scr[rows, half + k * LANES:half + (k + 1) * LANES]

    @pl.when(f == last_f)
    def _():
        for cp in writeback(b):
            cp.start()

        @pl.when(b == n_blk - 1)
        def _():
            for cp in writeback(b):
                cp.wait()


def _moe(blk_e, blk_valid, n_act, xs, w_up, b_up, w_down, b_down):
    rows = xs.shape[0] // SUBLANES
    d = 2 * SUBLANES * LANES
    n_exp, _, two_f = w_up.shape[1:]
    assert w_up.shape[2] == d, "one (8, 128) tile of packed words per row"
    ff = two_f // 2
    tf = MOE_FF_TILE
    nf = ff // tf
    nb = rows // MOE_ROWS
    b_up3 = b_up.reshape(n_exp, 1, two_f)
    b_dn3 = b_down.reshape(n_exp, 1, d)

    def blk(b, f, be, bv, na):
        return jnp.minimum(b, na[0] - 1)

    def ftile(b, f, be, bv, na):
        return jnp.where(b < na[0], f, nf - 1)

    y_shape = jax.ShapeDtypeStruct((rows * SUBLANES, LANES), F32)
    return pl.pallas_call(
        functools.partial(_moe_kernel, tf=tf),
        out_shape=(y_shape, y_shape),
        grid_spec=pltpu.PrefetchScalarGridSpec(
            num_scalar_prefetch=3,
            grid=(nb, nf),
            in_specs=[pl.BlockSpec((MOE_ROWS * SUBLANES, LANES), lambda *a: (blk(*a), 0)),
                      pl.BlockSpec((None, None, d, tf), lambda *a: (0, a[2][a[0]], 0, ftile(*a))),
                      pl.BlockSpec((None, None, d, tf), lambda *a: (0, a[2][a[0]], 0, ftile(*a) + nf)),
                      pl.BlockSpec((None, 1, tf), lambda *a: (a[2][a[0]], 0, ftile(*a))),
                      pl.BlockSpec((None, 1, tf), lambda *a: (a[2][a[0]], 0, ftile(*a) + nf)),
                      pl.BlockSpec((None, None, tf, d), lambda *a: (0, a[2][a[0]], ftile(*a), 0)),
                      pl.BlockSpec((None, 1, d), lambda *a: (a[2][a[0]], 0, 0))],
            out_specs=(pl.BlockSpec(memory_space=pl.ANY), pl.BlockSpec(memory_space=pl.ANY)),
            scratch_shapes=[pltpu.VMEM((MOE_ROWS, d), F32),
                            pltpu.VMEM((MOE_ROWS * SUBLANES, LANES), F32),
                            pltpu.VMEM((MOE_ROWS * SUBLANES, LANES), F32),
                            pltpu.VMEM((d, 2 * tf), BF16), pltpu.VMEM((tf, d), BF16),
                            pltpu.SemaphoreType.DMA((2,))]),
        compiler_params=_cparams(("arbitrary", "arbitrary")),
        name="moe_experts",
    )(blk_e, blk_valid, n_act, xs, w_up, w_up, b_up3, b_up3, w_down, b_dn3)


def _combine_kernel(lpos_ref, segc_ref, segl_ref, segg_ref, ya_ref, yb_ref, x1_ref, gate_ref, g2_ref, fg_ref,
                    yp_ref, yo_ref, bufa, bufb, outa, outb, sem, *, n_tok, n_exp, n_p):
    i = pl.program_id(0)
    n_steps = pl.num_programs(0)
    tm = x1_ref.shape[0]
    base = i * tm
    halves = ((ya_ref, bufa, outa, 0), (yb_ref, bufb, outb, 1))

    def fetch(tile, y_ref, buf, which):
        def go(e, carry):
            s = tile * n_exp + e
            src0 = segg_ref[s]
            dst0 = segl_ref[s]
            _segment_copies(segc_ref[s], lambda off, n: pltpu.make_async_copy(
                y_ref.at[_tile_rows(src0 + off, n), :], buf.at[_tile_rows(dst0 + off, n), :], sem.at[which]), "start")
            return carry
        lax.fori_loop(0, n_exp, go, 0)

    def mix(buf, out):
        def go(t, carry):
            acc = None
            for j in range(TOP_K):
                src = pl.ds(pl.multiple_of(lpos_ref[(base + t) * TOP_K + j], SUBLANES), SUBLANES)
                term = gate_ref[j, pl.ds(t, 1), :] * buf[src, :]
                acc = term if acc is None else acc + term
            out[_tile_rows(t, 1), :] = acc
            return carry
        lax.fori_loop(0, tm, go, 0, unroll=ROW_UNROLL)

    @pl.when(i == 0)
    def _():
        for y_ref, buf, _, which in halves:
            fetch(0, y_ref, buf, which)

    for y_ref, buf, out, which in halves:
        pltpu.make_async_copy(y_ref.at[pl.ds(0, buf.shape[0]), :], buf, sem.at[which]).wait()
        mix(buf, out)

        @pl.when(i + 1 < n_steps)
        def _():
            fetch(i + 1, y_ref, buf, which)

    def finish(out_ref):
        for c in range(tm // MOD_CHUNK):
            rows = slice(c * MOD_CHUNK, (c + 1) * MOD_CHUNK)
            moe = jnp.concatenate([outa[_slab(c * MOD_CHUNK, MOD_CHUNK, k), :] for k in range(SUBLANES)] +
                                  [outb[_slab(c * MOD_CHUNK, MOD_CHUNK, k), :] for k in range(SUBLANES)], axis=1)
            xo = x1_ref[rows, :] + g2_ref[c:c + 1, :] * moe
            out_ref[rows, :] = xo * lax.rsqrt(jnp.mean(xo * xo, axis=-1, keepdims=True) + EPS) * fg_ref[...]

    @pl.when(i < n_p)
    def _():
        finish(yp_ref)

    @pl.when(i >= n_p)
    def _():
        finish(yo_ref)


def _combine(lpos_flat, seg_cnt, seg_l, seg_g, ys_a, ys_b, x1, gate_x, g2, final_gain, tp, n_exp):
    t, d = x1.shape
    lanes = LANES
    ts = t - tp
    n_p = tp // TOK_TILE
    n_s = ts // TOK_TILE
    cpt = TOK_TILE // MOD_CHUNK
    kern = functools.partial(_combine_kernel, n_tok=t, n_exp=n_exp, n_p=n_p)
    tok = lambda i, *_: (i, 0)
    return pl.pallas_call(
        kern,
        out_shape=(jax.ShapeDtypeStruct((tp, d), F32), jax.ShapeDtypeStruct((ts, d), F32)),
        grid_spec=pltpu.PrefetchScalarGridSpec(
            num_scalar_prefetch=4,
            grid=(n_p + n_s,),
            in_specs=[pl.BlockSpec(memory_space=pl.ANY),
                      pl.BlockSpec(memory_space=pl.ANY),
                      pl.BlockSpec((TOK_TILE, d), tok),
                      pl.BlockSpec((TOP_K, TOK_TILE, lanes), lambda i, *_: (0, i, 0)),
                      pl.BlockSpec((cpt, d), tok),
                      pl.BlockSpec((1, d), lambda i, *_: (0, 0))],
            out_specs=(pl.BlockSpec((TOK_TILE, d), lambda i, *_: (jnp.minimum(i, n_p - 1), 0)),
                       pl.BlockSpec((TOK_TILE, d), lambda i, *_: (jnp.maximum(i - n_p, 0), 0))),
            scratch_shapes=[pltpu.VMEM((TOK_TILE * TOP_K * SUBLANES, lanes), F32),
                            pltpu.VMEM((TOK_TILE * TOP_K * SUBLANES, lanes), F32),
                            pltpu.VMEM((TOK_TILE * SUBLANES, lanes), F32),
                            pltpu.VMEM((TOK_TILE * SUBLANES, lanes), F32),
                            pltpu.SemaphoreType.DMA((2,))]),
        compiler_params=_cparams(("arbitrary",)),
        name="combine",
    )(lpos_flat, seg_cnt, seg_l, seg_g, ys_a, ys_b, x1, gate_x, g2, final_gain)


def _state_ext(c_state, n_state):
    b, h, dh, _ = c_state.shape
    ct = jnp.swapaxes(c_state, -1, -2)
    pad = jnp.zeros((b, h, dh, LANES - 1), F32)
    return jnp.concatenate([ct, n_state[..., None], pad], axis=-1)


def _state_split(cx):
    dh = cx.shape[2]
    return jnp.swapaxes(cx[..., 0:dh], -1, -2), cx[..., dh]


def kernel(x_prompt, x_sample, state_pool, state_mlstm_C, state_mlstm_n, state_mlstm_m, c_prompt, c_sample,
           w_ada, b_ada, w_in, b_gate, pool_w, pool_scale, mlstm_gain, w_out, w_router, b_router,
           w_up, b_up, w_down, b_down, final_gain):
    depth = w_ada.shape[0]
    assert depth == 1, "single-layer trunk"
    bp, sp, d = x_prompt.shape
    bs, ss, _ = x_sample.shape
    tp, ts = bp * sp, bs * ss
    t = tp + ts
    pw = state_pool.shape[-1]
    heads, dh = state_mlstm_C.shape[2], state_mlstm_C.shape[3]
    mw = heads * dh
    n_exp = w_router.shape[-1]
    pad_rows = state_pool.shape[2]
    assert tp % TOK_TILE == 0 and ts % TOK_TILE == 0 and sp % TOK_TILE == 0
    assert ss % MOD_CHUNK == 0 and 2 * heads <= SUBLANES and pad_rows < POOL_HALO

    xp = x_prompt.reshape(tp, d)
    xs = x_sample.reshape(ts, d)

    n_c = bp + bs
    c_rows = -(-n_c // SUBLANES) * SUBLANES
    c_all = jnp.concatenate([c_prompt, c_sample, jnp.zeros((c_rows - n_c, d), F32)], axis=0)
    mod = _adaln(c_all, w_ada[0], b_ada)
    mod_c = jnp.concatenate([
        jnp.broadcast_to(mod[0:bp, None, :], (bp, sp // MOD_CHUNK, mod.shape[1])).reshape(tp // MOD_CHUNK, -1),
        jnp.broadcast_to(mod[bp:n_c, None, :], (bs, ss // MOD_CHUNK, mod.shape[1])).reshape(ts // MOD_CHUNK, -1)])
    sh1, sc1, g1, sh2, sc2, g2 = [mod_c[:, k * d:(k + 1) * d] for k in range(6)]

    w_in0 = w_in[0]
    o0 = pw
    w_main = jnp.concatenate([w_in0[:, 0:pw], w_in0[:, o0:o0 + mw], w_in0[:, o0 + 2 * mw:o0 + 4 * mw]],
                             axis=1).astype(BF16)
    w_kt = (w_in0[:, o0 + mw:o0 + 2 * mw] * (dh ** -0.5)).T.astype(BF16)
    w_g = w_in0[:, o0 + 4 * mw:]
    ng = w_g.shape[1]
    w_gc = jnp.pad(w_g, ((0, 0), (0, LANES - ng))).astype(BF16)
    w_gt = jnp.pad(w_g.T, ((0, SUBLANES - ng), (0, 0))).astype(BF16)
    b_gc = jnp.pad(b_gate[0], (0, LANES - ng)).reshape(1, LANES)
    b_gr = jnp.pad(b_gate[0], (0, SUBLANES - ng)).reshape(SUBLANES, 1)
    u, q, kt, v, og, gc, gr = _inproj(xp, xs, sc1, sh1, w_main, w_kt, w_gc, w_gt, b_gc, b_gr, pw, mw)

    pool_wb = pool_w[0].astype(BF16)
    zeros_p = jnp.zeros((bp, POOL_HALO, pw), F32)
    st_s = jnp.concatenate([jnp.zeros((bs, POOL_HALO - pad_rows, pw), F32), state_pool[0]], axis=1)
    pool_p = _pool(u, zeros_p, pool_wb, pool_scale, batch=bp, seq=sp, row0=0, tm=TOK_TILE, pos0=0)
    pool_s = _pool(u, st_s, pool_wb, pool_scale, batch=bs, seq=ss, row0=tp, tm=ss, pos0=PAST_LEN)
    new_pool_p = u[0:tp].reshape(bp, sp, pw)[:, sp - pad_rows:]
    new_pool_s = u[tp:].reshape(bs, ss, pw)[:, ss - pad_rows:]

    gain = mlstm_gain[0].reshape(1, mw)
    c0_p = jnp.zeros((bp, heads, dh, dh + LANES), F32)
    m0_p = jnp.zeros((bp, heads, 1, 1), F32)
    chunk_p = MLSTM_CHUNK if sp % MLSTM_CHUNK == 0 else MOD_CHUNK
    ml_p, cx_p, m_p = _mlstm(q, kt, v, og, gc, gr, gain, c0_p, m0_p, batch=bp, seq=sp, chunk=chunk_p,
                             row0=0, kt_per_seq=False)
    c0_s = _state_ext(state_mlstm_C[0], state_mlstm_n[0])
    m0_s = state_mlstm_m[0].reshape(bs, heads, 1, 1)
    kt_s = kt[:, tp:].reshape(mw, bs, ss).transpose(1, 0, 2)
    gr_s = gr[:, tp:].reshape(gr.shape[0], bs, ss).transpose(1, 0, 2)
    ml_s, cx_s, m_s = _mlstm(q, kt_s, v, og, gc, gr_s, gain, c0_s, m0_s, batch=bs, seq=ss, chunk=ss,
                             row0=tp, kt_per_seq=True)
    new_c_p, new_n_p = _state_split(cx_p)
    new_c_s, new_n_s = _state_split(cx_s)

    w_rt = w_router[0].T.astype(BF16)
    b_r = b_router[0].reshape(n_exp, 1)
    x1, h_packed, idx_t, gate_t = _outproj(pool_p, pool_s, ml_p, ml_s, xp, xs, g1, sc2, sh2,
                                           w_out[0].astype(BF16), w_rt, b_r)

    lpos_t, cnt = _ranks(idx_t, n_exp)
    lpos_flat = (lpos_t.T * SUBLANES).reshape(-1)
    c_te = cnt[:, ::LANES].T.astype(I32)
    counts = jnp.sum(c_te, axis=0)
    padded = (counts + MOE_ROWS - 1) // MOE_ROWS * MOE_ROWS
    pend = jnp.cumsum(padded)
    pstart = pend - padded
    seg_cnt = c_te.reshape(-1)
    seg_l = (jnp.cumsum(c_te, axis=1) - c_te).reshape(-1).astype(I32)
    seg_g = (pstart[None, :] + jnp.cumsum(c_te, axis=0) - c_te).reshape(-1).astype(I32)
    n_assign = t * TOP_K
    nb = -(-(n_assign + n_exp * (MOE_ROWS - 1)) // MOE_ROWS)
    n_act = pend[-1] // MOE_ROWS
    blk_ids = jnp.arange(nb, dtype=I32)
    last = jnp.minimum(blk_ids, n_act - 1)
    blk_e = jnp.minimum(jnp.searchsorted(pend, last * MOE_ROWS, side='right'), n_exp - 1).astype(I32)
    blk_valid = jnp.where(blk_ids < n_act,
                          jnp.clip(counts[blk_e] - (last * MOE_ROWS - pstart[blk_e]), 0, MOE_ROWS), 0).astype(I32)
    sub_end = (counts + MOE_SUB - 1) // MOE_SUB * MOE_SUB
    tail = pend[-1:]
    pad_row = jnp.concatenate([pstart + counts, tail]).astype(I32)
    pad_len = jnp.concatenate([sub_end - counts, jnp.zeros((1,), I32)]).astype(I32)
    fill_row = jnp.concatenate([pstart + sub_end, tail]).astype(I32)
    fill_n = jnp.concatenate([(padded - sub_end) // MOE_SUB, (nb * MOE_ROWS - tail) // MOE_SUB]).astype(I32)
    xs_grouped = _dispatch(lpos_flat, seg_cnt, seg_l, seg_g, pad_row, pad_len, fill_row, fill_n, h_packed,
                           nb * MOE_ROWS, n_exp)
    ys_a, ys_b = _moe(blk_e, blk_valid, n_act.reshape(1).astype(I32), xs_grouped, w_up, b_up[0], w_down, b_down[0])

    gate_x = jnp.broadcast_to(gate_t[:, :, None], (TOP_K, t, LANES))
    y_p, y_s = _combine(lpos_flat, seg_cnt, seg_l, seg_g, ys_a, ys_b, x1, gate_x, g2, final_gain.reshape(1, d),
                        tp, n_exp)

    return (y_p.reshape(bp, sp, d), y_s.reshape(bs, ss, d),
            new_pool_p[None], new_c_p[None], new_n_p[None], m_p.reshape(1, bp, heads),
            new_pool_s[None], new_c_s[None], new_n_s[None], m_s.reshape(1, bs, heads))
```

```python
import functools

import jax
import jax.numpy as jnp
from jax import lax
from jax.experimental import pallas as pl
from jax.experimental.pallas import tpu as pltpu

F32 = jnp.float32
BF16 = jnp.bfloat16
I32 = jnp.int32
U32 = jnp.uint32

EPS = 1e-6
POOL_WINDOWS = (2, 4, 8, 16)
TOP_K = 4
SWIGLU_LIMIT = 7.0
SWIGLU_ALPHA = 1.702
PAST_LEN = 2048

LANES = 128
SUBLANES = 8
MOD_CHUNK = 64
TOK_TILE = 512
POOL_HALO = 16
MLSTM_CHUNK = 256
ADA_TILE = 1024
MOE_ROWS = 768
MOE_SUB = 256
MOE_FF_TILE = 512
SEG_SHIFT = 4
SEG_CHUNK = 1 << SEG_SHIFT
ROW_UNROLL = 8
VMEM_LIMIT = 56 * 1024 * 1024


def _cparams(sem, vmem=VMEM_LIMIT):
    return pltpu.CompilerParams(dimension_semantics=sem, vmem_limit_bytes=vmem)


MOD_SHIFT1, MOD_SCALE1, MOD_GATE1, MOD_SHIFT2, MOD_SCALE2, MOD_GATE2 = range(6)


def _mod_spec(rows, d, which):
    return pl.BlockSpec((rows, d), lambda i, *_: (i, which))


def _resident(shape, index_map):
    return pl.BlockSpec(shape, index_map, pipeline_mode=pl.Buffered(1))


def _dot(a, b):
    return jnp.dot(a, b, preferred_element_type=F32)


def _dot_nt(a, b):
    return lax.dot_general(a, b, (((1,), (1,)), ((), ())), preferred_element_type=F32)


def _split3(x):
    hi = x.astype(BF16)
    r1 = x - hi.astype(F32)
    mid = r1.astype(BF16)
    lo = (r1 - mid.astype(F32)).astype(BF16)
    return hi, mid, lo


def _log_sigmoid(x):
    return jnp.minimum(x, 0.0) - jnp.log1p(jnp.exp(-jnp.abs(x)))


def _adaln_kernel(c_ref, w_ref, b_ref, o_ref):
    c = c_ref[...]
    s = (c * jax.nn.sigmoid(c)).astype(BF16)
    o_ref[...] = _dot(s, w_ref[...].astype(BF16)) + b_ref[...]


def _adaln(c_all, w_ada, b_ada):
    rows, d = c_all.shape
    n = w_ada.shape[1]
    return pl.pallas_call(
        _adaln_kernel,
        out_shape=jax.ShapeDtypeStruct((rows, n), F32),
        grid=(n // ADA_TILE,),
        in_specs=[pl.BlockSpec((rows, d), lambda j: (0, 0)),
                  pl.BlockSpec((d, ADA_TILE), lambda j: (0, j)),
                  pl.BlockSpec((1, ADA_TILE), lambda j: (0, j))],
        out_specs=pl.BlockSpec((rows, ADA_TILE), lambda j: (0, j)),
        compiler_params=_cparams(("parallel",)),
        name="adaln",
    )(c_all, w_ada, b_ada)


def _modulated_norm(x, sc_ref, sh_ref, h_scr):
    xn = x * lax.rsqrt(jnp.mean(x * x, axis=-1, keepdims=True) + EPS)
    for c in range(x.shape[0] // MOD_CHUNK):
        rows = slice(c * MOD_CHUNK, (c + 1) * MOD_CHUNK)
        h_scr[rows, :] = (xn[rows, :] * (1.0 + sc_ref[c:c + 1, :]) + sh_ref[c:c + 1, :]).astype(h_scr.dtype)


def _inproj_kernel(xp_ref, xs_ref, sc_ref, sh_ref, w_ref, wkt_ref, wg_ref, wgt_ref, bgc_ref, bgr_ref,
                   u_ref, q_ref, kt_ref, v_ref, og_ref, gc_ref, gr_ref, h_scr, *, n_p, pw, mw):
    i = pl.program_id(0)

    def body(x_ref):
        _modulated_norm(x_ref[...], sc_ref, sh_ref, h_scr)
        h = h_scr[...]
        u_ref[...] = _dot(h, w_ref[:, 0:pw])
        q_ref[...] = _dot(h, w_ref[:, pw:pw + mw]).astype(BF16)
        v_ref[...] = _dot(h, w_ref[:, pw + mw:pw + 2 * mw]).astype(BF16)
        og_ref[...] = _dot(h, w_ref[:, pw + 2 * mw:pw + 3 * mw])
        kt_ref[...] = _dot_nt(wkt_ref[...], h).astype(BF16)
        gc_ref[...] = _dot(h, wg_ref[...]) + bgc_ref[...]
        gr_ref[...] = _dot_nt(wgt_ref[...], h) + bgr_ref[...]

    pl.when(i < n_p)(functools.partial(body, xp_ref))
    pl.when(i >= n_p)(functools.partial(body, xs_ref))


def _inproj(xp, xs, sc1, sh1, w_main, w_kt, w_gc, w_gt, b_gc, b_gr, pw, mw):
    tp, d = xp.shape
    ts = xs.shape[0]
    t = tp + ts
    n_p = tp // TOK_TILE
    n_s = ts // TOK_TILE
    cpt = TOK_TILE // MOD_CHUNK
    ng = w_gt.shape[0]
    kern = functools.partial(_inproj_kernel, n_p=n_p, pw=pw, mw=mw)
    tok = lambda i: (i, 0)
    return pl.pallas_call(
        kern,
        out_shape=(jax.ShapeDtypeStruct((t, pw), F32),
                   jax.ShapeDtypeStruct((t, mw), BF16),
                   jax.ShapeDtypeStruct((mw, t), BF16),
                   jax.ShapeDtypeStruct((t, mw), BF16),
                   jax.ShapeDtypeStruct((t, mw), F32),
                   jax.ShapeDtypeStruct((t, LANES), F32),
                   jax.ShapeDtypeStruct((ng, t), F32)),
        grid=(n_p + n_s,),
        in_specs=[pl.BlockSpec((TOK_TILE, d), lambda i: (jnp.minimum(i, n_p - 1), 0)),
                  pl.BlockSpec((TOK_TILE, d), lambda i: (jnp.maximum(i - n_p, 0), 0)),
                  _mod_spec(cpt, d, MOD_SCALE1),
                  _mod_spec(cpt, d, MOD_SHIFT1),
                  _resident(w_main.shape, lambda i: (0, 0)),
                  _resident(w_kt.shape, lambda i: (0, 0)),
                  _resident(w_gc.shape, lambda i: (0, 0)),
                  _resident(w_gt.shape, lambda i: (0, 0)),
                  _resident(b_gc.shape, lambda i: (0, 0)),
                  _resident(b_gr.shape, lambda i: (0, 0))],
        out_specs=(pl.BlockSpec((TOK_TILE, pw), tok),
                   pl.BlockSpec((TOK_TILE, mw), tok),
                   pl.BlockSpec((mw, TOK_TILE), lambda i: (0, i)),
                   pl.BlockSpec((TOK_TILE, mw), tok),
                   pl.BlockSpec((TOK_TILE, mw), tok),
                   pl.BlockSpec((TOK_TILE, LANES), tok),
                   pl.BlockSpec((ng, TOK_TILE), lambda i: (0, i))),
        scratch_shapes=[pltpu.VMEM((TOK_TILE, d), BF16)],
        compiler_params=_cparams(("parallel",)),
        name="inproj",
    )(xp, xs, sc1, sh1, w_main, w_kt, w_gc, w_gt, b_gc, b_gr)


def _pool_kernel(u_ref, st_ref, pw_ref, ps_ref, o_ref, xp_scr, *, tm, pos0, group):
    j = pl.program_id(1)

    @pl.when(j == 0)
    def _():
        xp_scr[0:POOL_HALO, :] = st_ref[...]

    @pl.when(j > 0)
    def _():
        xp_scr[0:POOL_HALO, :] = xp_scr[tm:tm + POOL_HALO, :]

    xp_scr[POOL_HALO:POOL_HALO + tm, :] = u_ref[...]
    pos = pos0 + j * tm + lax.broadcasted_iota(I32, (tm, group), 0)
    for g, w in enumerate(POOL_WINDOWS):
        cs = slice(g * group, (g + 1) * group)
        x = xp_scr[POOL_HALO:POOL_HALO + tm, cs]
        acc = x
        for s in range(1, w):
            acc = acc + xp_scr[POOL_HALO - s:POOL_HALO - s + tm, cs]
        cnt = jnp.minimum(pos + 1, w).astype(F32)
        pooled = acc / cnt - x
        mixed = _dot(pooled.astype(BF16), pw_ref[g]) * ps_ref[:, cs]
        o_ref[:, cs] = mixed.astype(o_ref.dtype)


def _pool(u, state16, pool_w, pool_scale, *, batch, seq, row0, tm, pos0):
    c = u.shape[1]
    group = c // len(POOL_WINDOWS)
    nt = seq // tm
    blk0 = row0 // tm
    kern = functools.partial(_pool_kernel, tm=tm, pos0=pos0, group=group)
    return pl.pallas_call(
        kern,
        out_shape=jax.ShapeDtypeStruct((batch * seq, c), BF16),
        grid=(batch, nt),
        in_specs=[pl.BlockSpec((tm, c), lambda b, j: (blk0 + b * nt + j, 0)),
                  pl.BlockSpec((None, POOL_HALO, c), lambda b, j: (b, 0, 0)),
                  _resident(pool_w.shape, lambda b, j: (0, 0, 0)),
                  _resident(pool_scale.shape, lambda b, j: (0, 0))],
        out_specs=pl.BlockSpec((tm, c), lambda b, j: (b * nt + j, 0)),
        scratch_shapes=[pltpu.VMEM((POOL_HALO + tm, c), F32)],
        compiler_params=_cparams(("parallel", "arbitrary")),
        name="pool",
    )(u, state16, pool_w, pool_scale)


def _mlstm_kernel(q_ref, kt_ref, v_ref, og_ref, gc_ref, gr_ref, gain_ref, c0_ref, m0_ref,
                  o_ref, c_ref, m_ref, *, chunk, heads, dh):
    ci = pl.program_id(1)
    L = chunk

    @pl.when(ci == 0)
    def _():
        c_ref[...] = c0_ref[...]
        m_ref[...] = m0_ref[...]

    gc = gc_ref[...]
    gr = gr_ref[...]
    lf_c = _log_sigmoid(gc)
    lf_r = _log_sigmoid(gr)
    row_i = lax.broadcasted_iota(I32, (L, L), 0)
    col_i = lax.broadcasted_iota(I32, (L, L), 1)
    causal = col_i <= row_i
    tri = causal.astype(BF16)
    tri_t = (row_i <= col_i).astype(BF16)
    f_c = sum(_dot(tri, p) for p in _split3(lf_c))
    f_r = sum(_dot(p, tri_t) for p in _split3(lf_r))
    one_col = (lax.broadcasted_iota(I32, (L, LANES), 1) == 0).astype(BF16)
    neg_inf = jnp.float32(-jnp.inf)

    for h in range(heads):
        hs = slice(h * dh, (h + 1) * dh)
        fc = f_c[:, heads + h:heads + h + 1]
        a_r = gr[h:h + 1, :] - f_r[heads + h:heads + h + 1, :]
        m_prev = m_ref[h]
        cm = jnp.max(jnp.where(causal, a_r, neg_inf), axis=1, keepdims=True)
        m_t = fc + jnp.maximum(m_prev, cm)
        dm = jnp.exp(jnp.where(causal, (fc - m_t) + a_r, neg_inf))
        qh = q_ref[:, hs]
        kth = kt_ref[hs, :]
        vh = v_ref[:, hs]
        sc = _dot(qh, kth) * dm
        inter = jnp.exp(fc + m_prev - m_t)
        cx = c_ref[h]
        g = _dot(qh, cx.astype(BF16))
        num = _dot(sc.astype(BF16), vh) + inter * g[:, 0:dh]
        den = jnp.sum(sc, axis=1, keepdims=True) + inter * g[:, dh:dh + 1]
        hh = num * (1.0 / jnp.maximum(jnp.abs(den), jnp.exp(-m_t)))
        hn = hh * lax.rsqrt(jnp.mean(hh * hh, axis=1, keepdims=True) + EPS) * gain_ref[:, hs]
        o_ref[:, hs] = (jax.nn.sigmoid(og_ref[:, hs]) * hn).astype(o_ref.dtype)

        m_last = m_t[L - 1:L, :]
        f_last = fc[L - 1:L, :]
        w_r = jnp.exp(f_last + a_r - m_last)
        decay = jnp.exp(f_last + m_prev - m_last)
        kw = (kth.astype(F32) * w_r).astype(BF16)
        v_ext = jnp.concatenate([vh, one_col], axis=1)
        c_ref[h] = decay * cx + _dot(kw, v_ext)
        m_ref[h] = m_last


def _mlstm(q, kt, v, og, gc, gr, gain, c0, m0, *, batch, seq, chunk, row0, kt_per_seq):
    mw = q.shape[1]
    heads, dh = c0.shape[1], c0.shape[2]
    nc = seq // chunk
    blk0 = row0 // chunk
    ng = gr.shape[-2]
    tok = lambda b, c: (blk0 + b * nc + c, 0)
    if kt_per_seq:
        kt_spec = pl.BlockSpec((None, mw, chunk), lambda b, c: (b, 0, c))
        gr_spec = pl.BlockSpec((None, ng, chunk), lambda b, c: (b, 0, c))
    else:
        kt_spec = pl.BlockSpec((mw, chunk), lambda b, c: (0, blk0 + b * nc + c))
        gr_spec = pl.BlockSpec((ng, chunk), lambda b, c: (0, blk0 + b * nc + c))
    kern = functools.partial(_mlstm_kernel, chunk=chunk, heads=heads, dh=dh)
    st_spec = pl.BlockSpec((None, heads, dh, dh + LANES), lambda b, c: (b, 0, 0, 0))
    m_spec = pl.BlockSpec((None, heads, 1, 1), lambda b, c: (b, 0, 0, 0))
    return pl.pallas_call(
        kern,
        out_shape=(jax.ShapeDtypeStruct((batch * seq, mw), BF16),
                   jax.ShapeDtypeStruct(c0.shape, F32),
                   jax.ShapeDtypeStruct(m0.shape, F32)),
        grid=(batch, nc),
        in_specs=[pl.BlockSpec((chunk, mw), tok),
                  kt_spec,
                  pl.BlockSpec((chunk, mw), tok),
                  pl.BlockSpec((chunk, mw), tok),
                  pl.BlockSpec((chunk, LANES), tok),
                  gr_spec,
                  _resident(gain.shape, lambda b, c: (0, 0)),
                  st_spec, m_spec],
        out_specs=(pl.BlockSpec((chunk, mw), lambda b, c: (b * nc + c, 0)), st_spec, m_spec),
        compiler_params=_cparams(("parallel", "arbitrary")),
        name="mlstm",
    )(q, kt, v, og, gc, gr, gain, c0, m0)


def _outproj_body(p_ref, m_ref, x_ref, g1_ref, sc_ref, sh_ref, wo_ref, wrt_ref, br_ref,
                  x1_ref, hp_ref, idx_ref, gate_ref, h_scr, *, pw, n_exp):
    mix = _dot(p_ref[...], wo_ref[0:pw, :]) + _dot(m_ref[...], wo_ref[pw:, :])
    x = x_ref[...]
    tm, d = x.shape
    for c in range(tm // MOD_CHUNK):
        rows = slice(c * MOD_CHUNK, (c + 1) * MOD_CHUNK)
        x1_ref[rows, :] = x[rows, :] + g1_ref[c:c + 1, :] * mix[rows, :]
    _modulated_norm(x1_ref[...], sc_ref, sh_ref, h_scr)
    hb = h_scr[...]
    hi = lax.bitcast_convert_type(hb[:, 0:d // 2].astype(F32), U32)
    lo = lax.bitcast_convert_type(hb[:, d // 2:].astype(F32), U32)
    words = (hi & jnp.uint32(0xFFFF0000)) | (lo >> 16)
    n_slab = words.shape[1] // LANES
    for k in range(n_slab):
        hp_ref[pl.ds(k, tm, stride=n_slab), :] = words[:, k * LANES:(k + 1) * LANES]

    logits = _dot_nt(wrt_ref[...], hb) + br_ref[...]
    e_iota = lax.broadcasted_iota(I32, logits.shape, 0)
    vals = []
    for j in range(TOP_K):
        mx = jnp.max(logits, axis=0, keepdims=True)
        ix = jnp.min(jnp.where(logits == mx, e_iota, n_exp), axis=0, keepdims=True)
        idx_ref[j:j + 1, :] = ix
        vals.append(mx)
        logits = jnp.where(e_iota == ix, -jnp.inf, logits)
    ex = [jnp.exp(v - vals[0]) for v in vals]
    tot = ex[0]
    for e in ex[1:]:
        tot = tot + e
    inv = 1.0 / tot
    for j in range(TOP_K):
        gate_ref[j:j + 1, :] = ex[j] * inv


def _outproj_kernel(pp_ref, ps_ref, mp_ref, ms_ref, xp_ref, xs_ref, g1_ref, sc_ref, sh_ref,
                    wo_ref, wrt_ref, br_ref, x1_ref, hp_ref, idx_ref, gate_ref, h_scr, *, n_p, pw, n_exp):
    i = pl.program_id(0)
    rest = (g1_ref, sc_ref, sh_ref, wo_ref, wrt_ref, br_ref, x1_ref, hp_ref, idx_ref, gate_ref, h_scr)

    @pl.when(i < n_p)
    def _():
        _outproj_body(pp_ref, mp_ref, xp_ref, *rest, pw=pw, n_exp=n_exp)

    @pl.when(i >= n_p)
    def _():
        _outproj_body(ps_ref, ms_ref, xs_ref, *rest, pw=pw, n_exp=n_exp)


def _outproj(pool_p, pool_s, ml_p, ml_s, xp, xs, g1, sc2, sh2, w_out, w_rt, b_r):
    tp, d = xp.shape
    ts = xs.shape[0]
    t = tp + ts
    pw = pool_p.shape[1]
    mw = ml_p.shape[1]
    n_exp = w_rt.shape[0]
    n_p = tp // TOK_TILE
    n_s = ts // TOK_TILE
    cpt = TOK_TILE // MOD_CHUNK
    kern = functools.partial(_outproj_kernel, n_p=n_p, pw=pw, n_exp=n_exp)
    tok = lambda i: (i, 0)
    pidx = lambda i: (jnp.minimum(i, n_p - 1), 0)
    sidx = lambda i: (jnp.maximum(i - n_p, 0), 0)
    return pl.pallas_call(
        kern,
        out_shape=(jax.ShapeDtypeStruct((t, d), F32),
                   jax.ShapeDtypeStruct((t * (d // 2 // LANES), LANES), U32),
                   jax.ShapeDtypeStruct((TOP_K, t), I32),
                   jax.ShapeDtypeStruct((TOP_K, t), F32)),
        grid=(n_p + n_s,),
        in_specs=[pl.BlockSpec((TOK_TILE, pw), pidx), pl.BlockSpec((TOK_TILE, pw), sidx),
                  pl.BlockSpec((TOK_TILE, mw), pidx), pl.BlockSpec((TOK_TILE, mw), sidx),
                  pl.BlockSpec((TOK_TILE, d), pidx), pl.BlockSpec((TOK_TILE, d), sidx),
                  _mod_spec(cpt, d, MOD_GATE1), _mod_spec(cpt, d, MOD_SCALE2), _mod_spec(cpt, d, MOD_SHIFT2),
                  _resident(w_out.shape, lambda i: (0, 0)),
                  _resident(w_rt.shape, lambda i: (0, 0)),
                  _resident(b_r.shape, lambda i: (0, 0))],
        out_specs=(pl.BlockSpec((TOK_TILE, d), tok),
                   pl.BlockSpec((TOK_TILE * (d // 2 // LANES), LANES), tok),
                   pl.BlockSpec((TOP_K, TOK_TILE), lambda i: (0, i)),
                   pl.BlockSpec((TOP_K, TOK_TILE), lambda i: (0, i))),
        scratch_shapes=[pltpu.VMEM((TOK_TILE, d), BF16)],
        compiler_params=_cparams(("parallel",)),
        name="outproj",
    )(pool_p, pool_s, ml_p, ml_s, xp, xs, g1, sc2, sh2, w_out, w_rt, b_r)


def _rank_kernel(idx_ref, lpos_ref, cnt_ref, *, n_exp):
    tm = idx_ref.shape[1]
    e_iota = lax.broadcasted_iota(I32, (n_exp, tm), 0)
    hots = [e_iota == idx_ref[j:j + 1, :] for j in range(TOP_K)]
    cnt = hots[0].astype(F32)
    for hot in hots[1:]:
        cnt = cnt + hot.astype(F32)
    cnt_b = cnt.astype(BF16)
    r = lax.broadcasted_iota(I32, (tm, tm), 0)
    c = lax.broadcasted_iota(I32, (tm, tm), 1)
    earlier_tok = _dot(cnt_b, (r < c).astype(BF16))
    er = lax.broadcasted_iota(I32, (n_exp, n_exp), 0)
    ec = lax.broadcasted_iota(I32, (n_exp, n_exp), 1)
    lower_exp = _dot((ec < er).astype(BF16), cnt_b)
    seg_start = jnp.sum(lower_exp, axis=1, keepdims=True)
    pos = earlier_tok + seg_start
    for j in range(TOP_K):
        lpos_ref[j:j + 1, :] = jnp.sum(jnp.where(hots[j], pos, 0.0), axis=0, keepdims=True).astype(I32)
    cnt_ref[...] = jnp.broadcast_to(jnp.sum(cnt, axis=1, keepdims=True), cnt_ref.shape)


def _ranks(idx_t, n_exp):
    k, t = idx_t.shape
    n_tiles = t // TOK_TILE
    return pl.pallas_call(
        functools.partial(_rank_kernel, n_exp=n_exp),
        out_shape=(jax.ShapeDtypeStruct((k, t), I32), jax.ShapeDtypeStruct((n_exp, n_tiles * LANES), F32)),
        grid=(n_tiles,),
        in_specs=[pl.BlockSpec((k, TOK_TILE), lambda i: (0, i))],
        out_specs=(pl.BlockSpec((k, TOK_TILE), lambda i: (0, i)),
                   pl.BlockSpec((n_exp, LANES), lambda i: (0, i))),
        compiler_params=_cparams(("parallel",)),
        name="ranks",
    )(idx_t)


def _tile_rows(first, n):
    start = first * SUBLANES
    if not isinstance(start, int):
        start = pl.multiple_of(start, SUBLANES)
    return pl.ds(start, n * SUBLANES)


def _slab(first, n, k):
    return pl.ds(first * SUBLANES + k, n, stride=SUBLANES)


def _segment_copies(count, make_copy, op):
    n_full = count >> SEG_SHIFT

    def full(k, carry):
        getattr(make_copy(k * SEG_CHUNK, SEG_CHUNK), op)()
        return carry

    lax.fori_loop(0, n_full, full, 0)
    rest = count - n_full * SEG_CHUNK
    base = n_full * SEG_CHUNK
    bit = SEG_CHUNK // 2
    while bit:
        off = base + (rest & -(2 * bit))

        def one(off=off, bit=bit):
            getattr(make_copy(off, bit), op)()

        pl.when((rest & bit) != 0)(one)
        bit //= 2


def _dispatch_kernel(lpos_ref, segc_ref, segl_ref, segg_ref, padrow_ref, padlen_ref, fillrow_ref, filln_ref,
                     h_ref, xs_ref, sorted_a, sorted_b, zero_scr, sem, *, n_steps, n_exp, n_regions):
    i = pl.program_id(0)
    tm = h_ref.shape[0] // SUBLANES
    base = i * tm
    slots = ((sorted_a, 0), (sorted_b, 1))

    def wait_sent(buf, which):
        pltpu.make_async_copy(buf, xs_ref.at[pl.ds(0, buf.shape[0]), :], sem.at[which]).wait()

    def run(buf, which):
        @pl.when(i >= 2)
        def _():
            wait_sent(buf, which)

        def place(t, carry):
            row = h_ref[_tile_rows(t, 1), :]
            for j in range(TOP_K):
                dst = pl.multiple_of(lpos_ref[(base + t) * TOP_K + j], SUBLANES)
                buf[pl.ds(dst, SUBLANES), :] = row
            return carry

        lax.fori_loop(0, tm, place, 0, unroll=ROW_UNROLL)

        def send(e, carry):
            s = i * n_exp + e
            src0 = segl_ref[s]
            dst0 = segg_ref[s]
            _segment_copies(segc_ref[s], lambda off, n: pltpu.make_async_copy(
                buf.at[_tile_rows(src0 + off, n), :], xs_ref.at[_tile_rows(dst0 + off, n), :], sem.at[which]), "start")
            return carry

        lax.fori_loop(0, n_exp, send, 0)

    for buf, which in slots:
        pl.when(i % 2 == which)(functools.partial(run, buf, which))

    @pl.when(i == 0)
    def _():
        zero_scr[...] = jnp.zeros_like(zero_scr)

        def fill_copy(e, c):
            return pltpu.make_async_copy(zero_scr, xs_ref.at[_tile_rows(fillrow_ref[e] + c * MOE_SUB, MOE_SUB), :],
                                         sem.at[2])

        def region(op):
            def go(e, carry):
                dst0 = padrow_ref[e]
                _segment_copies(padlen_ref[e], lambda off, n: pltpu.make_async_copy(
                    zero_scr.at[_tile_rows(0, n), :], xs_ref.at[_tile_rows(dst0 + off, n), :], sem.at[2]), op)
                lax.fori_loop(0, filln_ref[e], lambda k, c: (getattr(fill_copy(e, k), op)(), c)[1], 0)
                return carry
            return go

        lax.fori_loop(0, n_regions, region("start"), 0)
        lax.fori_loop(0, n_regions, region("wait"), 0)

    @pl.when(i == n_steps - 1)
    def _():
        for buf, which in slots[:min(n_steps, 2)]:
            wait_sent(buf, which)


def _dispatch(lpos_flat, seg_cnt, seg_l, seg_g, pad_row, pad_len, fill_row, fill_n, h_tiles, rows, n_exp):
    t = h_tiles.shape[0] // SUBLANES
    kern = functools.partial(_dispatch_kernel, n_steps=t // TOK_TILE, n_exp=n_exp, n_regions=pad_row.shape[0])
    return pl.pallas_call(
        kern,
        out_shape=jax.ShapeDtypeStruct((rows * SUBLANES, LANES), U32),
        grid_spec=pltpu.PrefetchScalarGridSpec(
            num_scalar_prefetch=8,
            grid=(t // TOK_TILE,),
            in_specs=[pl.BlockSpec((TOK_TILE * SUBLANES, LANES), lambda i, *_: (i, 0))],
            out_specs=pl.BlockSpec(memory_space=pl.ANY),
            scratch_shapes=[pltpu.VMEM((TOK_TILE * TOP_K * SUBLANES, LANES), U32),
                            pltpu.VMEM((TOK_TILE * TOP_K * SUBLANES, LANES), U32),
                            pltpu.VMEM((MOE_SUB * SUBLANES, LANES), U32),
                            pltpu.SemaphoreType.DMA((3,))]),
        compiler_params=_cparams(("arbitrary",)),
        name="dispatch",
    )(lpos_flat, seg_cnt, seg_l, seg_g, pad_row, pad_len, fill_row, fill_n, h_tiles)


def _unpack_rows(x_ref, first, n):
    his, los = [], []
    for k in range(SUBLANES):
        words = x_ref[_slab(first, n, k), :]
        his.append(lax.bitcast_convert_type(words & jnp.uint32(0xFFFF0000), F32).astype(BF16))
        los.append(lax.bitcast_convert_type(words << 16, F32).astype(BF16))
    return jnp.concatenate(his + los, axis=1)


def _moe_kernel(be_ref, bv_ref, na_ref, x_ref, wg_ref, wl_ref, bg_ref, bl_ref, wd_ref, bd_ref,
                ya_ref, yb_ref, acc_scr, sta_scr, stb_scr, wup_scr, wd_scr, sem, *, tf):
    b = pl.program_id(0)
    f = pl.program_id(1)
    n_blk = pl.num_programs(0)
    last_f = pl.num_programs(1) - 1
    rows_total = x_ref.shape[0] // SUBLANES
    n_sub_max = rows_total // MOE_SUB
    half = SUBLANES * LANES
    active = b < na_ref[0]

    def writeback(blk):
        dst = _tile_rows(blk * rows_total, rows_total)
        return (pltpu.make_async_copy(sta_scr, ya_ref.at[dst, :], sem.at[0]),
                pltpu.make_async_copy(stb_scr, yb_ref.at[dst, :], sem.at[1]))

    @pl.when(jnp.logical_and(f == last_f, b > 0))
    def _():
        for cp in writeback(b - 1):
            cp.wait()

    @pl.when(jnp.logical_and(jnp.logical_not(active), f == last_f))
    def _():
        sta_scr[...] = jnp.zeros_like(sta_scr)
        stb_scr[...] = jnp.zeros_like(stb_scr)

    @pl.when(active)
    def _():
        @pl.when(f == 0)
        def _():
            acc_scr[...] = jnp.broadcast_to(bd_ref[...], acc_scr.shape)

        n_sub = (bv_ref[b] + MOE_SUB - 1) // MOE_SUB

        def stage(s, y):
            for k in range(SUBLANES):
                slab = _slab(s * MOE_SUB, MOE_SUB, k)
                sta_scr[slab, :] = y[:, k * LANES:(k + 1) * LANES]
                stb_scr[slab, :] = y[:, half + k * LANES:half + (k + 1) * LANES]

        def body(n_live, to_staging):
            wup_scr[:, 0:tf] = wg_ref[...].astype(BF16)
            wup_scr[:, tf:] = wl_ref[...].astype(BF16)
            wd_scr[...] = wd_ref[...].astype(BF16)
            for s in range(n_live):
                rows = slice(s * MOE_SUB, (s + 1) * MOE_SUB)
                x = _unpack_rows(x_ref, s * MOE_SUB, MOE_SUB)
                up = _dot(x, wup_scr[...])
                g = jnp.minimum(up[:, 0:tf] + bg_ref[...], SWIGLU_LIMIT)
                lin = jnp.clip(up[:, tf:] + bl_ref[...], -SWIGLU_LIMIT, SWIGLU_LIMIT)
                act = g * jax.nn.sigmoid(SWIGLU_ALPHA * g) * (lin + 1.0)
                y = acc_scr[rows, :] + _dot(act.astype(BF16), wd_scr[...])
                if to_staging:
                    stage(s, y)
                else:
                    acc_scr[rows, :] = y

        full = n_sub == n_sub_max
        for n_live in range(1, n_sub_max):
            pl.when(n_sub == n_live)(functools.partial(body, n_live, False))
        pl.when(jnp.logical_and(full, f != last_f))(functools.partial(body, n_sub_max, False))
        pl.when(jnp.logical_and(full, f == last_f))(functools.partial(body, n_sub_max, True))

        @pl.when(jnp.logical_and(jnp.logical_not(full), f == last_f))
        def _():
            for s in range(n_sub_max):
                stage(s, acc_scr[s * MOE_SUB:(s + 1) * MOE_SUB, :])

    @pl.when(f == last_f)
    def _():
        for cp in writeback(b):
            cp.start()

        @pl.when(b == n_blk - 1)
        def _():
            for cp in writeback(b):
                cp.wait()


def _moe(blk_e, blk_valid, n_act, xs, w_up, b_up, w_down, b_down):
    rows = xs.shape[0] // SUBLANES
    d = 2 * SUBLANES * LANES
    n_exp, _, two_f = w_up.shape[1:]
    assert w_up.shape[2] == d, "one (8, 128) tile of packed words per row"
    ff = two_f // 2
    tf = MOE_FF_TILE
    nf = ff // tf
    nb = rows // MOE_ROWS
    b_up3 = b_up.reshape(n_exp, 1, two_f)
    b_dn3 = b_down.reshape(n_exp, 1, d)

    def blk(b, f, be, bv, na):
        return jnp.minimum(b, na[0] - 1)

    def ftile(b, f, be, bv, na):
        return jnp.where(b < na[0], f, nf - 1)

    y_shape = jax.ShapeDtypeStruct((rows * SUBLANES, LANES), F32)
    return pl.pallas_call(
        functools.partial(_moe_kernel, tf=tf),
        out_shape=(y_shape, y_shape),
        grid_spec=pltpu.PrefetchScalarGridSpec(
            num_scalar_prefetch=3,
            grid=(nb, nf),
            in_specs=[pl.BlockSpec((MOE_ROWS * SUBLANES, LANES), lambda *a: (blk(*a), 0)),
                      pl.BlockSpec((None, None, d, tf), lambda *a: (0, a[2][a[0]], 0, ftile(*a))),
                      pl.BlockSpec((None, None, d, tf), lambda *a: (0, a[2][a[0]], 0, ftile(*a) + nf)),
                      pl.BlockSpec((None, 1, tf), lambda *a: (a[2][a[0]], 0, ftile(*a))),
                      pl.BlockSpec((None, 1, tf), lambda *a: (a[2][a[0]], 0, ftile(*a) + nf)),
                      pl.BlockSpec((None, None, tf, d), lambda *a: (0, a[2][a[0]], ftile(*a), 0)),
                      pl.BlockSpec((None, 1, d), lambda *a: (a[2][a[0]], 0, 0))],
            out_specs=(pl.BlockSpec(memory_space=pl.ANY), pl.BlockSpec(memory_space=pl.ANY)),
            scratch_shapes=[pltpu.VMEM((MOE_ROWS, d), F32),
                            pltpu.VMEM((MOE_ROWS * SUBLANES, LANES), F32),
                            pltpu.VMEM((MOE_ROWS * SUBLANES, LANES), F32),
                            pltpu.VMEM((d, 2 * tf), BF16), pltpu.VMEM((tf, d), BF16),
                            pltpu.SemaphoreType.DMA((2,))]),
        compiler_params=_cparams(("arbitrary", "arbitrary")),
        name="moe_experts",
    )(blk_e, blk_valid, n_act, xs, w_up, w_up, b_up3, b_up3, w_down, b_dn3)


def _combine_kernel(lpos_ref, segc_ref, segl_ref, segg_ref, ya_ref, yb_ref, x1_ref, gate_ref, g2_ref, fg_ref,
                    yp_ref, yo_ref, bufa, bufb, outa, outb, sem, *, n_tok, n_exp, n_p):
    i = pl.program_id(0)
    n_steps = pl.num_programs(0)
    tm = x1_ref.shape[0]
    base = i * tm
    halves = ((ya_ref, bufa, outa, 0), (yb_ref, bufb, outb, 1))

    def fetch(tile, y_ref, buf, which):
        def go(e, carry):
            s = tile * n_exp + e
            src0 = segg_ref[s]
            dst0 = segl_ref[s]
            _segment_copies(segc_ref[s], lambda off, n: pltpu.make_async_copy(
                y_ref.at[_tile_rows(src0 + off, n), :], buf.at[_tile_rows(dst0 + off, n), :], sem.at[which]), "start")
            return carry
        lax.fori_loop(0, n_exp, go, 0)

    def mix(buf, out):
        def go(t, carry):
            acc = None
            for j in range(TOP_K):
                src = pl.ds(pl.multiple_of(lpos_ref[(base + t) * TOP_K + j], SUBLANES), SUBLANES)
                term = gate_ref[j, pl.ds(t, 1), :] * buf[src, :]
                acc = term if acc is None else acc + term
            out[_tile_rows(t, 1), :] = acc
            return carry
        lax.fori_loop(0, tm, go, 0, unroll=ROW_UNROLL)

    @pl.when(i == 0)
    def _():
        for y_ref, buf, _, which in halves:
            fetch(0, y_ref, buf, which)

    for y_ref, buf, out, which in halves:
        pltpu.make_async_copy(y_ref.at[pl.ds(0, buf.shape[0]), :], buf, sem.at[which]).wait()
        mix(buf, out)

        @pl.when(i + 1 < n_steps)
        def _():
            fetch(i + 1, y_ref, buf, which)

    def finish(out_ref):
        for c in range(tm // MOD_CHUNK):
            rows = slice(c * MOD_CHUNK, (c + 1) * MOD_CHUNK)
            moe = jnp.concatenate([outa[_slab(c * MOD_CHUNK, MOD_CHUNK, k), :] for k in range(SUBLANES)] +
                                  [outb[_slab(c * MOD_CHUNK, MOD_CHUNK, k), :] for k in range(SUBLANES)], axis=1)
            xo = x1_ref[rows, :] + g2_ref[c:c + 1, :] * moe
            out_ref[rows, :] = xo * lax.rsqrt(jnp.mean(xo * xo, axis=-1, keepdims=True) + EPS) * fg_ref[...]

    @pl.when(i < n_p)
    def _():
        finish(yp_ref)

    @pl.when(i >= n_p)
    def _():
        finish(yo_ref)


def _combine(lpos_flat, seg_cnt, seg_l, seg_g, ys_a, ys_b, x1, gate_x, g2, final_gain, tp, n_exp):
    t, d = x1.shape
    lanes = LANES
    ts = t - tp
    n_p = tp // TOK_TILE
    n_s = ts // TOK_TILE
    cpt = TOK_TILE // MOD_CHUNK
    kern = functools.partial(_combine_kernel, n_tok=t, n_exp=n_exp, n_p=n_p)
    tok = lambda i, *_: (i, 0)
    return pl.pallas_call(
        kern,
        out_shape=(jax.ShapeDtypeStruct((tp, d), F32), jax.ShapeDtypeStruct((ts, d), F32)),
        grid_spec=pltpu.PrefetchScalarGridSpec(
            num_scalar_prefetch=4,
            grid=(n_p + n_s,),
            in_specs=[pl.BlockSpec(memory_space=pl.ANY),
                      pl.BlockSpec(memory_space=pl.ANY),
                      pl.BlockSpec((TOK_TILE, d), tok),
                      pl.BlockSpec((TOP_K, TOK_TILE, lanes), lambda i, *_: (0, i, 0)),
                      _mod_spec(cpt, d, MOD_GATE2),
                      pl.BlockSpec((1, d), lambda i, *_: (0, 0))],
            out_specs=(pl.BlockSpec((TOK_TILE, d), lambda i, *_: (jnp.minimum(i, n_p - 1), 0)),
                       pl.BlockSpec((TOK_TILE, d), lambda i, *_: (jnp.maximum(i - n_p, 0), 0))),
            scratch_shapes=[pltpu.VMEM((TOK_TILE * TOP_K * SUBLANES, lanes), F32),
                            pltpu.VMEM((TOK_TILE * TOP_K * SUBLANES, lanes), F32),
                            pltpu.VMEM((TOK_TILE * SUBLANES, lanes), F32),
                            pltpu.VMEM((TOK_TILE * SUBLANES, lanes), F32),
                            pltpu.SemaphoreType.DMA((2,))]),
        compiler_params=_cparams(("arbitrary",)),
        name="combine",
    )(lpos_flat, seg_cnt, seg_l, seg_g, ys_a, ys_b, x1, gate_x, g2, final_gain)


def _state_ext(c_state, n_state):
    b, h, dh, _ = c_state.shape
    ct = jnp.swapaxes(c_state, -1, -2)
    pad = jnp.zeros((b, h, dh, LANES - 1), F32)
    return jnp.concatenate([ct, n_state[..., None], pad], axis=-1)


def _state_split(cx):
    dh = cx.shape[2]
    return jnp.swapaxes(cx[..., 0:dh], -1, -2), cx[..., dh]


def kernel(x_prompt, x_sample, state_pool, state_mlstm_C, state_mlstm_n, state_mlstm_m, c_prompt, c_sample,
           w_ada, b_ada, w_in, b_gate, pool_w, pool_scale, mlstm_gain, w_out, w_router, b_router,
           w_up, b_up, w_down, b_down, final_gain):
    depth = w_ada.shape[0]
    assert depth == 1, "single-layer trunk"
    bp, sp, d = x_prompt.shape
    bs, ss, _ = x_sample.shape
    tp, ts = bp * sp, bs * ss
    t = tp + ts
    pw = state_pool.shape[-1]
    heads, dh = state_mlstm_C.shape[2], state_mlstm_C.shape[3]
    mw = heads * dh
    n_exp = w_router.shape[-1]
    pad_rows = state_pool.shape[2]
    assert tp % TOK_TILE == 0 and ts % TOK_TILE == 0 and sp % TOK_TILE == 0
    assert ss % MOD_CHUNK == 0 and 2 * heads <= SUBLANES and pad_rows < POOL_HALO

    xp = x_prompt.reshape(tp, d)
    xs = x_sample.reshape(ts, d)

    n_c = bp + bs
    c_rows = -(-n_c // SUBLANES) * SUBLANES
    c_all = jnp.concatenate([c_prompt, c_sample, jnp.zeros((c_rows - n_c, d), F32)], axis=0)
    mod = _adaln(c_all, w_ada[0], b_ada)
    mod_c = jnp.concatenate([
        jnp.broadcast_to(mod[0:bp, None, :], (bp, sp // MOD_CHUNK, mod.shape[1])).reshape(tp // MOD_CHUNK, -1),
        jnp.broadcast_to(mod[bp:n_c, None, :], (bs, ss // MOD_CHUNK, mod.shape[1])).reshape(ts // MOD_CHUNK, -1)])
    sh1 = sc1 = g1 = sh2 = sc2 = g2 = mod_c

    w_in0 = w_in[0]
    o0 = pw
    w_main = jnp.concatenate([w_in0[:, 0:pw], w_in0[:, o0:o0 + mw], w_in0[:, o0 + 2 * mw:o0 + 4 * mw]],
                             axis=1).astype(BF16)
    w_kt = (w_in0[:, o0 + mw:o0 + 2 * mw] * (dh ** -0.5)).T.astype(BF16)
    w_g = w_in0[:, o0 + 4 * mw:]
    ng = w_g.shape[1]
    w_gc = jnp.pad(w_g, ((0, 0), (0, LANES - ng))).astype(BF16)
    w_gt = jnp.pad(w_g.T, ((0, SUBLANES - ng), (0, 0))).astype(BF16)
    b_gc = jnp.pad(b_gate[0], (0, LANES - ng)).reshape(1, LANES)
    b_gr = jnp.pad(b_gate[0], (0, SUBLANES - ng)).reshape(SUBLANES, 1)
    u, q, kt, v, og, gc, gr = _inproj(xp, xs, sc1, sh1, w_main, w_kt, w_gc, w_gt, b_gc, b_gr, pw, mw)

    pool_wb = pool_w[0].astype(BF16)
    zeros_p = jnp.zeros((bp, POOL_HALO, pw), F32)
    st_s = jnp.concatenate([jnp.zeros((bs, POOL_HALO - pad_rows, pw), F32), state_pool[0]], axis=1)
    pool_p = _pool(u, zeros_p, pool_wb, pool_scale, batch=bp, seq=sp, row0=0, tm=TOK_TILE, pos0=0)
    pool_s = _pool(u, st_s, pool_wb, pool_scale, batch=bs, seq=ss, row0=tp, tm=ss, pos0=PAST_LEN)
    new_pool_p = u[0:tp].reshape(bp, sp, pw)[:, sp - pad_rows:]
    new_pool_s = u[tp:].reshape(bs, ss, pw)[:, ss - pad_rows:]

    gain = mlstm_gain[0].reshape(1, mw)
    c0_p = jnp.zeros((bp, heads, dh, dh + LANES), F32)
    m0_p = jnp.zeros((bp, heads, 1, 1), F32)
    chunk_p = MLSTM_CHUNK if sp % MLSTM_CHUNK == 0 else MOD_CHUNK
    ml_p, cx_p, m_p = _mlstm(q, kt, v, og, gc, gr, gain, c0_p, m0_p, batch=bp, seq=sp, chunk=chunk_p,
                             row0=0, kt_per_seq=False)
    c0_s = _state_ext(state_mlstm_C[0], state_mlstm_n[0])
    m0_s = state_mlstm_m[0].reshape(bs, heads, 1, 1)
    kt_s = kt[:, tp:].reshape(mw, bs, ss).transpose(1, 0, 2)
    gr_s = gr[:, tp:].reshape(gr.shape[0], bs, ss).transpose(1, 0, 2)
    ml_s, cx_s, m_s = _mlstm(q, kt_s, v, og, gc, gr_s, gain, c0_s, m0_s, batch=bs, seq=ss, chunk=ss,
                             row0=tp, kt_per_seq=True)
    new_c_p, new_n_p = _state_split(cx_p)
    new_c_s, new_n_s = _state_split(cx_s)

    w_rt = w_router[0].T.astype(BF16)
    b_r = b_router[0].reshape(n_exp, 1)
    x1, h_packed, idx_t, gate_t = _outproj(pool_p, pool_s, ml_p, ml_s, xp, xs, g1, sc2, sh2,
                                           w_out[0].astype(BF16), w_rt, b_r)

    lpos_t, cnt = _ranks(idx_t, n_exp)
    lpos_flat = (lpos_t.T * SUBLANES).reshape(-1)
    c_te = cnt[:, ::LANES].T.astype(I32)
    counts = jnp.sum(c_te, axis=0)
    padded = (counts + MOE_ROWS - 1) // MOE_ROWS * MOE_ROWS
    pend = jnp.cumsum(padded)
    pstart = pend - padded
    seg_cnt = c_te.reshape(-1)
    seg_l = (jnp.cumsum(c_te, axis=1) - c_te).reshape(-1).astype(I32)
    seg_g = (pstart[None, :] + jnp.cumsum(c_te, axis=0) - c_te).reshape(-1).astype(I32)
    n_assign = t * TOP_K
    nb = -(-(n_assign + n_exp * (MOE_ROWS - 1)) // MOE_ROWS)
    n_act = pend[-1] // MOE_ROWS
    blk_ids = jnp.arange(nb, dtype=I32)
    last = jnp.minimum(blk_ids, n_act - 1)
    blk_e = jnp.minimum(jnp.searchsorted(pend, last * MOE_ROWS, side='right'), n_exp - 1).astype(I32)
    blk_valid = jnp.where(blk_ids < n_act,
                          jnp.clip(counts[blk_e] - (last * MOE_ROWS - pstart[blk_e]), 0, MOE_ROWS), 0).astype(I32)
    sub_end = (counts + MOE_SUB - 1) // MOE_SUB * MOE_SUB
    tail = pend[-1:]
    pad_row = jnp.concatenate([pstart + counts, tail]).astype(I32)
    pad_len = jnp.concatenate([sub_end - counts, jnp.zeros((1,), I32)]).astype(I32)
    fill_row = jnp.concatenate([pstart + sub_end, tail]).astype(I32)
    fill_n = jnp.concatenate([(padded - sub_end) // MOE_SUB, (nb * MOE_ROWS - tail) // MOE_SUB]).astype(I32)
    xs_grouped = _dispatch(lpos_flat, seg_cnt, seg_l, seg_g, pad_row, pad_len, fill_row, fill_n, h_packed,
                           nb * MOE_ROWS, n_exp)
    ys_a, ys_b = _moe(blk_e, blk_valid, n_act.reshape(1).astype(I32), xs_grouped, w_up, b_up[0], w_down, b_down[0])

    gate_x = jnp.broadcast_to(gate_t[:, :, None], (TOP_K, t, LANES))
    y_p, y_s = _combine(lpos_flat, seg_cnt, seg_l, seg_g, ys_a, ys_b, x1, gate_x, g2, final_gain.reshape(1, d),
                        tp, n_exp)

    return (y_p.reshape(bp, sp, d), y_s.reshape(bs, ss, d),
            new_pool_p[None], new_c_p[None], new_n_p[None], m_p.reshape(1, bp, heads),
            new_pool_s[None], new_c_s[None], new_n_s[None], m_s.reshape(1, bs, heads))
```

```python
import functools

import jax
import jax.numpy as jnp
from jax import lax
from jax.experimental import pallas as pl
from jax.experimental.pallas import tpu as pltpu

F32 = jnp.float32
BF16 = jnp.bfloat16
I32 = jnp.int32
U32 = jnp.uint32

EPS = 1e-6
POOL_WINDOWS = (2, 4, 8, 16)
TOP_K = 4
SWIGLU_LIMIT = 7.0
SWIGLU_ALPHA = 1.702
PAST_LEN = 2048

LANES = 128
SUBLANES = 8
MOD_CHUNK = 64
TOK_TILE = 512
POOL_HALO = 16
MLSTM_CHUNK = 256
ADA_TILE = 1024
MOE_ROWS = 768
MOE_SUB = 256
MOE_FF_TILE = 512
SEG_SHIFT = 4
SEG_CHUNK = 1 << SEG_SHIFT
ROW_UNROLL = 8
VMEM_LIMIT = 56 * 1024 * 1024


def _cparams(sem, vmem=VMEM_LIMIT):
    return pltpu.CompilerParams(dimension_semantics=sem, vmem_limit_bytes=vmem)


MOD_SHIFT1, MOD_SCALE1, MOD_GATE1, MOD_SHIFT2, MOD_SCALE2, MOD_GATE2 = range(6)


def _mod_spec(rows, d, which):
    return pl.BlockSpec((rows, d), lambda i, *_: (i, which))


def _resident(shape, index_map):
    return pl.BlockSpec(shape, index_map, pipeline_mode=pl.Buffered(1))


def _dot(a, b):
    return jnp.dot(a, b, preferred_element_type=F32)


def _dot_nt(a, b):
    return lax.dot_general(a, b, (((1,), (1,)), ((), ())), preferred_element_type=F32)


def _split3(x):
    hi = x.astype(BF16)
    r1 = x - hi.astype(F32)
    mid = r1.astype(BF16)
    lo = (r1 - mid.astype(F32)).astype(BF16)
    return hi, mid, lo


def _log_sigmoid(x):
    return jnp.minimum(x, 0.0) - jnp.log1p(jnp.exp(-jnp.abs(x)))


def _adaln_kernel(c_ref, w_ref, b_ref, o_ref):
    c = c_ref[...]
    s = (c * jax.nn.sigmoid(c)).astype(BF16)
    o_ref[...] = _dot(s, w_ref[...].astype(BF16)) + b_ref[...]


def _adaln(c_all, w_ada, b_ada):
    rows, d = c_all.shape
    n = w_ada.shape[1]
    return pl.pallas_call(
        _adaln_kernel,
        out_shape=jax.ShapeDtypeStruct((rows, n), F32),
        grid=(n // ADA_TILE,),
        in_specs=[pl.BlockSpec((rows, d), lambda j: (0, 0)),
                  pl.BlockSpec((d, ADA_TILE), lambda j: (0, j)),
                  pl.BlockSpec((1, ADA_TILE), lambda j: (0, j))],
        out_specs=pl.BlockSpec((rows, ADA_TILE), lambda j: (0, j)),
        compiler_params=_cparams(("parallel",)),
        name="adaln",
    )(c_all, w_ada, b_ada)


def _modulated_norm(x, sc_ref, sh_ref, h_scr):
    xn = x * lax.rsqrt(jnp.mean(x * x, axis=-1, keepdims=True) + EPS)
    for c in range(x.shape[0] // MOD_CHUNK):
        rows = slice(c * MOD_CHUNK, (c + 1) * MOD_CHUNK)
        h_scr[rows, :] = (xn[rows, :] * (1.0 + sc_ref[c:c + 1, :]) + sh_ref[c:c + 1, :]).astype(h_scr.dtype)


def _inproj_kernel(xp_ref, xs_ref, sc_ref, sh_ref, w_ref, wkt_ref, wg_ref, wgt_ref, bgc_ref, bgr_ref,
                   u_ref, q_ref, kt_ref, v_ref, og_ref, gc_ref, gr_ref, h_scr, *, n_p, pw, mw):
    i = pl.program_id(0)

    def body(x_ref):
        _modulated_norm(x_ref[...], sc_ref, sh_ref, h_scr)
        h = h_scr[...]
        u_ref[...] = _dot(h, w_ref[:, 0:pw])
        q_ref[...] = _dot(h, w_ref[:, pw:pw + mw]).astype(BF16)
        v_ref[...] = _dot(h, w_ref[:, pw + mw:pw + 2 * mw]).astype(BF16)
        og_ref[...] = _dot(h, w_ref[:, pw + 2 * mw:pw + 3 * mw])
        kt_ref[...] = _dot_nt(wkt_ref[...], h).astype(BF16)
        gc_ref[...] = _dot(h, wg_ref[...]) + bgc_ref[...]
        gr_ref[...] = _dot_nt(wgt_ref[...], h) + bgr_ref[...]

    pl.when(i < n_p)(functools.partial(body, xp_ref))
    pl.when(i >= n_p)(functools.partial(body, xs_ref))


def _inproj(xp, xs, sc1, sh1, w_main, w_kt, w_gc, w_gt, b_gc, b_gr, pw, mw):
    tp, d = xp.shape
    ts = xs.shape[0]
    t = tp + ts
    n_p = tp // TOK_TILE
    n_s = ts // TOK_TILE
    cpt = TOK_TILE // MOD_CHUNK
    ng = w_gt.shape[0]
    kern = functools.partial(_inproj_kernel, n_p=n_p, pw=pw, mw=mw)
    tok = lambda i: (i, 0)
    return pl.pallas_call(
        kern,
        out_shape=(jax.ShapeDtypeStruct((t, pw), F32),
                   jax.ShapeDtypeStruct((t, mw), BF16),
                   jax.ShapeDtypeStruct((mw, t), BF16),
                   jax.ShapeDtypeStruct((t, mw), BF16),
                   jax.ShapeDtypeStruct((t, mw), F32),
                   jax.ShapeDtypeStruct((t, LANES), F32),
                   jax.ShapeDtypeStruct((ng, t), F32)),
        grid=(n_p + n_s,),
        in_specs=[pl.BlockSpec((TOK_TILE, d), lambda i: (jnp.minimum(i, n_p - 1), 0)),
                  pl.BlockSpec((TOK_TILE, d), lambda i: (jnp.maximum(i - n_p, 0), 0)),
                  _mod_spec(cpt, d, MOD_SCALE1),
                  _mod_spec(cpt, d, MOD_SHIFT1),
                  _resident(w_main.shape, lambda i: (0, 0)),
                  _resident(w_kt.shape, lambda i: (0, 0)),
                  _resident(w_gc.shape, lambda i: (0, 0)),
                  _resident(w_gt.shape, lambda i: (0, 0)),
                  _resident(b_gc.shape, lambda i: (0, 0)),
                  _resident(b_gr.shape, lambda i: (0, 0))],
        out_specs=(pl.BlockSpec((TOK_TILE, pw), tok),
                   pl.BlockSpec((TOK_TILE, mw), tok),
                   pl.BlockSpec((mw, TOK_TILE), lambda i: (0, i)),
                   pl.BlockSpec((TOK_TILE, mw), tok),
                   pl.BlockSpec((TOK_TILE, mw), tok),
                   pl.BlockSpec((TOK_TILE, LANES), tok),
                   pl.BlockSpec((ng, TOK_TILE), lambda i: (0, i))),
        scratch_shapes=[pltpu.VMEM((TOK_TILE, d), BF16)],
        compiler_params=_cparams(("parallel",)),
        name="inproj",
    )(xp, xs, sc1, sh1, w_main, w_kt, w_gc, w_gt, b_gc, b_gr)


def _pool_kernel(u_ref, st_ref, pw_ref, ps_ref, o_ref, xp_scr, *, tm, pos0, group):
    j = pl.program_id(1)

    @pl.when(j == 0)
    def _():
        xp_scr[0:POOL_HALO, :] = st_ref[...]

    @pl.when(j > 0)
    def _():
        xp_scr[0:POOL_HALO, :] = xp_scr[tm:tm + POOL_HALO, :]

    xp_scr[POOL_HALO:POOL_HALO + tm, :] = u_ref[...]
    pos = pos0 + j * tm + lax.broadcasted_iota(I32, (tm, group), 0)
    for g, w in enumerate(POOL_WINDOWS):
        cs = slice(g * group, (g + 1) * group)
        x = xp_scr[POOL_HALO:POOL_HALO + tm, cs]
        acc = x
        for s in range(1, w):
            acc = acc + xp_scr[POOL_HALO - s:POOL_HALO - s + tm, cs]
        cnt = jnp.minimum(pos + 1, w).astype(F32)
        pooled = acc / cnt - x
        mixed = _dot(pooled.astype(BF16), pw_ref[g]) * ps_ref[:, cs]
        o_ref[:, cs] = mixed.astype(o_ref.dtype)


def _pool(u, state16, pool_w, pool_scale, *, batch, seq, row0, tm, pos0):
    c = u.shape[1]
    group = c // len(POOL_WINDOWS)
    nt = seq // tm
    blk0 = row0 // tm
    kern = functools.partial(_pool_kernel, tm=tm, pos0=pos0, group=group)
    return pl.pallas_call(
        kern,
        out_shape=jax.ShapeDtypeStruct((batch * seq, c), BF16),
        grid=(batch, nt),
        in_specs=[pl.BlockSpec((tm, c), lambda b, j: (blk0 + b * nt + j, 0)),
                  pl.BlockSpec((None, POOL_HALO, c), lambda b, j: (b, 0, 0)),
                  _resident(pool_w.shape, lambda b, j: (0, 0, 0)),
                  _resident(pool_scale.shape, lambda b, j: (0, 0))],
        out_specs=pl.BlockSpec((tm, c), lambda b, j: (b * nt + j, 0)),
        scratch_shapes=[pltpu.VMEM((POOL_HALO + tm, c), F32)],
        compiler_params=_cparams(("parallel", "arbitrary")),
        name="pool",
    )(u, state16, pool_w, pool_scale)


def _mlstm_kernel(q_ref, kt_ref, v_ref, og_ref, gc_ref, gr_ref, gain_ref, c0_ref, m0_ref,
                  o_ref, c_ref, m_ref, *, chunk, heads, dh):
    ci = pl.program_id(1)
    L = chunk

    @pl.when(ci == 0)
    def _():
        c_ref[...] = c0_ref[...]
        m_ref[...] = m0_ref[...]

    gc = gc_ref[...]
    gr = gr_ref[...]
    lf_c = _log_sigmoid(gc)
    lf_r = _log_sigmoid(gr)
    row_i = lax.broadcasted_iota(I32, (L, L), 0)
    col_i = lax.broadcasted_iota(I32, (L, L), 1)
    causal = col_i <= row_i
    tri = causal.astype(BF16)
    tri_t = (row_i <= col_i).astype(BF16)
    f_c = sum(_dot(tri, p) for p in _split3(lf_c))
    f_r = sum(_dot(p, tri_t) for p in _split3(lf_r))
    one_col = (lax.broadcasted_iota(I32, (L, LANES), 1) == 0).astype(BF16)
    neg_inf = jnp.float32(-jnp.inf)

    for h in range(heads):
        hs = slice(h * dh, (h + 1) * dh)
        fc = f_c[:, heads + h:heads + h + 1]
        a_r = gr[h:h + 1, :] - f_r[heads + h:heads + h + 1, :]
        m_prev = m_ref[h]
        cm = jnp.max(jnp.where(causal, a_r, neg_inf), axis=1, keepdims=True)
        m_t = fc + jnp.maximum(m_prev, cm)
        dm = jnp.exp(jnp.where(causal, (fc - m_t) + a_r, neg_inf))
        qh = q_ref[:, hs]
        kth = kt_ref[hs, :]
        vh = v_ref[:, hs]
        sc = _dot(qh, kth) * dm
        inter = jnp.exp(fc + m_prev - m_t)
        cx = c_ref[h]
        g = _dot(qh, cx.astype(BF16))
        num = _dot(sc.astype(BF16), vh) + inter * g[:, 0:dh]
        den = jnp.sum(sc, axis=1, keepdims=True) + inter * g[:, dh:dh + 1]
        hh = num * (1.0 / jnp.maximum(jnp.abs(den), jnp.exp(-m_t)))
        hn = hh * lax.rsqrt(jnp.mean(hh * hh, axis=1, keepdims=True) + EPS) * gain_ref[:, hs]
        o_ref[:, hs] = (jax.nn.sigmoid(og_ref[:, hs]) * hn).astype(o_ref.dtype)

        m_last = m_t[L - 1:L, :]
        f_last = fc[L - 1:L, :]
        w_r = jnp.exp(f_last + a_r - m_last)
        decay = jnp.exp(f_last + m_prev - m_last)
        kw = (kth.astype(F32) * w_r).astype(BF16)
        v_ext = jnp.concatenate([vh, one_col], axis=1)
        c_ref[h] = decay * cx + _dot(kw, v_ext)
        m_ref[h] = m_last


def _mlstm(q, kt, v, og, gc, gr, gain, c0, m0, *, batch, seq, chunk, row0, kt_per_seq):
    mw = q.shape[1]
    heads, dh = c0.shape[1], c0.shape[2]
    nc = seq // chunk
    blk0 = row0 // chunk
    ng = gr.shape[-2]
    tok = lambda b, c: (blk0 + b * nc + c, 0)
    if kt_per_seq:
        kt_spec = pl.BlockSpec((None, mw, chunk), lambda b, c: (b, 0, c))
        gr_spec = pl.BlockSpec((None, ng, chunk), lambda b, c: (b, 0, c))
    else:
        kt_spec = pl.BlockSpec((mw, chunk), lambda b, c: (0, blk0 + b * nc + c))
        gr_spec = pl.BlockSpec((ng, chunk), lambda b, c: (0, blk0 + b * nc + c))
    kern = functools.partial(_mlstm_kernel, chunk=chunk, heads=heads, dh=dh)
    st_spec = pl.BlockSpec((None, heads, dh, dh + LANES), lambda b, c: (b, 0, 0, 0))
    m_spec = pl.BlockSpec((None, heads, 1, 1), lambda b, c: (b, 0, 0, 0))
    return pl.pallas_call(
        kern,
        out_shape=(jax.ShapeDtypeStruct((batch * seq, mw), BF16),
                   jax.ShapeDtypeStruct(c0.shape, F32),
                   jax.ShapeDtypeStruct(m0.shape, F32)),
        grid=(batch, nc),
        in_specs=[pl.BlockSpec((chunk, mw), tok),
                  kt_spec,
                  pl.BlockSpec((chunk, mw), tok),
                  pl.BlockSpec((chunk, mw), tok),
                  pl.BlockSpec((chunk, LANES), tok),
                  gr_spec,
                  _resident(gain.shape, lambda b, c: (0, 0)),
                  st_spec, m_spec],
        out_specs=(pl.BlockSpec((chunk, mw), lambda b, c: (b * nc + c, 0)), st_spec, m_spec),
        compiler_params=_cparams(("parallel", "arbitrary")),
        name="mlstm",
    )(q, kt, v, og, gc, gr, gain, c0, m0)


def _outproj_body(p_ref, m_ref, x_ref, g1_ref, sc_ref, sh_ref, wo_ref, wrt_ref, br_ref,
                  x1_ref, hp_ref, idx_ref, gate_ref, h_scr, *, pw, n_exp):
    mix = _dot(p_ref[...], wo_ref[0:pw, :]) + _dot(m_ref[...], wo_ref[pw:, :])
    x = x_ref[...]
    tm, d = x.shape
    for c in range(tm // MOD_CHUNK):
        rows = slice(c * MOD_CHUNK, (c + 1) * MOD_CHUNK)
        x1_ref[rows, :] = x[rows, :] + g1_ref[c:c + 1, :] * mix[rows, :]
    _modulated_norm(x1_ref[...], sc_ref, sh_ref, h_scr)
    hb = h_scr[...]
    hi = lax.bitcast_convert_type(hb[:, 0:d // 2].astype(F32), U32)
    lo = lax.bitcast_convert_type(hb[:, d // 2:].astype(F32), U32)
    words = (hi & jnp.uint32(0xFFFF0000)) | (lo >> 16)
    n_slab = words.shape[1] // LANES
    for k in range(n_slab):
        hp_ref[pl.ds(k, tm, stride=n_slab), :] = words[:, k * LANES:(k + 1) * LANES]

    logits = _dot_nt(wrt_ref[...], hb) + br_ref[...]
    e_iota = lax.broadcasted_iota(I32, logits.shape, 0)
    vals = []
    for j in range(TOP_K):
        mx = jnp.max(logits, axis=0, keepdims=True)
        ix = jnp.min(jnp.where(logits == mx, e_iota, n_exp), axis=0, keepdims=True)
        idx_ref[j:j + 1, :] = ix
        vals.append(mx)
        logits = jnp.where(e_iota == ix, -jnp.inf, logits)
    ex = [jnp.exp(v - vals[0]) for v in vals]
    tot = ex[0]
    for e in ex[1:]:
        tot = tot + e
    inv = 1.0 / tot
    for j in range(TOP_K):
        gate_ref[j:j + 1, :] = ex[j] * inv


def _outproj_kernel(pp_ref, ps_ref, mp_ref, ms_ref, xp_ref, xs_ref, g1_ref, sc_ref, sh_ref,
                    wo_ref, wrt_ref, br_ref, x1_ref, hp_ref, idx_ref, gate_ref, h_scr, *, n_p, pw, n_exp):
    i = pl.program_id(0)
    rest = (g1_ref, sc_ref, sh_ref, wo_ref, wrt_ref, br_ref, x1_ref, hp_ref, idx_ref, gate_ref, h_scr)

    @pl.when(i < n_p)
    def _():
        _outproj_body(pp_ref, mp_ref, xp_ref, *rest, pw=pw, n_exp=n_exp)

    @pl.when(i >= n_p)
    def _():
        _outproj_body(ps_ref, ms_ref, xs_ref, *rest, pw=pw, n_exp=n_exp)


def _outproj(pool_p, pool_s, ml_p, ml_s, xp, xs, g1, sc2, sh2, w_out, w_rt, b_r):
    tp, d = xp.shape
    ts = xs.shape[0]
    t = tp + ts
    pw = pool_p.shape[1]
    mw = ml_p.shape[1]
    n_exp = w_rt.shape[0]
    n_p = tp // TOK_TILE
    n_s = ts // TOK_TILE
    cpt = TOK_TILE // MOD_CHUNK
    kern = functools.partial(_outproj_kernel, n_p=n_p, pw=pw, n_exp=n_exp)
    tok = lambda i: (i, 0)
    pidx = lambda i: (jnp.minimum(i, n_p - 1), 0)
    sidx = lambda i: (jnp.maximum(i - n_p, 0), 0)
    return pl.pallas_call(
        kern,
        out_shape=(jax.ShapeDtypeStruct((t, d), F32),
                   jax.ShapeDtypeStruct((t * (d // 2 // LANES), LANES), U32),
                   jax.ShapeDtypeStruct((TOP_K, t), I32),
                   jax.ShapeDtypeStruct((TOP_K, t), F32)),
        grid=(n_p + n_s,),
        in_specs=[pl.BlockSpec((TOK_TILE, pw), pidx), pl.BlockSpec((TOK_TILE, pw), sidx),
                  pl.BlockSpec((TOK_TILE, mw), pidx), pl.BlockSpec((TOK_TILE, mw), sidx),
                  pl.BlockSpec((TOK_TILE, d), pidx), pl.BlockSpec((TOK_TILE, d), sidx),
                  _mod_spec(cpt, d, MOD_GATE1), _mod_spec(cpt, d, MOD_SCALE2), _mod_spec(cpt, d, MOD_SHIFT2),
                  _resident(w_out.shape, lambda i: (0, 0)),
                  _resident(w_rt.shape, lambda i: (0, 0)),
                  _resident(b_r.shape, lambda i: (0, 0))],
        out_specs=(pl.BlockSpec((TOK_TILE, d), tok),
                   pl.BlockSpec((TOK_TILE * (d // 2 // LANES), LANES), tok),
                   pl.BlockSpec((TOP_K, TOK_TILE), lambda i: (0, i)),
                   pl.BlockSpec((TOP_K, TOK_TILE), lambda i: (0, i))),
        scratch_shapes=[pltpu.VMEM((TOK_TILE, d), BF16)],
        compiler_params=_cparams(("parallel",)),
        name="outproj",
    )(pool_p, pool_s, ml_p, ml_s, xp, xs, g1, sc2, sh2, w_out, w_rt, b_r)


def _rank_kernel(idx_ref, lpos_ref, cnt_ref, *, n_exp):
    tm = idx_ref.shape[1]
    e_iota = lax.broadcasted_iota(I32, (n_exp, tm), 0)
    hots = [e_iota == idx_ref[j:j + 1, :] for j in range(TOP_K)]
    cnt = hots[0].astype(F32)
    for hot in hots[1:]:
        cnt = cnt + hot.astype(F32)
    cnt_b = cnt.astype(BF16)
    r = lax.broadcasted_iota(I32, (tm, tm), 0)
    c = lax.broadcasted_iota(I32, (tm, tm), 1)
    earlier_tok = _dot(cnt_b, (r < c).astype(BF16))
    er = lax.broadcasted_iota(I32, (n_exp, n_exp), 0)
    ec = lax.broadcasted_iota(I32, (n_exp, n_exp), 1)
    lower_exp = _dot((ec < er).astype(BF16), cnt_b)
    seg_start = jnp.sum(lower_exp, axis=1, keepdims=True)
    pos = earlier_tok + seg_start
    for j in range(TOP_K):
        lpos_ref[j:j + 1, :] = jnp.sum(jnp.where(hots[j], pos, 0.0), axis=0, keepdims=True).astype(I32)
    cnt_ref[...] = jnp.broadcast_to(jnp.sum(cnt, axis=1, keepdims=True), cnt_ref.shape)


def _ranks(idx_t, n_exp):
    k, t = idx_t.shape
    n_tiles = t // TOK_TILE
    return pl.pallas_call(
        functools.partial(_rank_kernel, n_exp=n_exp),
        out_shape=(jax.ShapeDtypeStruct((k, t), I32), jax.ShapeDtypeStruct((n_exp, n_tiles * LANES), F32)),
        grid=(n_tiles,),
        in_specs=[pl.BlockSpec((k, TOK_TILE), lambda i: (0, i))],
        out_specs=(pl.BlockSpec((k, TOK_TILE), lambda i: (0, i)),
                   pl.BlockSpec((n_exp, LANES), lambda i: (0, i))),
        compiler_params=_cparams(("parallel",)),
        name="ranks",
    )(idx_t)


def _tile_rows(first, n):
    start = first * SUBLANES
    if not isinstance(start, int):
        start = pl.multiple_of(start, SUBLANES)
    return pl.ds(start, n * SUBLANES)


def _slab(first, n, k):
    return pl.ds(first * SUBLANES + k, n, stride=SUBLANES)


def _segment_copies(count, make_copy, op):
    n_full = count >> SEG_SHIFT

    def full(k, carry):
        getattr(make_copy(k * SEG_CHUNK, SEG_CHUNK), op)()
        return carry

    lax.fori_loop(0, n_full, full, 0)
    rest = count - n_full * SEG_CHUNK
    base = n_full * SEG_CHUNK
    bit = SEG_CHUNK // 2
    while bit:
        off = base + (rest & -(2 * bit))

        def one(off=off, bit=bit):
            getattr(make_copy(off, bit), op)()

        pl.when((rest & bit) != 0)(one)
        bit //= 2


def _dispatch_kernel(lpos_ref, segc_ref, segl_ref, segg_ref, padrow_ref, padlen_ref, fillrow_ref, filln_ref,
                     h_ref, xs_ref, sorted_a, sorted_b, zero_scr, sem, *, n_steps, n_exp, n_regions):
    i = pl.program_id(0)
    tm = h_ref.shape[0] // SUBLANES
    base = i * tm
    slots = ((sorted_a, 0), (sorted_b, 1))

    def wait_sent(buf, which):
        pltpu.make_async_copy(buf, xs_ref.at[pl.ds(0, buf.shape[0]), :], sem.at[which]).wait()

    def run(buf, which):
        @pl.when(i >= 2)
        def _():
            wait_sent(buf, which)

        def place(t, carry):
            row = h_ref[_tile_rows(t, 1), :]
            for j in range(TOP_K):
                dst = pl.multiple_of(lpos_ref[(base + t) * TOP_K + j], SUBLANES)
                buf[pl.ds(dst, SUBLANES), :] = row
            return carry

        lax.fori_loop(0, tm, place, 0, unroll=ROW_UNROLL)

        def send(e, carry):
            s = i * n_exp + e
            src0 = segl_ref[s]
            dst0 = segg_ref[s]
            _segment_copies(segc_ref[s], lambda off, n: pltpu.make_async_copy(
                buf.at[_tile_rows(src0 + off, n), :], xs_ref.at[_tile_rows(dst0 + off, n), :], sem.at[which]), "start")
            return carry

        lax.fori_loop(0, n_exp, send, 0)

    for buf, which in slots:
        pl.when(i % 2 == which)(functools.partial(run, buf, which))

    @pl.when(i == 0)
    def _():
        zero_scr[...] = jnp.zeros_like(zero_scr)

        def fill_copy(e, c):
            return pltpu.make_async_copy(zero_scr, xs_ref.at[_tile_rows(fillrow_ref[e] + c * MOE_SUB, MOE_SUB), :],
                                         sem.at[2])

        def region(op):
            def go(e, carry):
                dst0 = padrow_ref[e]
                _segment_copies(padlen_ref[e], lambda off, n: pltpu.make_async_copy(
                    zero_scr.at[_tile_rows(0, n), :], xs_ref.at[_tile_rows(dst0 + off, n), :], sem.at[2]), op)
                lax.fori_loop(0, filln_ref[e], lambda k, c: (getattr(fill_copy(e, k), op)(), c)[1], 0)
                return carry
            return go

        lax.fori_loop(0, n_regions, region("start"), 0)
        lax.fori_loop(0, n_regions, region("wait"), 0)

    @pl.when(i == n_steps - 1)
    def _():
        for buf, which in slots[:min(n_steps, 2)]:
            wait_sent(buf, which)


def _dispatch(lpos_flat, seg_cnt, seg_l, seg_g, pad_row, pad_len, fill_row, fill_n, h_tiles, rows, n_exp):
    t = h_tiles.shape[0] // SUBLANES
    kern = functools.partial(_dispatch_kernel, n_steps=t // TOK_TILE, n_exp=n_exp, n_regions=pad_row.shape[0])
    return pl.pallas_call(
        kern,
        out_shape=jax.ShapeDtypeStruct((rows * SUBLANES, LANES), U32),
        grid_spec=pltpu.PrefetchScalarGridSpec(
            num_scalar_prefetch=8,
            grid=(t // TOK_TILE,),
            in_specs=[pl.BlockSpec((TOK_TILE * SUBLANES, LANES), lambda i, *_: (i, 0))],
            out_specs=pl.BlockSpec(memory_space=pl.ANY),
            scratch_shapes=[pltpu.VMEM((TOK_TILE * TOP_K * SUBLANES, LANES), U32),
                            pltpu.VMEM((TOK_TILE * TOP_K * SUBLANES, LANES), U32),
                            pltpu.VMEM((MOE_SUB * SUBLANES, LANES), U32),
                            pltpu.SemaphoreType.DMA((3,))]),
        compiler_params=_cparams(("arbitrary",)),
        name="dispatch",
    )(lpos_flat, seg_cnt, seg_l, seg_g, pad_row, pad_len, fill_row, fill_n, h_tiles)


def _unpack_rows(x_ref, first, n):
    his, los = [], []
    for k in range(SUBLANES):
        words = x_ref[_slab(first, n, k), :]
        his.append(lax.bitcast_convert_type(words & jnp.uint32(0xFFFF0000), F32).astype(BF16))
        los.append(lax.bitcast_convert_type(words << 16, F32).astype(BF16))
    return jnp.concatenate(his + los, axis=1)


def _moe_kernel(be_ref, bv_ref, na_ref, x_ref, wg_ref, wl_ref, bg_ref, bl_ref, wd_ref, bd_ref,
                ya_ref, yb_ref, acc_scr, sta_scr, stb_scr, wup_scr, wd_scr, sem, *, tf):
    b = pl.program_id(0)
    f = pl.program_id(1)
    n_blk = pl.num_programs(0)
    last_f = pl.num_programs(1) - 1
    rows_total = x_ref.shape[0] // SUBLANES
    n_sub_max = rows_total // MOE_SUB
    half = SUBLANES * LANES
    active = b < na_ref[0]

    def writeback(blk):
        dst = _tile_rows(blk * rows_total, rows_total)
        return (pltpu.make_async_copy(sta_scr, ya_ref.at[dst, :], sem.at[0]),
                pltpu.make_async_copy(stb_scr, yb_ref.at[dst, :], sem.at[1]))

    @pl.when(jnp.logical_and(f == last_f, b > 0))
    def _():
        for cp in writeback(b - 1):
            cp.wait()

    @pl.when(jnp.logical_and(b == na_ref[0], f == last_f))
    def _():
        sta_scr[...] = jnp.zeros_like(sta_scr)
        stb_scr[...] = jnp.zeros_like(stb_scr)

    @pl.when(jnp.logical_and(b == 0, f == 0))
    def _():
        acc_scr[...] = jnp.zeros_like(acc_scr)

    @pl.when(active)
    def _():
        n_sub = (bv_ref[b] + MOE_SUB - 1) // MOE_SUB

        def stage(s, y):
            rows = slice(s * MOE_SUB, (s + 1) * MOE_SUB)
            y = y + bd_ref[...]
            for k in range(SUBLANES):
                slab = _slab(s * MOE_SUB, MOE_SUB, k)
                sta_scr[slab, :] = y[:, k * LANES:(k + 1) * LANES]
                stb_scr[slab, :] = y[:, half + k * LANES:half + (k + 1) * LANES]
            acc_scr[rows, :] = jnp.zeros((MOE_SUB, acc_scr.shape[1]), F32)

        def body(n_live, to_staging):
            wup_scr[:, 0:tf] = wg_ref[...].astype(BF16)
            wup_scr[:, tf:] = wl_ref[...].astype(BF16)
            wd_scr[...] = wd_ref[...].astype(BF16)
            for s in range(n_live):
                rows = slice(s * MOE_SUB, (s + 1) * MOE_SUB)
                x = _unpack_rows(x_ref, s * MOE_SUB, MOE_SUB)
                up = _dot(x, wup_scr[...])
                g = jnp.minimum(up[:, 0:tf] + bg_ref[...], SWIGLU_LIMIT)
                lin = jnp.clip(up[:, tf:] + bl_ref[...], -SWIGLU_LIMIT, SWIGLU_LIMIT)
                act = g * jax.nn.sigmoid(SWIGLU_ALPHA * g) * (lin + 1.0)
                y = acc_scr[rows, :] + _dot(act.astype(BF16), wd_scr[...])
                if to_staging:
                    stage(s, y)
                else:
                    acc_scr[rows, :] = y

        full = n_sub == n_sub_max
        for n_live in range(1, n_sub_max):
            pl.when(n_sub == n_live)(functools.partial(body, n_live, False))
        pl.when(jnp.logical_and(full, f != last_f))(functools.partial(body, n_sub_max, False))
        pl.when(jnp.logical_and(full, f == last_f))(functools.partial(body, n_sub_max, True))

        @pl.when(jnp.logical_and(jnp.logical_not(full), f == last_f))
        def _():
            for s in range(n_sub_max):
                stage(s, acc_scr[s * MOE_SUB:(s + 1) * MOE_SUB, :])

    @pl.when(f == last_f)
    def _():
        for cp in writeback(b):
            cp.start()

        @pl.when(b == n_blk - 1)
        def _():
            for cp in writeback(b):
                cp.wait()


def _moe(blk_e, blk_valid, n_act, xs, w_up, b_up, w_down, b_down):
    rows = xs.shape[0] // SUBLANES
    d = 2 * SUBLANES * LANES
    n_exp, _, two_f = w_up.shape[1:]
    assert w_up.shape[2] == d, "one (8, 128) tile of packed words per row"
    ff = two_f // 2
    tf = MOE_FF_TILE
    nf = ff // tf
    nb = rows // MOE_ROWS
    b_up3 = b_up.reshape(n_exp, 1, two_f)
    b_dn3 = b_down.reshape(n_exp, 1, d)

    def blk(b, f, be, bv, na):
        return jnp.minimum(b, na[0] - 1)

    def ftile(b, f, be, bv, na):
        return jnp.where(b < na[0], f, nf - 1)

    y_shape = jax.ShapeDtypeStruct((rows * SUBLANES, LANES), F32)
    return pl.pallas_call(
        functools.partial(_moe_kernel, tf=tf),
        out_shape=(y_shape, y_shape),
        grid_spec=pltpu.PrefetchScalarGridSpec(
            num_scalar_prefetch=3,
            grid=(nb, nf),
            in_specs=[pl.BlockSpec((MOE_ROWS * SUBLANES, LANES), lambda *a: (blk(*a), 0)),
                      pl.BlockSpec((None, None, d, tf), lambda *a: (0, a[2][a[0]], 0, ftile(*a))),
                      pl.BlockSpec((None, None, d, tf), lambda *a: (0, a[2][a[0]], 0, ftile(*a) + nf)),
                      pl.BlockSpec((None, 1, tf), lambda *a: (a[2][a[0]], 0, ftile(*a))),
                      pl.BlockSpec((None, 1, tf), lambda *a: (a[2][a[0]], 0, ftile(*a) + nf)),
                      pl.BlockSpec((None, None, tf, d), lambda *a: (0, a[2][a[0]], ftile(*a), 0)),
                      pl.BlockSpec((None, 1, d), lambda *a: (a[2][a[0]], 0, 0))],
            out_specs=(pl.BlockSpec(memory_space=pl.ANY), pl.BlockSpec(memory_space=pl.ANY)),
            scratch_shapes=[pltpu.VMEM((MOE_ROWS, d), F32),
                            pltpu.VMEM((MOE_ROWS * SUBLANES, LANES), F32),
                            pltpu.VMEM((MOE_ROWS * SUBLANES, LANES), F32),
                            pltpu.VMEM((d, 2 * tf), BF16), pltpu.VMEM((tf, d), BF16),
                            pltpu.SemaphoreType.DMA((2,))]),
        compiler_params=_cparams(("arbitrary", "arbitrary")),
        name="moe_experts",
    )(blk_e, blk_valid, n_act, xs, w_up, w_up, b_up3, b_up3, w_down, b_dn3)


def _combine_kernel(lpos_ref, segc_ref, segl_ref, segg_ref, ya_ref, yb_ref, x1_ref, gate_ref, g2_ref, fg_ref,
                    yp_ref, yo_ref, bufa, bufb, outa, outb, sem, *, n_tok, n_exp, n_p):
    i = pl.program_id(0)
    n_steps = pl.num_programs(0)
    tm = x1_ref.shape[0]
    base = i * tm
    halves = ((ya_ref, bufa, outa, 0), (yb_ref, bufb, outb, 1))

    def fetch(tile, y_ref, buf, which):
        def go(e, carry):
            s = tile * n_exp + e
            src0 = segg_ref[s]
            dst0 = segl_ref[s]
            _segment_copies(segc_ref[s], lambda off, n: pltpu.make_async_copy(
                y_ref.at[_tile_rows(src0 + off, n), :], buf.at[_tile_rows(dst0 + off, n), :], sem.at[which]), "start")
            return carry
        lax.fori_loop(0, n_exp, go, 0)

    def mix(buf, out):
        def go(t, carry):
            acc = None
            for j in range(TOP_K):
                src = pl.ds(pl.multiple_of(lpos_ref[(base + t) * TOP_K + j], SUBLANES), SUBLANES)
                term = gate_ref[j, pl.ds(t, 1), :] * buf[src, :]
                acc = term if acc is None else acc + term
            out[_tile_rows(t, 1), :] = acc
            return carry
        lax.fori_loop(0, tm, go, 0, unroll=ROW_UNROLL)

    @pl.when(i == 0)
    def _():
        for y_ref, buf, _, which in halves:
            fetch(0, y_ref, buf, which)

    for y_ref, buf, out, which in halves:
        pltpu.make_async_copy(y_ref.at[pl.ds(0, buf.shape[0]), :], buf, sem.at[which]).wait()
        mix(buf, out)

        @pl.when(i + 1 < n_steps)
        def _():
            fetch(i + 1, y_ref, buf, which)

    def finish(out_ref):
        for c in range(tm // MOD_CHUNK):
            rows = slice(c * MOD_CHUNK, (c + 1) * MOD_CHUNK)
            moe = jnp.concatenate([outa[_slab(c * MOD_CHUNK, MOD_CHUNK, k), :] for k in range(SUBLANES)] +
                                  [outb[_slab(c * MOD_CHUNK, MOD_CHUNK, k), :] for k in range(SUBLANES)], axis=1)
            xo = x1_ref[rows, :] + g2_ref[c:c + 1, :] * moe
            out_ref[rows, :] = xo * lax.rsqrt(jnp.mean(xo * xo, axis=-1, keepdims=True) + EPS) * fg_ref[...]

    @pl.when(i < n_p)
    def _():
        finish(yp_ref)

    @pl.when(i >= n_p)
    def _():
        finish(yo_ref)


def _combine(lpos_flat, seg_cnt, seg_l, seg_g, ys_a, ys_b, x1, gate_x, g2, final_gain, tp, n_exp):
    t, d = x1.shape
    lanes = LANES
    ts = t - tp
    n_p = tp // TOK_TILE
    n_s = ts // TOK_TILE
    cpt = TOK_TILE // MOD_CHUNK
    kern = functools.partial(_combine_kernel, n_tok=t, n_exp=n_exp, n_p=n_p)
    tok = lambda i, *_: (i, 0)
    return pl.pallas_call(
        kern,
        out_shape=(jax.ShapeDtypeStruct((tp, d), F32), jax.ShapeDtypeStruct((ts, d), F32)),
        grid_spec=pltpu.PrefetchScalarGridSpec(
            num_scalar_prefetch=4,
            grid=(n_p + n_s,),
            in_specs=[pl.BlockSpec(memory_space=pl.ANY),
                      pl.BlockSpec(memory_space=pl.ANY),
                      pl.BlockSpec((TOK_TILE, d), tok),
                      pl.BlockSpec((TOP_K, TOK_TILE, lanes), lambda i, *_: (0, i, 0)),
                      _mod_spec(cpt, d, MOD_GATE2),
                      pl.BlockSpec((1, d), lambda i, *_: (0, 0))],
            out_specs=(pl.BlockSpec((TOK_TILE, d), lambda i, *_: (jnp.minimum(i, n_p - 1), 0)),
                       pl.BlockSpec((TOK_TILE, d), lambda i, *_: (jnp.maximum(i - n_p, 0), 0))),
            scratch_shapes=[pltpu.VMEM((TOK_TILE * TOP_K * SUBLANES, lanes), F32),
                            pltpu.VMEM((TOK_TILE * TOP_K * SUBLANES, lanes), F32),
                            pltpu.VMEM((TOK_TILE * SUBLANES, lanes), F32),
                            pltpu.VMEM((TOK_TILE * SUBLANES, lanes), F32),
                            pltpu.SemaphoreType.DMA((2,))]),
        compiler_params=_cparams(("arbitrary",)),
        name="combine",
    )(lpos_flat, seg_cnt, seg_l, seg_g, ys_a, ys_b, x1, gate_x, g2, final_gain)


def _state_ext(c_state, n_state):
    b, h, dh, _ = c_state.shape
    ct = jnp.swapaxes(c_state, -1, -2)
    pad = jnp.zeros((b, h, dh, LANES - 1), F32)
    return jnp.concatenate([ct, n_state[..., None], pad], axis=-1)


def _state_split(cx):
    dh = cx.shape[2]
    return jnp.swapaxes(cx[..., 0:dh], -1, -2), cx[..., dh]


def kernel(x_prompt, x_sample, state_pool, state_mlstm_C, state_mlstm_n, state_mlstm_m, c_prompt, c_sample,
           w_ada, b_ada, w_in, b_gate, pool_w, pool_scale, mlstm_gain, w_out, w_router, b_router,
           w_up, b_up, w_down, b_down, final_gain):
    depth = w_ada.shape[0]
    assert depth == 1, "single-layer trunk"
    bp, sp, d = x_prompt.shape
    bs, ss, _ = x_sample.shape
    tp, ts = bp * sp, bs * ss
    t = tp + ts
    pw = state_pool.shape[-1]
    heads, dh = state_mlstm_C.shape[2], state_mlstm_C.shape[3]
    mw = heads * dh
    n_exp = w_router.shape[-1]
    pad_rows = state_pool.shape[2]
    assert tp % TOK_TILE == 0 and ts % TOK_TILE == 0 and sp % TOK_TILE == 0
    assert ss % MOD_CHUNK == 0 and 2 * heads <= SUBLANES and pad_rows < POOL_HALO

    xp = x_prompt.reshape(tp, d)
    xs = x_sample.reshape(ts, d)

    n_c = bp + bs
    c_rows = -(-n_c // SUBLANES) * SUBLANES
    c_all = jnp.concatenate([c_prompt, c_sample, jnp.zeros((c_rows - n_c, d), F32)], axis=0)
    mod = _adaln(c_all, w_ada[0], b_ada)
    mod_c = jnp.concatenate([
        jnp.broadcast_to(mod[0:bp, None, :], (bp, sp // MOD_CHUNK, mod.shape[1])).reshape(tp // MOD_CHUNK, -1),
        jnp.broadcast_to(mod[bp:n_c, None, :], (bs, ss // MOD_CHUNK, mod.shape[1])).reshape(ts // MOD_CHUNK, -1)])
    sh1 = sc1 = g1 = sh2 = sc2 = g2 = mod_c

    w_in0 = w_in[0]
    o0 = pw
    w_main = jnp.concatenate([w_in0[:, 0:pw], w_in0[:, o0:o0 + mw], w_in0[:, o0 + 2 * mw:o0 + 4 * mw]],
                             axis=1).astype(BF16)
    w_kt = (w_in0[:, o0 + mw:o0 + 2 * mw] * (dh ** -0.5)).T.astype(BF16)
    w_g = w_in0[:, o0 + 4 * mw:]
    ng = w_g.shape[1]
    w_gc = jnp.pad(w_g, ((0, 0), (0, LANES - ng))).astype(BF16)
    w_gt = jnp.pad(w_g.T, ((0, SUBLANES - ng), (0, 0))).astype(BF16)
    b_gc = jnp.pad(b_gate[0], (0, LANES - ng)).reshape(1, LANES)
    b_gr = jnp.pad(b_gate[0], (0, SUBLANES - ng)).reshape(SUBLANES, 1)
    u, q, kt, v, og, gc, gr = _inproj(xp, xs, sc1, sh1, w_main, w_kt, w_gc, w_gt, b_gc, b_gr, pw, mw)

    pool_wb = pool_w[0].astype(BF16)
    zeros_p = jnp.zeros((bp, POOL_HALO, pw), F32)
    st_s = jnp.concatenate([jnp.zeros((bs, POOL_HALO - pad_rows, pw), F32), state_pool[0]], axis=1)
    pool_p = _pool(u, zeros_p, pool_wb, pool_scale, batch=bp, seq=sp, row0=0, tm=TOK_TILE, pos0=0)
    pool_s = _pool(u, st_s, pool_wb, pool_scale, batch=bs, seq=ss, row0=tp, tm=ss, pos0=PAST_LEN)
    new_pool_p = u[0:tp].reshape(bp, sp, pw)[:, sp - pad_rows:]
    new_pool_s = u[tp:].reshape(bs, ss, pw)[:, ss - pad_rows:]

    gain = mlstm_gain[0].reshape(1, mw)
    c0_p = jnp.zeros((bp, heads, dh, dh + LANES), F32)
    m0_p = jnp.zeros((bp, heads, 1, 1), F32)
    chunk_p = MLSTM_CHUNK if sp % MLSTM_CHUNK == 0 else MOD_CHUNK
    ml_p, cx_p, m_p = _mlstm(q, kt, v, og, gc, gr, gain, c0_p, m0_p, batch=bp, seq=sp, chunk=chunk_p,
                             row0=0, kt_per_seq=False)
    c0_s = _state_ext(state_mlstm_C[0], state_mlstm_n[0])
    m0_s = state_mlstm_m[0].reshape(bs, heads, 1, 1)
    kt_s = kt[:, tp:].reshape(mw, bs, ss).transpose(1, 0, 2)
    gr_s = gr[:, tp:].reshape(gr.shape[0], bs, ss).transpose(1, 0, 2)
    ml_s, cx_s, m_s = _mlstm(q, kt_s, v, og, gc, gr_s, gain, c0_s, m0_s, batch=bs, seq=ss, chunk=ss,
                             row0=tp, kt_per_seq=True)
    new_c_p, new_n_p = _state_split(cx_p)
    new_c_s, new_n_s = _state_split(cx_s)

    w_rt = w_router[0].T.astype(BF16)
    b_r = b_router[0].reshape(n_exp, 1)
    x1, h_packed, idx_t, gate_t = _outproj(pool_p, pool_s, ml_p, ml_s, xp, xs, g1, sc2, sh2,
                                           w_out[0].astype(BF16), w_rt, b_r)

    lpos_t, cnt = _ranks(idx_t, n_exp)
    lpos_flat = (lpos_t.T * SUBLANES).reshape(-1)
    c_te = cnt[:, ::LANES].T.astype(I32)
    counts = jnp.sum(c_te, axis=0)
    padded = (counts + MOE_ROWS - 1) // MOE_ROWS * MOE_ROWS
    pend = jnp.cumsum(padded)
    pstart = pend - padded
    seg_cnt = c_te.reshape(-1)
    seg_l = (jnp.cumsum(c_te, axis=1) - c_te).reshape(-1).astype(I32)
    seg_g = (pstart[None, :] + jnp.cumsum(c_te, axis=0) - c_te).reshape(-1).astype(I32)
    n_assign = t * TOP_K
    nb = -(-(n_assign + n_exp * (MOE_ROWS - 1)) // MOE_ROWS)
    n_act = pend[-1] // MOE_ROWS
    blk_ids = jnp.arange(nb, dtype=I32)
    last = jnp.minimum(blk_ids, n_act - 1)
    blk_e = jnp.minimum(jnp.searchsorted(pend, last * MOE_ROWS, side='right'), n_exp - 1).astype(I32)
    blk_valid = jnp.where(blk_ids < n_act,
                          jnp.clip(counts[blk_e] - (last * MOE_ROWS - pstart[blk_e]), 0, MOE_ROWS), 0).astype(I32)
    sub_end = (counts + MOE_SUB - 1) // MOE_SUB * MOE_SUB
    tail = pend[-1:]
    pad_row = jnp.concatenate([pstart + counts, tail]).astype(I32)
    pad_len = jnp.concatenate([sub_end - counts, jnp.zeros((1,), I32)]).astype(I32)
    fill_row = jnp.concatenate([pstart + sub_end, tail]).astype(I32)
    fill_n = jnp.concatenate([(padded - sub_end) // MOE_SUB, (nb * MOE_ROWS - tail) // MOE_SUB]).astype(I32)
    xs_grouped = _dispatch(lpos_flat, seg_cnt, seg_l, seg_g, pad_row, pad_len, fill_row, fill_n, h_packed,
                           nb * MOE_ROWS, n_exp)
    ys_a, ys_b = _moe(blk_e, blk_valid, n_act.reshape(1).astype(I32), xs_grouped, w_up, b_up[0], w_down, b_down[0])

    gate_x = jnp.broadcast_to(gate_t[:, :, None], (TOP_K, t, LANES))
    y_p, y_s = _combine(lpos_flat, seg_cnt, seg_l, seg_g, ys_a, ys_b, x1, gate_x, g2, final_gain.reshape(1, d),
                        tp, n_exp)

    return (y_p.reshape(bp, sp, d), y_s.reshape(bs, ss, d),
            new_pool_p[None], new_c_p[None], new_n_p[None], m_p.reshape(1, bp, heads),
            new_pool_s[None], new_c_s[None], new_n_s[None], m_s.reshape(1, bs, heads))
```

```python
import functools

import jax
import jax.numpy as jnp
from jax import lax
from jax.experimental import pallas as pl
from jax.experimental.pallas import tpu as pltpu

F32 = jnp.float32
BF16 = jnp.bfloat16
I32 = jnp.int32
U32 = jnp.uint32

EPS = 1e-6
POOL_WINDOWS = (2, 4, 8, 16)
TOP_K = 4
SWIGLU_LIMIT = 7.0
SWIGLU_ALPHA = 1.702
PAST_LEN = 2048

LANES = 128
SUBLANES = 8
MOD_CHUNK = 64
TOK_TILE = 512
POOL_HALO = 16
MLSTM_CHUNK = 256
ADA_TILE = 1024
MOE_ROWS = 768
MOE_SUB = 256
MOE_FF_TILE = 512
SEG_SHIFT = 4
SEG_CHUNK = 1 << SEG_SHIFT
ROW_UNROLL = 16
VMEM_LIMIT = 56 * 1024 * 1024


def _cparams(sem, vmem=VMEM_LIMIT):
    return pltpu.CompilerParams(dimension_semantics=sem, vmem_limit_bytes=vmem)


MOD_SHIFT1, MOD_SCALE1, MOD_GATE1, MOD_SHIFT2, MOD_SCALE2, MOD_GATE2 = range(6)


def _mod_spec(rows, d, which):
    return pl.BlockSpec((rows, d), lambda i, *_: (i, which))


def _resident(shape, index_map):
    return pl.BlockSpec(shape, index_map, pipeline_mode=pl.Buffered(1))


def _dot(a, b):
    return jnp.dot(a, b, preferred_element_type=F32)


def _dot_nt(a, b):
    return lax.dot_general(a, b, (((1,), (1,)), ((), ())), preferred_element_type=F32)


def _split3(x):
    hi = x.astype(BF16)
    r1 = x - hi.astype(F32)
    mid = r1.astype(BF16)
    lo = (r1 - mid.astype(F32)).astype(BF16)
    return hi, mid, lo


def _log_sigmoid(x):
    return jnp.minimum(x, 0.0) - jnp.log1p(jnp.exp(-jnp.abs(x)))


def _adaln_kernel(c_ref, w_ref, b_ref, o_ref):
    c = c_ref[...]
    s = (c * jax.nn.sigmoid(c)).astype(BF16)
    o_ref[...] = _dot(s, w_ref[...].astype(BF16)) + b_ref[...]


def _adaln(c_all, w_ada, b_ada):
    rows, d = c_all.shape
    n = w_ada.shape[1]
    return pl.pallas_call(
        _adaln_kernel,
        out_shape=jax.ShapeDtypeStruct((rows, n), F32),
        grid=(n // ADA_TILE,),
        in_specs=[pl.BlockSpec((rows, d), lambda j: (0, 0)),
                  pl.BlockSpec((d, ADA_TILE), lambda j: (0, j)),
                  pl.BlockSpec((1, ADA_TILE), lambda j: (0, j))],
        out_specs=pl.BlockSpec((rows, ADA_TILE), lambda j: (0, j)),
        compiler_params=_cparams(("parallel",)),
        name="adaln",
    )(c_all, w_ada, b_ada)


def _modulated_norm(x, sc_ref, sh_ref, h_scr):
    xn = x * lax.rsqrt(jnp.mean(x * x, axis=-1, keepdims=True) + EPS)
    for c in range(x.shape[0] // MOD_CHUNK):
        rows = slice(c * MOD_CHUNK, (c + 1) * MOD_CHUNK)
        h_scr[rows, :] = (xn[rows, :] * (1.0 + sc_ref[c:c + 1, :]) + sh_ref[c:c + 1, :]).astype(h_scr.dtype)


def _inproj_kernel(xp_ref, xs_ref, sc_ref, sh_ref, w_ref, wkt_ref, wg_ref, wgt_ref, bgc_ref, bgr_ref,
                   u_ref, q_ref, kt_ref, v_ref, og_ref, gc_ref, gr_ref, h_scr, *, n_p, pw, mw):
    i = pl.program_id(0)

    def body(x_ref):
        _modulated_norm(x_ref[...], sc_ref, sh_ref, h_scr)
        h = h_scr[...]
        u_ref[...] = _dot(h, w_ref[:, 0:pw])
        q_ref[...] = _dot(h, w_ref[:, pw:pw + mw]).astype(BF16)
        v_ref[...] = _dot(h, w_ref[:, pw + mw:pw + 2 * mw]).astype(BF16)
        og_ref[...] = _dot(h, w_ref[:, pw + 2 * mw:pw + 3 * mw])
        kt_ref[...] = _dot_nt(wkt_ref[...], h).astype(BF16)
        gc_ref[...] = _dot(h, wg_ref[...]) + bgc_ref[...]
        gr_ref[...] = _dot_nt(wgt_ref[...], h) + bgr_ref[...]

    pl.when(i < n_p)(functools.partial(body, xp_ref))
    pl.when(i >= n_p)(functools.partial(body, xs_ref))


def _inproj(xp, xs, sc1, sh1, w_main, w_kt, w_gc, w_gt, b_gc, b_gr, pw, mw):
    tp, d = xp.shape
    ts = xs.shape[0]
    t = tp + ts
    n_p = tp // TOK_TILE
    n_s = ts // TOK_TILE
    cpt = TOK_TILE // MOD_CHUNK
    ng = w_gt.shape[0]
    kern = functools.partial(_inproj_kernel, n_p=n_p, pw=pw, mw=mw)
    tok = lambda i: (i, 0)
    return pl.pallas_call(
        kern,
        out_shape=(jax.ShapeDtypeStruct((t, pw), F32),
                   jax.ShapeDtypeStruct((t, mw), BF16),
                   jax.ShapeDtypeStruct((mw, t), BF16),
                   jax.ShapeDtypeStruct((t, mw), BF16),
                   jax.ShapeDtypeStruct((t, mw), F32),
                   jax.ShapeDtypeStruct((t, LANES), F32),
                   jax.ShapeDtypeStruct((ng, t), F32)),
        grid=(n_p + n_s,),
        in_specs=[pl.BlockSpec((TOK_TILE, d), lambda i: (jnp.minimum(i, n_p - 1), 0)),
                  pl.BlockSpec((TOK_TILE, d), lambda i: (jnp.maximum(i - n_p, 0), 0)),
                  _mod_spec(cpt, d, MOD_SCALE1),
                  _mod_spec(cpt, d, MOD_SHIFT1),
                  _resident(w_main.shape, lambda i: (0, 0)),
                  _resident(w_kt.shape, lambda i: (0, 0)),
                  _resident(w_gc.shape, lambda i: (0, 0)),
                  _resident(w_gt.shape, lambda i: (0, 0)),
                  _resident(b_gc.shape, lambda i: (0, 0)),
                  _resident(b_gr.shape, lambda i: (0, 0))],
        out_specs=(pl.BlockSpec((TOK_TILE, pw), tok),
                   pl.BlockSpec((TOK_TILE, mw), tok),
                   pl.BlockSpec((mw, TOK_TILE), lambda i: (0, i)),
                   pl.BlockSpec((TOK_TILE, mw), tok),
                   pl.BlockSpec((TOK_TILE, mw), tok),
                   pl.BlockSpec((TOK_TILE, LANES), tok),
                   pl.BlockSpec((ng, TOK_TILE), lambda i: (0, i))),
        scratch_shapes=[pltpu.VMEM((TOK_TILE, d), BF16)],
        compiler_params=_cparams(("parallel",)),
        name="inproj",
    )(xp, xs, sc1, sh1, w_main, w_kt, w_gc, w_gt, b_gc, b_gr)


def _pool_kernel(u_ref, st_ref, pw_ref, ps_ref, o_ref, xp_scr, *, tm, pos0, group):
    j = pl.program_id(1)

    @pl.when(j == 0)
    def _():
        xp_scr[0:POOL_HALO, :] = st_ref[...]

    @pl.when(j > 0)
    def _():
        xp_scr[0:POOL_HALO, :] = xp_scr[tm:tm + POOL_HALO, :]

    xp_scr[POOL_HALO:POOL_HALO + tm, :] = u_ref[...]
    pos = pos0 + j * tm + lax.broadcasted_iota(I32, (tm, group), 0)
    for g, w in enumerate(POOL_WINDOWS):
        cs = slice(g * group, (g + 1) * group)
        x = xp_scr[POOL_HALO:POOL_HALO + tm, cs]
        acc = x
        for s in range(1, w):
            acc = acc + xp_scr[POOL_HALO - s:POOL_HALO - s + tm, cs]
        cnt = jnp.minimum(pos + 1, w).astype(F32)
        pooled = acc / cnt - x
        mixed = _dot(pooled.astype(BF16), pw_ref[g]) * ps_ref[:, cs]
        o_ref[:, cs] = mixed.astype(o_ref.dtype)


def _pool(u, state16, pool_w, pool_scale, *, batch, seq, row0, tm, pos0):
    c = u.shape[1]
    group = c // len(POOL_WINDOWS)
    nt = seq // tm
    blk0 = row0 // tm
    kern = functools.partial(_pool_kernel, tm=tm, pos0=pos0, group=group)
    return pl.pallas_call(
        kern,
        out_shape=jax.ShapeDtypeStruct((batch * seq, c), BF16),
        grid=(batch, nt),
        in_specs=[pl.BlockSpec((tm, c), lambda b, j: (blk0 + b * nt + j, 0)),
                  pl.BlockSpec((None, POOL_HALO, c), lambda b, j: (b, 0, 0)),
                  _resident(pool_w.shape, lambda b, j: (0, 0, 0)),
                  _resident(pool_scale.shape, lambda b, j: (0, 0))],
        out_specs=pl.BlockSpec((tm, c), lambda b, j: (b * nt + j, 0)),
        scratch_shapes=[pltpu.VMEM((POOL_HALO + tm, c), F32)],
        compiler_params=_cparams(("parallel", "arbitrary")),
        name="pool",
    )(u, state16, pool_w, pool_scale)


def _mlstm_kernel(q_ref, kt_ref, v_ref, og_ref, gc_ref, gr_ref, gain_ref, c0_ref, m0_ref,
                  o_ref, c_ref, m_ref, *, chunk, heads, dh):
    ci = pl.program_id(1)
    L = chunk

    @pl.when(ci == 0)
    def _():
        c_ref[...] = c0_ref[...]
        m_ref[...] = m0_ref[...]

    gc = gc_ref[...]
    gr = gr_ref[...]
    lf_c = _log_sigmoid(gc)
    lf_r = _log_sigmoid(gr)
    row_i = lax.broadcasted_iota(I32, (L, L), 0)
    col_i = lax.broadcasted_iota(I32, (L, L), 1)
    causal = col_i <= row_i
    tri = causal.astype(BF16)
    tri_t = (row_i <= col_i).astype(BF16)
    f_c = sum(_dot(tri, p) for p in _split3(lf_c))
    f_r = sum(_dot(p, tri_t) for p in _split3(lf_r))
    one_col = (lax.broadcasted_iota(I32, (L, LANES), 1) == 0).astype(BF16)
    neg_inf = jnp.float32(-jnp.inf)

    for h in range(heads):
        hs = slice(h * dh, (h + 1) * dh)
        fc = f_c[:, heads + h:heads + h + 1]
        a_r = gr[h:h + 1, :] - f_r[heads + h:heads + h + 1, :]
        m_prev = m_ref[h]
        cm = jnp.max(jnp.where(causal, a_r, neg_inf), axis=1, keepdims=True)
        m_t = fc + jnp.maximum(m_prev, cm)
        dm = jnp.exp(jnp.where(causal, (fc - m_t) + a_r, neg_inf))
        qh = q_ref[:, hs]
        kth = kt_ref[hs, :]
        vh = v_ref[:, hs]
        sc = _dot(qh, kth) * dm
        inter = jnp.exp(fc + m_prev - m_t)
        cx = c_ref[h]
        g = _dot(qh, cx.astype(BF16))
        num = _dot(sc.astype(BF16), vh) + inter * g[:, 0:dh]
        den = jnp.sum(sc, axis=1, keepdims=True) + inter * g[:, dh:dh + 1]
        hh = num * (1.0 / jnp.maximum(jnp.abs(den), jnp.exp(-m_t)))
        hn = hh * lax.rsqrt(jnp.mean(hh * hh, axis=1, keepdims=True) + EPS) * gain_ref[:, hs]
        o_ref[:, hs] = (jax.nn.sigmoid(og_ref[:, hs]) * hn).astype(o_ref.dtype)

        m_last = m_t[L - 1:L, :]
        f_last = fc[L - 1:L, :]
        w_r = jnp.exp(f_last + a_r - m_last)
        decay = jnp.exp(f_last + m_prev - m_last)
        kw = (kth.astype(F32) * w_r).astype(BF16)
        v_ext = jnp.concatenate([vh, one_col], axis=1)
        c_ref[h] = decay * cx + _dot(kw, v_ext)
        m_ref[h] = m_last


def _mlstm(q, kt, v, og, gc, gr, gain, c0, m0, *, batch, seq, chunk, row0, kt_per_seq):
    mw = q.shape[1]
    heads, dh = c0.shape[1], c0.shape[2]
    nc = seq // chunk
    blk0 = row0 // chunk
    ng = gr.shape[-2]
    tok = lambda b, c: (blk0 + b * nc + c, 0)
    if kt_per_seq:
        kt_spec = pl.BlockSpec((None, mw, chunk), lambda b, c: (b, 0, c))
        gr_spec = pl.BlockSpec((None, ng, chunk), lambda b, c: (b, 0, c))
    else:
        kt_spec = pl.BlockSpec((mw, chunk), lambda b, c: (0, blk0 + b * nc + c))
        gr_spec = pl.BlockSpec((ng, chunk), lambda b, c: (0, blk0 + b * nc + c))
    kern = functools.partial(_mlstm_kernel, chunk=chunk, heads=heads, dh=dh)
    st_spec = pl.BlockSpec((None, heads, dh, dh + LANES), lambda b, c: (b, 0, 0, 0))
    m_spec = pl.BlockSpec((None, heads, 1, 1), lambda b, c: (b, 0, 0, 0))
    return pl.pallas_call(
        kern,
        out_shape=(jax.ShapeDtypeStruct((batch * seq, mw), BF16),
                   jax.ShapeDtypeStruct(c0.shape, F32),
                   jax.ShapeDtypeStruct(m0.shape, F32)),
        grid=(batch, nc),
        in_specs=[pl.BlockSpec((chunk, mw), tok),
                  kt_spec,
                  pl.BlockSpec((chunk, mw), tok),
                  pl.BlockSpec((chunk, mw), tok),
                  pl.BlockSpec((chunk, LANES), tok),
                  gr_spec,
                  _resident(gain.shape, lambda b, c: (0, 0)),
                  st_spec, m_spec],
        out_specs=(pl.BlockSpec((chunk, mw), lambda b, c: (b * nc + c, 0)), st_spec, m_spec),
        compiler_params=_cparams(("parallel", "arbitrary")),
        name="mlstm",
    )(q, kt, v, og, gc, gr, gain, c0, m0)


def _outproj_body(p_ref, m_ref, x_ref, g1_ref, sc_ref, sh_ref, wo_ref, wrt_ref, br_ref,
                  x1_ref, hp_ref, idx_ref, gate_ref, h_scr, *, pw, n_exp):
    mix = _dot(p_ref[...], wo_ref[0:pw, :]) + _dot(m_ref[...], wo_ref[pw:, :])
    x = x_ref[...]
    tm, d = x.shape
    for c in range(tm // MOD_CHUNK):
        rows = slice(c * MOD_CHUNK, (c + 1) * MOD_CHUNK)
        x1_ref[rows, :] = x[rows, :] + g1_ref[c:c + 1, :] * mix[rows, :]
    _modulated_norm(x1_ref[...], sc_ref, sh_ref, h_scr)
    hb = h_scr[...]
    hi = lax.bitcast_convert_type(hb[:, 0:d // 2].astype(F32), U32)
    lo = lax.bitcast_convert_type(hb[:, d // 2:].astype(F32), U32)
    words = (hi & jnp.uint32(0xFFFF0000)) | (lo >> 16)
    n_slab = words.shape[1] // LANES
    for k in range(n_slab):
        hp_ref[pl.ds(k, tm, stride=n_slab), :] = words[:, k * LANES:(k + 1) * LANES]

    logits = _dot_nt(wrt_ref[...], hb) + br_ref[...]
    e_iota = lax.broadcasted_iota(I32, logits.shape, 0)
    vals = []
    for j in range(TOP_K):
        mx = jnp.max(logits, axis=0, keepdims=True)
        ix = jnp.min(jnp.where(logits == mx, e_iota, n_exp), axis=0, keepdims=True)
        idx_ref[j:j + 1, :] = ix
        vals.append(mx)
        logits = jnp.where(e_iota == ix, -jnp.inf, logits)
    ex = [jnp.exp(v - vals[0]) for v in vals]
    tot = ex[0]
    for e in ex[1:]:
        tot = tot + e
    inv = 1.0 / tot
    for j in range(TOP_K):
        gate_ref[j:j + 1, :] = ex[j] * inv


def _outproj_kernel(pp_ref, ps_ref, mp_ref, ms_ref, xp_ref, xs_ref, g1_ref, sc_ref, sh_ref,
                    wo_ref, wrt_ref, br_ref, x1_ref, hp_ref, idx_ref, gate_ref, h_scr, *, n_p, pw, n_exp):
    i = pl.program_id(0)
    rest = (g1_ref, sc_ref, sh_ref, wo_ref, wrt_ref, br_ref, x1_ref, hp_ref, idx_ref, gate_ref, h_scr)

    @pl.when(i < n_p)
    def _():
        _outproj_body(pp_ref, mp_ref, xp_ref, *rest, pw=pw, n_exp=n_exp)

    @pl.when(i >= n_p)
    def _():
        _outproj_body(ps_ref, ms_ref, xs_ref, *rest, pw=pw, n_exp=n_exp)


def _outproj(pool_p, pool_s, ml_p, ml_s, xp, xs, g1, sc2, sh2, w_out, w_rt, b_r):
    tp, d = xp.shape
    ts = xs.shape[0]
    t = tp + ts
    pw = pool_p.shape[1]
    mw = ml_p.shape[1]
    n_exp = w_rt.shape[0]
    n_p = tp // TOK_TILE
    n_s = ts // TOK_TILE
    cpt = TOK_TILE // MOD_CHUNK
    kern = functools.partial(_outproj_kernel, n_p=n_p, pw=pw, n_exp=n_exp)
    tok = lambda i: (i, 0)
    pidx = lambda i: (jnp.minimum(i, n_p - 1), 0)
    sidx = lambda i: (jnp.maximum(i - n_p, 0), 0)
    return pl.pallas_call(
        kern,
        out_shape=(jax.ShapeDtypeStruct((t, d), F32),
                   jax.ShapeDtypeStruct((t * (d // 2 // LANES), LANES), U32),
                   jax.ShapeDtypeStruct((TOP_K, t), I32),
                   jax.ShapeDtypeStruct((TOP_K, t), F32)),
        grid=(n_p + n_s,),
        in_specs=[pl.BlockSpec((TOK_TILE, pw), pidx), pl.BlockSpec((TOK_TILE, pw), sidx),
                  pl.BlockSpec((TOK_TILE, mw), pidx), pl.BlockSpec((TOK_TILE, mw), sidx),
                  pl.BlockSpec((TOK_TILE, d), pidx), pl.BlockSpec((TOK_TILE, d), sidx),
                  _mod_spec(cpt, d, MOD_GATE1), _mod_spec(cpt, d, MOD_SCALE2), _mod_spec(cpt, d, MOD_SHIFT2),
                  _resident(w_out.shape, lambda i: (0, 0)),
                  _resident(w_rt.shape, lambda i: (0, 0)),
                  _resident(b_r.shape, lambda i: (0, 0))],
        out_specs=(pl.BlockSpec((TOK_TILE, d), tok),
                   pl.BlockSpec((TOK_TILE * (d // 2 // LANES), LANES), tok),
                   pl.BlockSpec((TOP_K, TOK_TILE), lambda i: (0, i)),
                   pl.BlockSpec((TOP_K, TOK_TILE), lambda i: (0, i))),
        scratch_shapes=[pltpu.VMEM((TOK_TILE, d), BF16)],
        compiler_params=_cparams(("parallel",)),
        name="outproj",
    )(pool_p, pool_s, ml_p, ml_s, xp, xs, g1, sc2, sh2, w_out, w_rt, b_r)


def _rank_kernel(idx_ref, lpos_ref, cnt_ref, *, n_exp):
    tm = idx_ref.shape[1]
    e_iota = lax.broadcasted_iota(I32, (n_exp, tm), 0)
    hots = [e_iota == idx_ref[j:j + 1, :] for j in range(TOP_K)]
    cnt = hots[0].astype(F32)
    for hot in hots[1:]:
        cnt = cnt + hot.astype(F32)
    cnt_b = cnt.astype(BF16)
    r = lax.broadcasted_iota(I32, (tm, tm), 0)
    c = lax.broadcasted_iota(I32, (tm, tm), 1)
    earlier_tok = _dot(cnt_b, (r < c).astype(BF16))
    er = lax.broadcasted_iota(I32, (n_exp, n_exp), 0)
    ec = lax.broadcasted_iota(I32, (n_exp, n_exp), 1)
    lower_exp = _dot((ec < er).astype(BF16), cnt_b)
    seg_start = jnp.sum(lower_exp, axis=1, keepdims=True)
    pos = earlier_tok + seg_start
    for j in range(TOP_K):
        lpos_ref[j:j + 1, :] = jnp.sum(jnp.where(hots[j], pos, 0.0), axis=0, keepdims=True).astype(I32)
    cnt_ref[...] = jnp.broadcast_to(jnp.sum(cnt, axis=1, keepdims=True), cnt_ref.shape)


def _ranks(idx_t, n_exp):
    k, t = idx_t.shape
    n_tiles = t // TOK_TILE
    return pl.pallas_call(
        functools.partial(_rank_kernel, n_exp=n_exp),
        out_shape=(jax.ShapeDtypeStruct((k, t), I32), jax.ShapeDtypeStruct((n_exp, n_tiles * LANES), F32)),
        grid=(n_tiles,),
        in_specs=[pl.BlockSpec((k, TOK_TILE), lambda i: (0, i))],
        out_specs=(pl.BlockSpec((k, TOK_TILE), lambda i: (0, i)),
                   pl.BlockSpec((n_exp, LANES), lambda i: (0, i))),
        compiler_params=_cparams(("parallel",)),
        name="ranks",
    )(idx_t)


def _tile_rows(first, n):
    start = first * SUBLANES
    if not isinstance(start, int):
        start = pl.multiple_of(start, SUBLANES)
    return pl.ds(start, n * SUBLANES)


def _slab(first, n, k):
    return pl.ds(first * SUBLANES + k, n, stride=SUBLANES)


def _segment_copies(count, make_copy, op):
    n_full = count >> SEG_SHIFT

    def full(k, carry):
        getattr(make_copy(k * SEG_CHUNK, SEG_CHUNK), op)()
        return carry

    lax.fori_loop(0, n_full, full, 0)
    rest = count - n_full * SEG_CHUNK
    base = n_full * SEG_CHUNK
    bit = SEG_CHUNK // 2
    while bit:
        off = base + (rest & -(2 * bit))

        def one(off=off, bit=bit):
            getattr(make_copy(off, bit), op)()

        pl.when((rest & bit) != 0)(one)
        bit //= 2


def _dispatch_kernel(lpos_ref, segc_ref, segl_ref, segg_ref, padrow_ref, padlen_ref, fillrow_ref, filln_ref,
                     h_ref, xs_ref, sorted_a, sorted_b, zero_scr, sem, *, n_steps, n_exp, n_regions):
    i = pl.program_id(0)
    tm = h_ref.shape[0] // SUBLANES
    base = i * tm
    slots = ((sorted_a, 0), (sorted_b, 1))

    def wait_sent(buf, which):
        pltpu.make_async_copy(buf, xs_ref.at[pl.ds(0, buf.shape[0]), :], sem.at[which]).wait()

    def run(buf, which):
        @pl.when(i >= 2)
        def _():
            wait_sent(buf, which)

        def place(t, carry):
            row = h_ref[_tile_rows(t, 1), :]
            for j in range(TOP_K):
                dst = pl.multiple_of(lpos_ref[(base + t) * TOP_K + j], SUBLANES)
                buf[pl.ds(dst, SUBLANES), :] = row
            return carry

        lax.fori_loop(0, tm, place, 0, unroll=ROW_UNROLL)

        def send(e, carry):
            s = i * n_exp + e
            src0 = segl_ref[s]
            dst0 = segg_ref[s]
            _segment_copies(segc_ref[s], lambda off, n: pltpu.make_async_copy(
                buf.at[_tile_rows(src0 + off, n), :], xs_ref.at[_tile_rows(dst0 + off, n), :], sem.at[which]), "start")
            return carry

        lax.fori_loop(0, n_exp, send, 0)

    for buf, which in slots:
        pl.when(i % 2 == which)(functools.partial(run, buf, which))

    @pl.when(i == 0)
    def _():
        zero_scr[...] = jnp.zeros_like(zero_scr)

        def fill_copy(e, c):
            return pltpu.make_async_copy(zero_scr, xs_ref.at[_tile_rows(fillrow_ref[e] + c * MOE_SUB, MOE_SUB), :],
                                         sem.at[2])

        def region(op):
            def go(e, carry):
                dst0 = padrow_ref[e]
                _segment_copies(padlen_ref[e], lambda off, n: pltpu.make_async_copy(
                    zero_scr.at[_tile_rows(0, n), :], xs_ref.at[_tile_rows(dst0 + off, n), :], sem.at[2]), op)
                lax.fori_loop(0, filln_ref[e], lambda k, c: (getattr(fill_copy(e, k), op)(), c)[1], 0)
                return carry
            return go

        lax.fori_loop(0, n_regions, region("start"), 0)
        lax.fori_loop(0, n_regions, region("wait"), 0)

    @pl.when(i == n_steps - 1)
    def _():
        for buf, which in slots[:min(n_steps, 2)]:
            wait_sent(buf, which)


def _dispatch(lpos_flat, seg_cnt, seg_l, seg_g, pad_row, pad_len, fill_row, fill_n, h_tiles, rows, n_exp):
    t = h_tiles.shape[0] // SUBLANES
    kern = functools.partial(_dispatch_kernel, n_steps=t // TOK_TILE, n_exp=n_exp, n_regions=pad_row.shape[0])
    return pl.pallas_call(
        kern,
        out_shape=jax.ShapeDtypeStruct((rows * SUBLANES, LANES), U32),
        grid_spec=pltpu.PrefetchScalarGridSpec(
            num_scalar_prefetch=8,
            grid=(t // TOK_TILE,),
            in_specs=[pl.BlockSpec((TOK_TILE * SUBLANES, LANES), lambda i, *_: (i, 0))],
            out_specs=pl.BlockSpec(memory_space=pl.ANY),
            scratch_shapes=[pltpu.VMEM((TOK_TILE * TOP_K * SUBLANES, LANES), U32),
                            pltpu.VMEM((TOK_TILE * TOP_K * SUBLANES, LANES), U32),
                            pltpu.VMEM((MOE_SUB * SUBLANES, LANES), U32),
                            pltpu.SemaphoreType.DMA((3,))]),
        compiler_params=_cparams(("arbitrary",)),
        name="dispatch",
    )(lpos_flat, seg_cnt, seg_l, seg_g, pad_row, pad_len, fill_row, fill_n, h_tiles)


def _unpack_rows(x_ref, first, n):
    his, los = [], []
    for k in range(SUBLANES):
        words = x_ref[_slab(first, n, k), :]
        his.append(lax.bitcast_convert_type(words & jnp.uint32(0xFFFF0000), F32).astype(BF16))
        los.append(lax.bitcast_convert_type(words << 16, F32).astype(BF16))
    return jnp.concatenate(his + los, axis=1)


def _moe_kernel(be_ref, bv_ref, na_ref, x_ref, wg_ref, wl_ref, bg_ref, bl_ref, wd_ref, bd_ref,
                ya_ref, yb_ref, acc_scr, sta_scr, stb_scr, wup_scr, wd_scr, sem, *, tf):
    b = pl.program_id(0)
    f = pl.program_id(1)
    n_blk = pl.num_programs(0)
    last_f = pl.num_programs(1) - 1
    rows_total = x_ref.shape[0] // SUBLANES
    n_sub_max = rows_total // MOE_SUB
    half = SUBLANES * LANES
    active = b < na_ref[0]

    def writeback(blk):
        dst = _tile_rows(blk * rows_total, rows_total)
        return (pltpu.make_async_copy(sta_scr, ya_ref.at[dst, :], sem.at[0]),
                pltpu.make_async_copy(stb_scr, yb_ref.at[dst, :], sem.at[1]))

    @pl.when(jnp.logical_and(f == last_f, b > 0))
    def _():
        for cp in writeback(b - 1):
            cp.wait()

    @pl.when(jnp.logical_and(b == na_ref[0], f == last_f))
    def _():
        sta_scr[...] = jnp.zeros_like(sta_scr)
        stb_scr[...] = jnp.zeros_like(stb_scr)

    @pl.when(active)
    def _():
        @pl.when(f == 0)
        def _():
            acc_scr[...] = jnp.broadcast_to(bd_ref[...], acc_scr.shape)

        n_sub = (bv_ref[b] + MOE_SUB - 1) // MOE_SUB

        def stage(s, y):
            for k in range(SUBLANES):
                slab = _slab(s * MOE_SUB, MOE_SUB, k)
                sta_scr[slab, :] = y[:, k * LANES:(k + 1) * LANES]
                stb_scr[slab, :] = y[:, half + k * LANES:half + (k + 1) * LANES]

        def body(n_live, to_staging):
            wup_scr[:, 0:tf] = wg_ref[...].astype(BF16)
            wup_scr[:, tf:] = wl_ref[...].astype(BF16)
            wd_scr[...] = wd_ref[...].astype(BF16)
            for s in range(n_live):
                rows = slice(s * MOE_SUB, (s + 1) * MOE_SUB)
                x = _unpack_rows(x_ref, s * MOE_SUB, MOE_SUB)
                up = _dot(x, wup_scr[...])
                g = jnp.minimum(up[:, 0:tf] + bg_ref[...], SWIGLU_LIMIT)
                lin = jnp.clip(up[:, tf:] + bl_ref[...], -SWIGLU_LIMIT, SWIGLU_LIMIT)
                act = g * jax.nn.sigmoid(SWIGLU_ALPHA * g) * (lin + 1.0)
                y = acc_scr[rows, :] + _dot(act.astype(BF16), wd_scr[...])
                if to_staging:
                    stage(s, y)
                else:
                    acc_scr[rows, :] = y

        full = n_sub == n_sub_max
        for n_live in range(1, n_sub_max):
            pl.when(n_sub == n_live)(functools.partial(body, n_live, False))
        pl.when(jnp.logical_and(full, f != last_f))(functools.partial(body, n_sub_max, False))
        pl.when(jnp.logical_and(full, f == last_f))(functools.partial(body, n_sub_max, True))

        @pl.when(jnp.logical_and(jnp.logical_not(full), f == last_f))
        def _():
            for s in range(n_sub_max):
                stage(s, acc_scr[s * MOE_SUB:(s + 1) * MOE_SUB, :])

    @pl.when(f == last_f)
    def _():
        for cp in writeback(b):
            cp.start()

        @pl.when(b == n_blk - 1)
        def _():
            for cp in writeback(b):
                cp.wait()


def _moe(blk_e, blk_valid, n_act, xs, w_up, b_up, w_down, b_down):
    rows = xs.shape[0] // SUBLANES
    d = 2 * SUBLANES * LANES
    n_exp, _, two_f = w_up.shape[1:]
    assert w_up.shape[2] == d, "one (8, 128) tile of packed words per row"
    ff = two_f // 2
    tf = MOE_FF_TILE
    nf = ff // tf
    nb = rows // MOE_ROWS
    b_up3 = b_up.reshape(n_exp, 1, two_f)
    b_dn3 = b_down.reshape(n_exp, 1, d)

    def blk(b, f, be, bv, na):
        return jnp.minimum(b, na[0] - 1)

    def ftile(b, f, be, bv, na):
        return jnp.where(b < na[0], f, nf - 1)

    y_shape = jax.ShapeDtypeStruct((rows * SUBLANES, LANES), F32)
    return pl.pallas_call(
        functools.partial(_moe_kernel, tf=tf),
        out_shape=(y_shape, y_shape),
        grid_spec=pltpu.PrefetchScalarGridSpec(
            num_scalar_prefetch=3,
            grid=(nb, nf),
            in_specs=[pl.BlockSpec((MOE_ROWS * SUBLANES, LANES), lambda *a: (blk(*a), 0)),
                      pl.BlockSpec((None, None, d, tf), lambda *a: (0, a[2][a[0]], 0, ftile(*a))),
                      pl.BlockSpec((None, None, d, tf), lambda *a: (0, a[2][a[0]], 0, ftile(*a) + nf)),
                      pl.BlockSpec((None, 1, tf), lambda *a: (a[2][a[0]], 0, ftile(*a))),
                      pl.BlockSpec((None, 1, tf), lambda *a: (a[2][a[0]], 0, ftile(*a) + nf)),
                      pl.BlockSpec((None, None, tf, d), lambda *a: (0, a[2][a[0]], ftile(*a), 0)),
                      pl.BlockSpec((None, 1, d), lambda *a: (a[2][a[0]], 0, 0))],
            out_specs=(pl.BlockSpec(memory_space=pl.ANY), pl.BlockSpec(memory_space=pl.ANY)),
            scratch_shapes=[pltpu.VMEM((MOE_ROWS, d), F32),
                            pltpu.VMEM((MOE_ROWS * SUBLANES, LANES), F32),
                            pltpu.VMEM((MOE_ROWS * SUBLANES, LANES), F32),
                            pltpu.VMEM((d, 2 * tf), BF16), pltpu.VMEM((tf, d), BF16),
                            pltpu.SemaphoreType.DMA((2,))]),
        compiler_params=_cparams(("arbitrary", "arbitrary")),
        name="moe_experts",
    )(blk_e, blk_valid, n_act, xs, w_up, w_up, b_up3, b_up3, w_down, b_dn3)


def _combine_kernel(lpos_ref, segc_ref, segl_ref, segg_ref, ya_ref, yb_ref, x1_ref, gate_ref, g2_ref, fg_ref,
                    yp_ref, yo_ref, bufa, bufb, outa, outb, sem, *, n_tok, n_exp, n_p):
    i = pl.program_id(0)
    n_steps = pl.num_programs(0)
    tm = x1_ref.shape[0]
    base = i * tm
    halves = ((ya_ref, bufa, outa, 0), (yb_ref, bufb, outb, 1))

    def fetch(tile, y_ref, buf, which):
        def go(e, carry):
            s = tile * n_exp + e
            src0 = segg_ref[s]
            dst0 = segl_ref[s]
            _segment_copies(segc_ref[s], lambda off, n: pltpu.make_async_copy(
                y_ref.at[_tile_rows(src0 + off, n), :], buf.at[_tile_rows(dst0 + off, n), :], sem.at[which]), "start")
            return carry
        lax.fori_loop(0, n_exp, go, 0)

    def mix(buf, out):
        def go(t, carry):
            acc = None
            for j in range(TOP_K):
                src = pl.ds(pl.multiple_of(lpos_ref[(base + t) * TOP_K + j], SUBLANES), SUBLANES)
                term = gate_ref[j, pl.ds(t, 1), :] * buf[src, :]
                acc = term if acc is None else acc + term
            out[_tile_rows(t, 1), :] = acc
            return carry
        lax.fori_loop(0, tm, go, 0, unroll=ROW_UNROLL)

    @pl.when(i == 0)
    def _():
        for y_ref, buf, _, which in halves:
            fetch(0, y_ref, buf, which)

    for y_ref, buf, out, which in halves:
        pltpu.make_async_copy(y_ref.at[pl.ds(0, buf.shape[0]), :], buf, sem.at[which]).wait()
        mix(buf, out)

        @pl.when(i + 1 < n_steps)
        def _():
            fetch(i + 1, y_ref, buf, which)

    def finish(out_ref):
        for c in range(tm // MOD_CHUNK):
            rows = slice(c * MOD_CHUNK, (c + 1) * MOD_CHUNK)
            moe = jnp.concatenate([outa[_slab(c * MOD_CHUNK, MOD_CHUNK, k), :] for k in range(SUBLANES)] +
                                  [outb[_slab(c * MOD_CHUNK, MOD_CHUNK, k), :] for k in range(SUBLANES)], axis=1)
            xo = x1_ref[rows, :] + g2_ref[c:c + 1, :] * moe
            out_ref[rows, :] = xo * lax.rsqrt(jnp.mean(xo * xo, axis=-1, keepdims=True) + EPS) * fg_ref[...]

    @pl.when(i < n_p)
    def _():
        finish(yp_ref)

    @pl.when(i >= n_p)
    def _():
        finish(yo_ref)


def _combine(lpos_flat, seg_cnt, seg_l, seg_g, ys_a, ys_b, x1, gate_x, g2, final_gain, tp, n_exp):
    t, d = x1.shape
    lanes = LANES
    ts = t - tp
    n_p = tp // TOK_TILE
    n_s = ts // TOK_TILE
    cpt = TOK_TILE // MOD_CHUNK
    kern = functools.partial(_combine_kernel, n_tok=t, n_exp=n_exp, n_p=n_p)
    tok = lambda i, *_: (i, 0)
    return pl.pallas_call(
        kern,
        out_shape=(jax.ShapeDtypeStruct((tp, d), F32), jax.ShapeDtypeStruct((ts, d), F32)),
        grid_spec=pltpu.PrefetchScalarGridSpec(
            num_scalar_prefetch=4,
            grid=(n_p + n_s,),
            in_specs=[pl.BlockSpec(memory_space=pl.ANY),
                      pl.BlockSpec(memory_space=pl.ANY),
                      pl.BlockSpec((TOK_TILE, d), tok),
                      pl.BlockSpec((TOP_K, TOK_TILE, lanes), lambda i, *_: (0, i, 0)),
                      _mod_spec(cpt, d, MOD_GATE2),
                      pl.BlockSpec((1, d), lambda i, *_: (0, 0))],
            out_specs=(pl.BlockSpec((TOK_TILE, d), lambda i, *_: (jnp.minimum(i, n_p - 1), 0)),
                       pl.BlockSpec((TOK_TILE, d), lambda i, *_: (jnp.maximum(i - n_p, 0), 0))),
            scratch_shapes=[pltpu.VMEM((TOK_TILE * TOP_K * SUBLANES, lanes), F32),
                            pltpu.VMEM((TOK_TILE * TOP_K * SUBLANES, lanes), F32),
                            pltpu.VMEM((TOK_TILE * SUBLANES, lanes), F32),
                            pltpu.VMEM((TOK_TILE * SUBLANES, lanes), F32),
                            pltpu.SemaphoreType.DMA((2,))]),
        compiler_params=_cparams(("arbitrary",)),
        name="combine",
    )(lpos_flat, seg_cnt, seg_l, seg_g, ys_a, ys_b, x1, gate_x, g2, final_gain)


def _state_ext(c_state, n_state):
    b, h, dh, _ = c_state.shape
    ct = jnp.swapaxes(c_state, -1, -2)
    pad = jnp.zeros((b, h, dh, LANES - 1), F32)
    return jnp.concatenate([ct, n_state[..., None], pad], axis=-1)


def _state_split(cx):
    dh = cx.shape[2]
    return jnp.swapaxes(cx[..., 0:dh], -1, -2), cx[..., dh]


def kernel(x_prompt, x_sample, state_pool, state_mlstm_C, state_mlstm_n, state_mlstm_m, c_prompt, c_sample,
           w_ada, b_ada, w_in, b_gate, pool_w, pool_scale, mlstm_gain, w_out, w_router, b_router,
           w_up, b_up, w_down, b_down, final_gain):
    depth = w_ada.shape[0]
    assert depth == 1, "single-layer trunk"
    bp, sp, d = x_prompt.shape
    bs, ss, _ = x_sample.shape
    tp, ts = bp * sp, bs * ss
    t = tp + ts
    pw = state_pool.shape[-1]
    heads, dh = state_mlstm_C.shape[2], state_mlstm_C.shape[3]
    mw = heads * dh
    n_exp = w_router.shape[-1]
    pad_rows = state_pool.shape[2]
    assert tp % TOK_TILE == 0 and ts % TOK_TILE == 0 and sp % TOK_TILE == 0
    assert ss % MOD_CHUNK == 0 and 2 * heads <= SUBLANES and pad_rows < POOL_HALO

    xp = x_prompt.reshape(tp, d)
    xs = x_sample.reshape(ts, d)

    n_c = bp + bs
    c_rows = -(-n_c // SUBLANES) * SUBLANES
    c_all = jnp.concatenate([c_prompt, c_sample, jnp.zeros((c_rows - n_c, d), F32)], axis=0)
    mod = _adaln(c_all, w_ada[0], b_ada)
    mod_c = jnp.concatenate([
        jnp.broadcast_to(mod[0:bp, None, :], (bp, sp // MOD_CHUNK, mod.shape[1])).reshape(tp // MOD_CHUNK, -1),
        jnp.broadcast_to(mod[bp:n_c, None, :], (bs, ss // MOD_CHUNK, mod.shape[1])).reshape(ts // MOD_CHUNK, -1)])
    sh1 = sc1 = g1 = sh2 = sc2 = g2 = mod_c

    w_in0 = w_in[0]
    o0 = pw
    w_main = jnp.concatenate([w_in0[:, 0:pw], w_in0[:, o0:o0 + mw], w_in0[:, o0 + 2 * mw:o0 + 4 * mw]],
                             axis=1).astype(BF16)
    w_kt = (w_in0[:, o0 + mw:o0 + 2 * mw] * (dh ** -0.5)).T.astype(BF16)
    w_g = w_in0[:, o0 + 4 * mw:]
    ng = w_g.shape[1]
    w_gc = jnp.pad(w_g, ((0, 0), (0, LANES - ng))).astype(BF16)
    w_gt = jnp.pad(w_g.T, ((0, SUBLANES - ng), (0, 0))).astype(BF16)
    b_gc = jnp.pad(b_gate[0], (0, LANES - ng)).reshape(1, LANES)
    b_gr = jnp.pad(b_gate[0], (0, SUBLANES - ng)).reshape(SUBLANES, 1)
    u, q, kt, v, og, gc, gr = _inproj(xp, xs, sc1, sh1, w_main, w_kt, w_gc, w_gt, b_gc, b_gr, pw, mw)

    pool_wb = pool_w[0].astype(BF16)
    zeros_p = jnp.zeros((bp, POOL_HALO, pw), F32)
    st_s = jnp.concatenate([jnp.zeros((bs, POOL_HALO - pad_rows, pw), F32), state_pool[0]], axis=1)
    pool_p = _pool(u, zeros_p, pool_wb, pool_scale, batch=bp, seq=sp, row0=0, tm=TOK_TILE, pos0=0)
    pool_s = _pool(u, st_s, pool_wb, pool_scale, batch=bs, seq=ss, row0=tp, tm=ss, pos0=PAST_LEN)
    new_pool_p = u[0:tp].reshape(bp, sp, pw)[:, sp - pad_rows:]
    new_pool_s = u[tp:].reshape(bs, ss, pw)[:, ss - pad_rows:]

    gain = mlstm_gain[0].reshape(1, mw)
    c0_p = jnp.zeros((bp, heads, dh, dh + LANES), F32)
    m0_p = jnp.zeros((bp, heads, 1, 1), F32)
    chunk_p = MLSTM_CHUNK if sp % MLSTM_CHUNK == 0 else MOD_CHUNK
    ml_p, cx_p, m_p = _mlstm(q, kt, v, og, gc, gr, gain, c0_p, m0_p, batch=bp, seq=sp, chunk=chunk_p,
                             row0=0, kt_per_seq=False)
    c0_s = _state_ext(state_mlstm_C[0], state_mlstm_n[0])
    m0_s = state_mlstm_m[0].reshape(bs, heads, 1, 1)
    kt_s = kt[:, tp:].reshape(mw, bs, ss).transpose(1, 0, 2)
    gr_s = gr[:, tp:].reshape(gr.shape[0], bs, ss).transpose(1, 0, 2)
    ml_s, cx_s, m_s = _mlstm(q, kt_s, v, og, gc, gr_s, gain, c0_s, m0_s, batch=bs, seq=ss, chunk=ss,
                             row0=tp, kt_per_seq=True)
    new_c_p, new_n_p = _state_split(cx_p)
    new_c_s, new_n_s = _state_split(cx_s)

    w_rt = w_router[0].T.astype(BF16)
    b_r = b_router[0].reshape(n_exp, 1)
    x1, h_packed, idx_t, gate_t = _outproj(pool_p, pool_s, ml_p, ml_s, xp, xs, g1, sc2, sh2,
                                           w_out[0].astype(BF16), w_rt, b_r)

    lpos_t, cnt = _ranks(idx_t, n_exp)
    lpos_flat = (lpos_t.T * SUBLANES).reshape(-1)
    c_te = cnt[:, ::LANES].T.astype(I32)
    counts = jnp.sum(c_te, axis=0)
    padded = (counts + MOE_ROWS - 1) // MOE_ROWS * MOE_ROWS
    pend = jnp.cumsum(padded)
    pstart = pend - padded
    seg_cnt = c_te.reshape(-1)
    seg_l = (jnp.cumsum(c_te, axis=1) - c_te).reshape(-1).astype(I32)
    seg_g = (pstart[None, :] + jnp.cumsum(c_te, axis=0) - c_te).reshape(-1).astype(I32)
    n_assign = t * TOP_K
    nb = -(-(n_assign + n_exp * (MOE_ROWS - 1)) // MOE_ROWS)
    n_act = pend[-1] // MOE_ROWS
    blk_ids = jnp.arange(nb, dtype=I32)
    last = jnp.minimum(blk_ids, n_act - 1)
    blk_e = jnp.minimum(jnp.searchsorted(pend, last * MOE_ROWS, side='right'), n_exp - 1).astype(I32)
    blk_valid = jnp.where(blk_ids < n_act,
                          jnp.clip(counts[blk_e] - (last * MOE_ROWS - pstart[blk_e]), 0, MOE_ROWS), 0).astype(I32)
    sub_end = (counts + MOE_SUB - 1) // MOE_SUB * MOE_SUB
    tail = pend[-1:]
    pad_row = jnp.concatenate([pstart + counts, tail]).astype(I32)
    pad_len = jnp.concatenate([sub_end - counts, jnp.zeros((1,), I32)]).astype(I32)
    fill_row = jnp.concatenate([pstart + sub_end, tail]).astype(I32)
    fill_n = jnp.concatenate([(padded - sub_end) // MOE_SUB, (nb * MOE_ROWS - tail) // MOE_SUB]).astype(I32)
    xs_grouped = _dispatch(lpos_flat, seg_cnt, seg_l, seg_g, pad_row, pad_len, fill_row, fill_n, h_packed,
                           nb * MOE_ROWS, n_exp)
    ys_a, ys_b = _moe(blk_e, blk_valid, n_act.reshape(1).astype(I32), xs_grouped, w_up, b_up[0], w_down, b_down[0])

    gate_x = jnp.broadcast_to(gate_t[:, :, None], (TOP_K, t, LANES))
    y_p, y_s = _combine(lpos_flat, seg_cnt, seg_l, seg_g, ys_a, ys_b, x1, gate_x, g2, final_gain.reshape(1, d),
                        tp, n_exp)

    return (y_p.reshape(bp, sp, d), y_s.reshape(bs, ss, d),
            new_pool_p[None], new_c_p[None], new_n_p[None], m_p.reshape(1, bp, heads),
            new_pool_s[None], new_c_s[None], new_n_s[None], m_s.reshape(1, bs, heads))
```

```python
import functools

import jax
import jax.numpy as jnp
from jax import lax
from jax.experimental import pallas as pl
from jax.experimental.pallas import tpu as pltpu

F32 = jnp.float32
BF16 = jnp.bfloat16
I32 = jnp.int32
U32 = jnp.uint32

EPS = 1e-6
POOL_WINDOWS = (2, 4, 8, 16)
TOP_K = 4
SWIGLU_LIMIT = 7.0
SWIGLU_ALPHA = 1.702
PAST_LEN = 2048

LANES = 128
SUBLANES = 8
MOD_CHUNK = 64
TOK_TILE = 512
POOL_HALO = 16
MLSTM_CHUNK = 256
ADA_TILE = 1024
MOE_ROWS = 768
MOE_SUB = 256
MOE_FF_TILE = 512
SEG_SHIFT = 4
SEG_CHUNK = 1 << SEG_SHIFT
ROW_UNROLL = 16
VMEM_LIMIT = 56 * 1024 * 1024


def _cparams(sem, vmem=VMEM_LIMIT):
    return pltpu.CompilerParams(dimension_semantics=sem, vmem_limit_bytes=vmem)


MOD_SHIFT1, MOD_SCALE1, MOD_GATE1, MOD_SHIFT2, MOD_SCALE2, MOD_GATE2 = range(6)


def _mod_spec(rows, d, which):
    return pl.BlockSpec((rows, d), lambda i, *_: (i, which))


def _resident(shape, index_map):
    return pl.BlockSpec(shape, index_map, pipeline_mode=pl.Buffered(1))


def _dot(a, b):
    return jnp.dot(a, b, preferred_element_type=F32)


def _dot_nt(a, b):
    return lax.dot_general(a, b, (((1,), (1,)), ((), ())), preferred_element_type=F32)


def _split3(x):
    hi = x.astype(BF16)
    r1 = x - hi.astype(F32)
    mid = r1.astype(BF16)
    lo = (r1 - mid.astype(F32)).astype(BF16)
    return hi, mid, lo


def _log_sigmoid(x):
    return jnp.minimum(x, 0.0) - jnp.log1p(jnp.exp(-jnp.abs(x)))


def _adaln_kernel(c_ref, w_ref, b_ref, o_ref):
    c = c_ref[...]
    s = (c * jax.nn.sigmoid(c)).astype(BF16)
    o_ref[...] = _dot(s, w_ref[...].astype(BF16)) + b_ref[...]


def _adaln(c_all, w_ada, b_ada):
    rows, d = c_all.shape
    n = w_ada.shape[1]
    return pl.pallas_call(
        _adaln_kernel,
        out_shape=jax.ShapeDtypeStruct((rows, n), F32),
        grid=(n // ADA_TILE,),
        in_specs=[pl.BlockSpec((rows, d), lambda j: (0, 0)),
                  pl.BlockSpec((d, ADA_TILE), lambda j: (0, j)),
                  pl.BlockSpec((1, ADA_TILE), lambda j: (0, j))],
        out_specs=pl.BlockSpec((rows, ADA_TILE), lambda j: (0, j)),
        compiler_params=_cparams(("parallel",)),
        name="adaln",
    )(c_all, w_ada, b_ada)


def _modulated_norm(x, sc_ref, sh_ref, h_scr):
    xn = x * lax.rsqrt(jnp.mean(x * x, axis=-1, keepdims=True) + EPS)
    for c in range(x.shape[0] // MOD_CHUNK):
        rows = slice(c * MOD_CHUNK, (c + 1) * MOD_CHUNK)
        h_scr[rows, :] = (xn[rows, :] * (1.0 + sc_ref[c:c + 1, :]) + sh_ref[c:c + 1, :]).astype(h_scr.dtype)


def _inproj_kernel(xp_ref, xs_ref, sc_ref, sh_ref, w_ref, wkt_ref, wg_ref, wgt_ref, bgc_ref, bgr_ref,
                   u_ref, q_ref, kt_ref, v_ref, og_ref, gc_ref, gr_ref, h_scr, *, n_p, pw, mw):
    i = pl.program_id(0)

    def body(x_ref):
        _modulated_norm(x_ref[...], sc_ref, sh_ref, h_scr)
        h = h_scr[...]
        u_ref[...] = _dot(h, w_ref[:, 0:pw])
        q_ref[...] = _dot(h, w_ref[:, pw:pw + mw]).astype(BF16)
        v_ref[...] = _dot(h, w_ref[:, pw + mw:pw + 2 * mw]).astype(BF16)
        og_ref[...] = _dot(h, w_ref[:, pw + 2 * mw:pw + 3 * mw])
        kt_ref[...] = _dot_nt(wkt_ref[...], h).astype(BF16)
        gc_ref[...] = _dot(h, wg_ref[...]) + bgc_ref[...]
        gr_ref[...] = _dot_nt(wgt_ref[...], h) + bgr_ref[...]

    pl.when(i < n_p)(functools.partial(body, xp_ref))
    pl.when(i >= n_p)(functools.partial(body, xs_ref))


def _inproj(xp, xs, sc1, sh1, w_main, w_kt, w_gc, w_gt, b_gc, b_gr, pw, mw):
    tp, d = xp.shape
    ts = xs.shape[0]
    t = tp + ts
    n_p = tp // TOK_TILE
    n_s = ts // TOK_TILE
    cpt = TOK_TILE // MOD_CHUNK
    ng = w_gt.shape[0]
    kern = functools.partial(_inproj_kernel, n_p=n_p, pw=pw, mw=mw)
    tok = lambda i: (i, 0)
    return pl.pallas_call(
        kern,
        out_shape=(jax.ShapeDtypeStruct((t, pw), F32),
                   jax.ShapeDtypeStruct((t, mw), BF16),
                   jax.ShapeDtypeStruct((mw, t), BF16),
                   jax.ShapeDtypeStruct((t, mw), BF16),
                   jax.ShapeDtypeStruct((t, mw), F32),
                   jax.ShapeDtypeStruct((t, LANES), F32),
                   jax.ShapeDtypeStruct((ng, t), F32)),
        grid=(n_p + n_s,),
        in_specs=[pl.BlockSpec((TOK_TILE, d), lambda i: (jnp.minimum(i, n_p - 1), 0)),
                  pl.BlockSpec((TOK_TILE, d), lambda i: (jnp.maximum(i - n_p, 0), 0)),
                  _mod_spec(cpt, d, MOD_SCALE1),
                  _mod_spec(cpt, d, MOD_SHIFT1),
                  _resident(w_main.shape, lambda i: (0, 0)),
                  _resident(w_kt.shape, lambda i: (0, 0)),
                  _resident(w_gc.shape, lambda i: (0, 0)),
                  _resident(w_gt.shape, lambda i: (0, 0)),
                  _resident(b_gc.shape, lambda i: (0, 0)),
                  _resident(b_gr.shape, lambda i: (0, 0))],
        out_specs=(pl.BlockSpec((TOK_TILE, pw), tok),
                   pl.BlockSpec((TOK_TILE, mw), tok),
                   pl.BlockSpec((mw, TOK_TILE), lambda i: (0, i)),
                   pl.BlockSpec((TOK_TILE, mw), tok),
                   pl.BlockSpec((TOK_TILE, mw), tok),
                   pl.BlockSpec((TOK_TILE, LANES), tok),
                   pl.BlockSpec((ng, TOK_TILE), lambda i: (0, i))),
        scratch_shapes=[pltpu.VMEM((TOK_TILE, d), BF16)],
        compiler_params=_cparams(("parallel",)),
        name="inproj",
    )(xp, xs, sc1, sh1, w_main, w_kt, w_gc, w_gt, b_gc, b_gr)


def _pool_kernel(u_ref, st_ref, pw_ref, ps_ref, o_ref, xp_scr, *, tm, pos0, group):
    j = pl.program_id(1)

    @pl.when(j == 0)
    def _():
        xp_scr[0:POOL_HALO, :] = st_ref[...]

    @pl.when(j > 0)
    def _():
        xp_scr[0:POOL_HALO, :] = xp_scr[tm:tm + POOL_HALO, :]

    xp_scr[POOL_HALO:POOL_HALO + tm, :] = u_ref[...]
    pos = pos0 + j * tm + lax.broadcasted_iota(I32, (tm, group), 0)
    for g, w in enumerate(POOL_WINDOWS):
        cs = slice(g * group, (g + 1) * group)
        x = xp_scr[POOL_HALO:POOL_HALO + tm, cs]
        acc = x
        for s in range(1, w):
            acc = acc + xp_scr[POOL_HALO - s:POOL_HALO - s + tm, cs]
        cnt = jnp.minimum(pos + 1, w).astype(F32)
        pooled = acc / cnt - x
        mixed = _dot(pooled.astype(BF16), pw_ref[g]) * ps_ref[:, cs]
        o_ref[:, cs] = mixed.astype(o_ref.dtype)


def _pool(u, state16, pool_w, pool_scale, *, batch, seq, row0, tm, pos0):
    c = u.shape[1]
    group = c // len(POOL_WINDOWS)
    nt = seq // tm
    blk0 = row0 // tm
    kern = functools.partial(_pool_kernel, tm=tm, pos0=pos0, group=group)
    return pl.pallas_call(
        kern,
        out_shape=jax.ShapeDtypeStruct((batch * seq, c), BF16),
        grid=(batch, nt),
        in_specs=[pl.BlockSpec((tm, c), lambda b, j: (blk0 + b * nt + j, 0)),
                  pl.BlockSpec((None, POOL_HALO, c), lambda b, j: (b, 0, 0)),
                  _resident(pool_w.shape, lambda b, j: (0, 0, 0)),
                  _resident(pool_scale.shape, lambda b, j: (0, 0))],
        out_specs=pl.BlockSpec((tm, c), lambda b, j: (b * nt + j, 0)),
        scratch_shapes=[pltpu.VMEM((POOL_HALO + tm, c), F32)],
        compiler_params=_cparams(("parallel", "arbitrary")),
        name="pool",
    )(u, state16, pool_w, pool_scale)


def _mlstm_kernel(q_ref, kt_ref, v_ref, og_ref, gc_ref, gr_ref, gain_ref, c0_ref, m0_ref,
                  o_ref, c_ref, m_ref, *, chunk, heads, dh):
    ci = pl.program_id(1)
    L = chunk

    @pl.when(ci == 0)
    def _():
        c_ref[...] = c0_ref[...]
        m_ref[...] = m0_ref[...]

    gc = gc_ref[...]
    gr = gr_ref[...]
    lf_c = _log_sigmoid(gc)
    lf_r = _log_sigmoid(gr)
    row_i = lax.broadcasted_iota(I32, (L, L), 0)
    col_i = lax.broadcasted_iota(I32, (L, L), 1)
    causal = col_i <= row_i
    tri = causal.astype(BF16)
    tri_t = (row_i <= col_i).astype(BF16)
    f_c = sum(_dot(tri, p) for p in _split3(lf_c))
    f_r = sum(_dot(p, tri_t) for p in _split3(lf_r))
    one_col = (lax.broadcasted_iota(I32, (L, LANES), 1) == 0).astype(BF16)
    neg_inf = jnp.float32(-jnp.inf)

    for h in range(heads):
        hs = slice(h * dh, (h + 1) * dh)
        fc = f_c[:, heads + h:heads + h + 1]
        a_r = gr[h:h + 1, :] - f_r[heads + h:heads + h + 1, :]
        m_prev = m_ref[h]
        cm = jnp.max(jnp.where(causal, a_r, neg_inf), axis=1, keepdims=True)
        m_t = fc + jnp.maximum(m_prev, cm)
        dm = jnp.exp(jnp.where(causal, (fc - m_t) + a_r, neg_inf))
        qh = q_ref[:, hs]
        kth = kt_ref[hs, :]
        vh = v_ref[:, hs]
        sc = _dot(qh, kth) * dm
        inter = jnp.exp(fc + m_prev - m_t)
        cx = c_ref[h]
        g = _dot(qh, cx.astype(BF16))
        num = _dot(sc.astype(BF16), vh) + inter * g[:, 0:dh]
        den = jnp.sum(sc, axis=1, keepdims=True) + inter * g[:, dh:dh + 1]
        hh = num * (1.0 / jnp.maximum(jnp.abs(den), jnp.exp(-m_t)))
        hn = hh * lax.rsqrt(jnp.mean(hh * hh, axis=1, keepdims=True) + EPS) * gain_ref[:, hs]
        o_ref[:, hs] = (jax.nn.sigmoid(og_ref[:, hs]) * hn).astype(o_ref.dtype)

        m_last = m_t[L - 1:L, :]
        f_last = fc[L - 1:L, :]
        w_r = jnp.exp(f_last + a_r - m_last)
        decay = jnp.exp(f_last + m_prev - m_last)
        kw = (kth.astype(F32) * w_r).astype(BF16)
        v_ext = jnp.concatenate([vh, one_col], axis=1)
        c_ref[h] = decay * cx + _dot(kw, v_ext)
        m_ref[h] = m_last


def _mlstm(q, kt, v, og, gc, gr, gain, c0, m0, *, batch, seq, chunk, row0, kt_per_seq):
    mw = q.shape[1]
    heads, dh = c0.shape[1], c0.shape[2]
    nc = seq // chunk
    blk0 = row0 // chunk
    ng = gr.shape[-2]
    tok = lambda b, c: (blk0 + b * nc + c, 0)
    if kt_per_seq:
        kt_spec = pl.BlockSpec((None, mw, chunk), lambda b, c: (b, 0, c))
        gr_spec = pl.BlockSpec((None, ng, chunk), lambda b, c: (b, 0, c))
    else:
        kt_spec = pl.BlockSpec((mw, chunk), lambda b, c: (0, blk0 + b * nc + c))
        gr_spec = pl.BlockSpec((ng, chunk), lambda b, c: (0, blk0 + b * nc + c))
    kern = functools.partial(_mlstm_kernel, chunk=chunk, heads=heads, dh=dh)
    st_spec = pl.BlockSpec((None, heads, dh, dh + LANES), lambda b, c: (b, 0, 0, 0))
    m_spec = pl.BlockSpec((None, heads, 1, 1), lambda b, c: (b, 0, 0, 0))
    return pl.pallas_call(
        kern,
        out_shape=(jax.ShapeDtypeStruct((batch * seq, mw), BF16),
                   jax.ShapeDtypeStruct(c0.shape, F32),
                   jax.ShapeDtypeStruct(m0.shape, F32)),
        grid=(batch, nc),
        in_specs=[pl.BlockSpec((chunk, mw), tok),
                  kt_spec,
                  pl.BlockSpec((chunk, mw), tok),
                  pl.BlockSpec((chunk, mw), tok),
                  pl.BlockSpec((chunk, LANES), tok),
                  gr_spec,
                  _resident(gain.shape, lambda b, c: (0, 0)),
                  st_spec, m_spec],
        out_specs=(pl.BlockSpec((chunk, mw), lambda b, c: (b * nc + c, 0)), st_spec, m_spec),
        compiler_params=_cparams(("parallel", "arbitrary")),
        name="mlstm",
    )(q, kt, v, og, gc, gr, gain, c0, m0)


def _outproj_body(p_ref, m_ref, x_ref, g1_ref, sc_ref, sh_ref, wo_ref, wrt_ref, br_ref,
                  x1_ref, hp_ref, idx_ref, gate_ref, h_scr, *, pw, n_exp):
    mix = _dot(p_ref[...], wo_ref[0:pw, :]) + _dot(m_ref[...], wo_ref[pw:, :])
    x = x_ref[...]
    tm, d = x.shape
    for c in range(tm // MOD_CHUNK):
        rows = slice(c * MOD_CHUNK, (c + 1) * MOD_CHUNK)
        x1_ref[rows, :] = x[rows, :] + g1_ref[c:c + 1, :] * mix[rows, :]
    _modulated_norm(x1_ref[...], sc_ref, sh_ref, h_scr)
    hb = h_scr[...]
    hi = lax.bitcast_convert_type(hb[:, 0:d // 2].astype(F32), U32)
    lo = lax.bitcast_convert_type(hb[:, d // 2:].astype(F32), U32)
    words = (hi & jnp.uint32(0xFFFF0000)) | (lo >> 16)
    n_slab = words.shape[1] // LANES
    for k in range(n_slab):
        hp_ref[pl.ds(k, tm, stride=n_slab), :] = words[:, k * LANES:(k + 1) * LANES]

    logits = _dot_nt(wrt_ref[...], hb) + br_ref[...]
    e_iota = lax.broadcasted_iota(I32, logits.shape, 0)
    vals = []
    for j in range(TOP_K):
        mx = jnp.max(logits, axis=0, keepdims=True)
        ix = jnp.min(jnp.where(logits == mx, e_iota, n_exp), axis=0, keepdims=True)
        idx_ref[j:j + 1, :] = ix
        vals.append(mx)
        logits = jnp.where(e_iota == ix, -jnp.inf, logits)
    ex = [jnp.exp(v - vals[0]) for v in vals]
    tot = ex[0]
    for e in ex[1:]:
        tot = tot + e
    inv = 1.0 / tot
    for j in range(TOP_K):
        gate_ref[j:j + 1, :] = ex[j] * inv


def _outproj_kernel(pp_ref, ps_ref, mp_ref, ms_ref, xp_ref, xs_ref, g1_ref, sc_ref, sh_ref,
                    wo_ref, wrt_ref, br_ref, x1_ref, hp_ref, idx_ref, gate_ref, h_scr, *, n_p, pw, n_exp):
    i = pl.program_id(0)
    rest = (g1_ref, sc_ref, sh_ref, wo_ref, wrt_ref, br_ref, x1_ref, hp_ref, idx_ref, gate_ref, h_scr)

    @pl.when(i < n_p)
    def _():
        _outproj_body(pp_ref, mp_ref, xp_ref, *rest, pw=pw, n_exp=n_exp)

    @pl.when(i >= n_p)
    def _():
        _outproj_body(ps_ref, ms_ref, xs_ref, *rest, pw=pw, n_exp=n_exp)


def _outproj(pool_p, pool_s, ml_p, ml_s, xp, xs, g1, sc2, sh2, w_out, w_rt, b_r):
    tp, d = xp.shape
    ts = xs.shape[0]
    t = tp + ts
    pw = pool_p.shape[1]
    mw = ml_p.shape[1]
    n_exp = w_rt.shape[0]
    n_p = tp // TOK_TILE
    n_s = ts // TOK_TILE
    cpt = TOK_TILE // MOD_CHUNK
    kern = functools.partial(_outproj_kernel, n_p=n_p, pw=pw, n_exp=n_exp)
    tok = lambda i: (i, 0)
    pidx = lambda i: (jnp.minimum(i, n_p - 1), 0)
    sidx = lambda i: (jnp.maximum(i - n_p, 0), 0)
    return pl.pallas_call(
        kern,
        out_shape=(jax.ShapeDtypeStruct((t, d), F32),
                   jax.ShapeDtypeStruct((t * (d // 2 // LANES), LANES), U32),
                   jax.ShapeDtypeStruct((TOP_K, t), I32),
                   jax.ShapeDtypeStruct((TOP_K, t), F32)),
        grid=(n_p + n_s,),
        in_specs=[pl.BlockSpec((TOK_TILE, pw), pidx), pl.BlockSpec((TOK_TILE, pw), sidx),
                  pl.BlockSpec((TOK_TILE, mw), pidx), pl.BlockSpec((TOK_TILE, mw), sidx),
                  pl.BlockSpec((TOK_TILE, d), pidx), pl.BlockSpec((TOK_TILE, d), sidx),
                  _mod_spec(cpt, d, MOD_GATE1), _mod_spec(cpt, d, MOD_SCALE2), _mod_spec(cpt, d, MOD_SHIFT2),
                  _resident(w_out.shape, lambda i: (0, 0)),
                  _resident(w_rt.shape, lambda i: (0, 0)),
                  _resident(b_r.shape, lambda i: (0, 0))],
        out_specs=(pl.BlockSpec((TOK_TILE, d), tok),
                   pl.BlockSpec((TOK_TILE * (d // 2 // LANES), LANES), tok),
                   pl.BlockSpec((TOP_K, TOK_TILE), lambda i: (0, i)),
                   pl.BlockSpec((TOP_K, TOK_TILE), lambda i: (0, i))),
        scratch_shapes=[pltpu.VMEM((TOK_TILE, d), BF16)],
        compiler_params=_cparams(("parallel",)),
        name="outproj",
    )(pool_p, pool_s, ml_p, ml_s, xp, xs, g1, sc2, sh2, w_out, w_rt, b_r)


def _rank_kernel(idx_ref, lpos_ref, cnt_ref, *, n_exp):
    tm = idx_ref.shape[1]
    e_iota = lax.broadcasted_iota(I32, (n_exp, tm), 0)
    hots = [e_iota == idx_ref[j:j + 1, :] for j in range(TOP_K)]
    cnt = hots[0].astype(F32)
    for hot in hots[1:]:
        cnt = cnt + hot.astype(F32)
    cnt_b = cnt.astype(BF16)
    r = lax.broadcasted_iota(I32, (tm, tm), 0)
    c = lax.broadcasted_iota(I32, (tm, tm), 1)
    earlier_tok = _dot(cnt_b, (r < c).astype(BF16))
    er = lax.broadcasted_iota(I32, (n_exp, n_exp), 0)
    ec = lax.broadcasted_iota(I32, (n_exp, n_exp), 1)
    lower_exp = _dot((ec < er).astype(BF16), cnt_b)
    seg_start = jnp.sum(lower_exp, axis=1, keepdims=True)
    pos = earlier_tok + seg_start
    for j in range(TOP_K):
        lpos_ref[j:j + 1, :] = jnp.sum(jnp.where(hots[j], pos, 0.0), axis=0, keepdims=True).astype(I32)
    cnt_ref[...] = jnp.broadcast_to(jnp.sum(cnt, axis=1, keepdims=True), cnt_ref.shape)


def _ranks(idx_t, n_exp):
    k, t = idx_t.shape
    n_tiles = t // TOK_TILE
    return pl.pallas_call(
        functools.partial(_rank_kernel, n_exp=n_exp),
        out_shape=(jax.ShapeDtypeStruct((k, t), I32), jax.ShapeDtypeStruct((n_exp, n_tiles * LANES), F32)),
        grid=(n_tiles,),
        in_specs=[pl.BlockSpec((k, TOK_TILE), lambda i: (0, i))],
        out_specs=(pl.BlockSpec((k, TOK_TILE), lambda i: (0, i)),
                   pl.BlockSpec((n_exp, LANES), lambda i: (0, i))),
        compiler_params=_cparams(("parallel",)),
        name="ranks",
    )(idx_t)


def _tile_rows(first, n):
    start = first * SUBLANES
    if not isinstance(start, int):
        start = pl.multiple_of(start, SUBLANES)
    return pl.ds(start, n * SUBLANES)


def _slab(first, n, k):
    return pl.ds(first * SUBLANES + k, n, stride=SUBLANES)


def _segment_copies(count, make_copy, op):
    n_full = count >> SEG_SHIFT

    def full(k, carry):
        getattr(make_copy(k * SEG_CHUNK, SEG_CHUNK), op)()
        return carry

    lax.fori_loop(0, n_full, full, 0)
    rest = count - n_full * SEG_CHUNK
    base = n_full * SEG_CHUNK
    bit = SEG_CHUNK // 2
    while bit:
        off = base + (rest & -(2 * bit))

        def one(off=off, bit=bit):
            getattr(make_copy(off, bit), op)()

        pl.when((rest & bit) != 0)(one)
        bit //= 2


def _dispatch_kernel(lpos_ref, segc_ref, segl_ref, segg_ref, padrow_ref, padlen_ref, fillrow_ref, filln_ref,
                     h_ref, xs_ref, sorted_a, sorted_b, zero_scr, sem, *, n_steps, n_exp, n_regions):
    i = pl.program_id(0)
    tm = h_ref.shape[0] // SUBLANES
    base = i * tm
    slots = ((sorted_a, 0), (sorted_b, 1))

    def wait_sent(buf, which):
        pltpu.make_async_copy(buf, xs_ref.at[pl.ds(0, buf.shape[0]), :], sem.at[which]).wait()

    def run(buf, which):
        @pl.when(i >= 2)
        def _():
            wait_sent(buf, which)

        def place(t, carry):
            row = h_ref[_tile_rows(t, 1), :]
            for j in range(TOP_K):
                dst = pl.multiple_of(lpos_ref[(base + t) * TOP_K + j], SUBLANES)
                buf[pl.ds(dst, SUBLANES), :] = row
            return carry

        lax.fori_loop(0, tm, place, 0, unroll=ROW_UNROLL)

        def send(e, carry):
            s = i * n_exp + e
            src0 = segl_ref[s]
            dst0 = segg_ref[s]
            _segment_copies(segc_ref[s], lambda off, n: pltpu.make_async_copy(
                buf.at[_tile_rows(src0 + off, n), :], xs_ref.at[_tile_rows(dst0 + off, n), :], sem.at[which]), "start")
            return carry

        lax.fori_loop(0, n_exp, send, 0)

    for buf, which in slots:
        pl.when(i % 2 == which)(functools.partial(run, buf, which))

    @pl.when(i == 0)
    def _():
        zero_scr[...] = jnp.zeros_like(zero_scr)

        def fill_copy(e, c):
            return pltpu.make_async_copy(zero_scr, xs_ref.at[_tile_rows(fillrow_ref[e] + c * MOE_SUB, MOE_SUB), :],
                                         sem.at[2])

        def region(op):
            def go(e, carry):
                dst0 = padrow_ref[e]
                _segment_copies(padlen_ref[e], lambda off, n: pltpu.make_async_copy(
                    zero_scr.at[_tile_rows(0, n), :], xs_ref.at[_tile_rows(dst0 + off, n), :], sem.at[2]), op)
                lax.fori_loop(0, filln_ref[e], lambda k, c: (getattr(fill_copy(e, k), op)(), c)[1], 0)
                return carry
            return go

        lax.fori_loop(0, n_regions, region("start"), 0)
        lax.fori_loop(0, n_regions, region("wait"), 0)

    @pl.when(i == n_steps - 1)
    def _():
        for buf, which in slots[:min(n_steps, 2)]:
            wait_sent(buf, which)


def _dispatch(lpos_flat, seg_cnt, seg_l, seg_g, pad_row, pad_len, fill_row, fill_n, h_tiles, rows, n_exp):
    t = h_tiles.shape[0] // SUBLANES
    kern = functools.partial(_dispatch_kernel, n_steps=t // TOK_TILE, n_exp=n_exp, n_regions=pad_row.shape[0])
    return pl.pallas_call(
        kern,
        out_shape=jax.ShapeDtypeStruct((rows * SUBLANES, LANES), U32),
        grid_spec=pltpu.PrefetchScalarGridSpec(
            num_scalar_prefetch=8,
            grid=(t // TOK_TILE,),
            in_specs=[pl.BlockSpec((TOK_TILE * SUBLANES, LANES), lambda i, *_: (i, 0))],
            out_specs=pl.BlockSpec(memory_space=pl.ANY),
            scratch_shapes=[pltpu.VMEM((TOK_TILE * TOP_K * SUBLANES, LANES), U32),
                            pltpu.VMEM((TOK_TILE * TOP_K * SUBLANES, LANES), U32),
                            pltpu.VMEM((MOE_SUB * SUBLANES, LANES), U32),
                            pltpu.SemaphoreType.DMA((3,))]),
        compiler_params=_cparams(("arbitrary",)),
        name="dispatch",
    )(lpos_flat, seg_cnt, seg_l, seg_g, pad_row, pad_len, fill_row, fill_n, h_tiles)


def _unpack_rows(x_ref, first, n):
    his, los = [], []
    for k in range(SUBLANES):
        words = x_ref[_slab(first, n, k), :]
        his.append(lax.bitcast_convert_type(words & jnp.uint32(0xFFFF0000), F32).astype(BF16))
        los.append(lax.bitcast_convert_type(words << 16, F32).astype(BF16))
    return jnp.concatenate(his + los, axis=1)


def _moe_kernel(be_ref, bv_ref, na_ref, x_ref, wg_ref, wl_ref, bg_ref, bl_ref, wd_ref, bd_ref,
                ya_ref, yb_ref, acc_scr, sta_scr, stb_scr, wup_scr, wd_scr, sem, *, tf):
    b = pl.program_id(0)
    f = pl.program_id(1)
    n_blk = pl.num_programs(0)
    last_f = pl.num_programs(1) - 1
    rows_total = x_ref.shape[0] // SUBLANES
    n_sub_max = rows_total // MOE_SUB
    half = SUBLANES * LANES
    active = b < na_ref[0]

    def writeback(blk):
        dst = _tile_rows(blk * rows_total, rows_total)
        return (pltpu.make_async_copy(sta_scr, ya_ref.at[dst, :], sem.at[0]),
                pltpu.make_async_copy(stb_scr, yb_ref.at[dst, :], sem.at[1]))

    @pl.when(jnp.logical_and(f == last_f, b > 0))
    def _():
        for cp in writeback(b - 1):
            cp.wait()

    @pl.when(jnp.logical_and(b == na_ref[0], f == last_f))
    def _():
        sta_scr[...] = jnp.zeros_like(sta_scr)
        stb_scr[...] = jnp.zeros_like(stb_scr)

    @pl.when(active)
    def _():
        @pl.when(f == 0)
        def _():
            acc_scr[...] = jnp.broadcast_to(bd_ref[...], acc_scr.shape)

        n_sub = (bv_ref[b] + MOE_SUB - 1) // MOE_SUB

        def stage(s, y):
            for k in range(SUBLANES):
                slab = _slab(s * MOE_SUB, MOE_SUB, k)
                sta_scr[slab, :] = y[:, k * LANES:(k + 1) * LANES]
                stb_scr[slab, :] = y[:, half + k * LANES:half + (k + 1) * LANES]

        def body(n_live, to_staging):
            wup_scr[:, 0:tf] = wg_ref[...].astype(BF16)
            wup_scr[:, tf:] = wl_ref[...].astype(BF16)
            wd_scr[...] = wd_ref[...].astype(BF16)
            for s in range(n_live):
                rows = slice(s * MOE_SUB, (s + 1) * MOE_SUB)
                x = _unpack_rows(x_ref, s * MOE_SUB, MOE_SUB)
                up = _dot(x, wup_scr[...])
                g = jnp.minimum(up[:, 0:tf] + bg_ref[...], SWIGLU_LIMIT)
                lin = jnp.clip(up[:, tf:] + bl_ref[...], -SWIGLU_LIMIT, SWIGLU_LIMIT)
                act = g * jax.nn.sigmoid(SWIGLU_ALPHA * g) * (lin + 1.0)
                y = acc_scr[rows, :] + _dot(act.astype(BF16), wd_scr[...])
                if to_staging:
                    stage(s, y)
                else:
                    acc_scr[rows, :] = y

        full = n_sub == n_sub_max
        for n_live in range(1, n_sub_max):
            pl.when(n_sub == n_live)(functools.partial(body, n_live, False))
        pl.when(jnp.logical_and(full, f != last_f))(functools.partial(body, n_sub_max, False))
        pl.when(jnp.logical_and(full, f == last_f))(functools.partial(body, n_sub_max, True))

        @pl.when(jnp.logical_and(jnp.logical_not(full), f == last_f))
        def _():
            for s in range(n_sub_max):
                stage(s, acc_scr[s * MOE_SUB:(s + 1) * MOE_SUB, :])

    @pl.when(f == last_f)
    def _():
        for cp in writeback(b):
            cp.start()

        @pl.when(b == n_blk - 1)
        def _():
            for cp in writeback(b):
                cp.wait()


def _moe(blk_e, blk_valid, n_act, xs, w_up, b_up, w_down, b_down):
    rows = xs.shape[0] // SUBLANES
    d = 2 * SUBLANES * LANES
    n_exp, _, two_f = w_up.shape[1:]
    assert w_up.shape[2] == d, "one (8, 128) tile of packed words per row"
    ff = two_f // 2
    tf = MOE_FF_TILE
    nf = ff // tf
    nb = rows // MOE_ROWS
    b_up3 = b_up.reshape(n_exp, 1, two_f)
    b_dn3 = b_down.reshape(n_exp, 1, d)

    def blk(b, f, be, bv, na):
        return jnp.minimum(b, na[0] - 1)

    def ftile(b, f, be, bv, na):
        return jnp.where(b < na[0], f, nf - 1)

    y_shape = jax.ShapeDtypeStruct((rows * SUBLANES, LANES), F32)
    return pl.pallas_call(
        functools.partial(_moe_kernel, tf=tf),
        out_shape=(y_shape, y_shape),
        grid_spec=pltpu.PrefetchScalarGridSpec(
            num_scalar_prefetch=3,
            grid=(nb, nf),
            in_specs=[pl.BlockSpec((MOE_ROWS * SUBLANES, LANES), lambda *a: (blk(*a), 0)),
                      pl.BlockSpec((None, None, d, tf), lambda *a: (0, a[2][a[0]], 0, ftile(*a))),
                      pl.BlockSpec((None, None, d, tf), lambda *a: (0, a[2][a[0]], 0, ftile(*a) + nf)),
                      pl.BlockSpec((None, 1, tf), lambda *a: (a[2][a[0]], 0, ftile(*a))),
                      pl.BlockSpec((None, 1, tf), lambda *a: (a[2][a[0]], 0, ftile(*a) + nf)),
                      pl.BlockSpec((None, None, tf, d), lambda *a: (0, a[2][a[0]], ftile(*a), 0)),
                      pl.BlockSpec((None, 1, d), lambda *a: (a[2][a[0]], 0, 0))],
            out_specs=(pl.BlockSpec(memory_space=pl.ANY), pl.BlockSpec(memory_space=pl.ANY)),
            scratch_shapes=[pltpu.VMEM((MOE_ROWS, d), F32),
                            pltpu.VMEM((MOE_ROWS * SUBLANES, LANES), F32),
                            pltpu.VMEM((MOE_ROWS * SUBLANES, LANES), F32),
                            pltpu.VMEM((d, 2 * tf), BF16), pltpu.VMEM((tf, d), BF16),
                            pltpu.SemaphoreType.DMA((2,))]),
        compiler_params=_cparams(("arbitrary", "arbitrary")),
        name="moe_experts",
    )(blk_e, blk_valid, n_act, xs, w_up, w_up, b_up3, b_up3, w_down, b_dn3)


def _combine_kernel(lpos_ref, segc_ref, segl_ref, segg_ref, ya_ref, yb_ref, x1_ref, gate_ref, g2_ref, fg_ref,
                    yp_ref, yo_ref, bufa, bufb, outa, outb, sem, *, n_tok, n_exp, n_p):
    i = pl.program_id(0)
    n_steps = pl.num_programs(0)
    tm = x1_ref.shape[0]
    base = i * tm
    halves = ((ya_ref, bufa, outa, 0), (yb_ref, bufb, outb, 1))

    def fetch(tile, y_ref, buf, which):
        def go(e, carry):
            s = tile * n_exp + e
            src0 = segg_ref[s]
            dst0 = segl_ref[s]
            _segment_copies(segc_ref[s], lambda off, n: pltpu.make_async_copy(
                y_ref.at[_tile_rows(src0 + off, n), :], buf.at[_tile_rows(dst0 + off, n), :], sem.at[which]), "start")
            return carry
        lax.fori_loop(0, n_exp, go, 0)

    def mix(buf, out):
        def go(t, carry):
            acc = None
            for j in range(TOP_K):
                src = pl.ds(pl.multiple_of(lpos_ref[(base + t) * TOP_K + j], SUBLANES), SUBLANES)
                term = gate_ref[j, pl.ds(t, 1), :] * buf[src, :]
                acc = term if acc is None else acc + term
            out[_tile_rows(t, 1), :] = acc
            return carry
        lax.fori_loop(0, tm, go, 0, unroll=ROW_UNROLL)

    @pl.when(i == 0)
    def _():
        for y_ref, buf, _, which in halves:
            fetch(0, y_ref, buf, which)

    for y_ref, buf, out, which in halves:
        pltpu.make_async_copy(y_ref.at[pl.ds(0, buf.shape[0]), :], buf, sem.at[which]).wait()
        mix(buf, out)

        @pl.when(i + 1 < n_steps)
        def _():
            fetch(i + 1, y_ref, buf, which)

    def finish(out_ref):
        for c in range(tm // MOD_CHUNK):
            rows = slice(c * MOD_CHUNK, (c + 1) * MOD_CHUNK)
            moe = jnp.concatenate([outa[_slab(c * MOD_CHUNK, MOD_CHUNK, k), :] for k in range(SUBLANES)] +
                                  [outb[_slab(c * MOD_CHUNK, MOD_CHUNK, k), :] for k in range(SUBLANES)], axis=1)
            xo = x1_ref[rows, :] + g2_ref[c:c + 1, :] * moe
            out_ref[rows, :] = xo * lax.rsqrt(jnp.mean(xo * xo, axis=-1, keepdims=True) + EPS) * fg_ref[...]

    @pl.when(i < n_p)
    def _():
        finish(yp_ref)

    @pl.when(i >= n_p)
    def _():
        finish(yo_ref)


def _combine(lpos_flat, seg_cnt, seg_l, seg_g, ys_a, ys_b, x1, gate_x, g2, final_gain, tp, n_exp):
    t, d = x1.shape
    lanes = LANES
    ts = t - tp
    n_p = tp // TOK_TILE
    n_s = ts // TOK_TILE
    cpt = TOK_TILE // MOD_CHUNK
    kern = functools.partial(_combine_kernel, n_tok=t, n_exp=n_exp, n_p=n_p)
    tok = lambda i, *_: (i, 0)
    return pl.pallas_call(
        kern,
        out_shape=(jax.ShapeDtypeStruct((tp, d), F32), jax.ShapeDtypeStruct((ts, d), F32)),
        grid_spec=pltpu.PrefetchScalarGridSpec(
            num_scalar_prefetch=4,
            grid=(n_p + n_s,),
            in_specs=[pl.BlockSpec(memory_space=pl.ANY),
                      pl.BlockSpec(memory_space=pl.ANY),
                      pl.BlockSpec((TOK_TILE, d), tok),
                      pl.BlockSpec((TOP_K, TOK_TILE, lanes), lambda i, *_: (0, i, 0)),
                      _mod_spec(cpt, d, MOD_GATE2),
                      pl.BlockSpec((1, d), lambda i, *_: (0, 0))],
            out_specs=(pl.BlockSpec((TOK_TILE, d), lambda i, *_: (jnp.minimum(i, n_p - 1), 0)),
                       pl.BlockSpec((TOK_TILE, d), lambda i, *_: (jnp.maximum(i - n_p, 0), 0))),
            scratch_shapes=[pltpu.VMEM((TOK_TILE * TOP_K * SUBLANES, lanes), F32),
                            pltpu.VMEM((TOK_TILE * TOP_K * SUBLANES, lanes), F32),
                            pltpu.VMEM((TOK_TILE * SUBLANES, lanes), F32),
                            pltpu.VMEM((TOK_TILE * SUBLANES, lanes), F32),
                            pltpu.SemaphoreType.DMA((2,))]),
        compiler_params=_cparams(("arbitrary",)),
        name="combine",
    )(lpos_flat, seg_cnt, seg_l, seg_g, ys_a, ys_b, x1, gate_x, g2, final_gain)


def _state_ext(c_state, n_state):
    b, h, dh, _ = c_state.shape
    ct = jnp.swapaxes(c_state, -1, -2)
    pad = jnp.zeros((b, h, dh, LANES - 1), F32)
    return jnp.concatenate([ct, n_state[..., None], pad], axis=-1)


def _state_split(cx):
    dh = cx.shape[2]
    return jnp.swapaxes(cx[..., 0:dh], -1, -2), cx[..., dh]


def kernel(x_prompt, x_sample, state_pool, state_mlstm_C, state_mlstm_n, state_mlstm_m, c_prompt, c_sample,
           w_ada, b_ada, w_in, b_gate, pool_w, pool_scale, mlstm_gain, w_out, w_router, b_router,
           w_up, b_up, w_down, b_down, final_gain):
    depth = w_ada.shape[0]
    assert depth == 1, "single-layer trunk"
    bp, sp, d = x_prompt.shape
    bs, ss, _ = x_sample.shape
    tp, ts = bp * sp, bs * ss
    t = tp + ts
    pw = state_pool.shape[-1]
    heads, dh = state_mlstm_C.shape[2], state_mlstm_C.shape[3]
    mw = heads * dh
    n_exp = w_router.shape[-1]
    pad_rows = state_pool.shape[2]
    assert tp % TOK_TILE == 0 and ts % TOK_TILE == 0 and sp % TOK_TILE == 0
    assert ss % MOD_CHUNK == 0 and 2 * heads <= SUBLANES and pad_rows < POOL_HALO

    xp = x_prompt.reshape(tp, d)
    xs = x_sample.reshape(ts, d)

    n_c = bp + bs
    c_rows = -(-n_c // SUBLANES) * SUBLANES
    c_all = jnp.concatenate([c_prompt, c_sample, jnp.zeros((c_rows - n_c, d), F32)], axis=0)
    mod = _adaln(c_all, w_ada[0], b_ada)
    mod_c = jnp.concatenate([
        jnp.broadcast_to(mod[0:bp, None, :], (bp, sp // MOD_CHUNK, mod.shape[1])).reshape(tp // MOD_CHUNK, -1),
        jnp.broadcast_to(mod[bp:n_c, None, :], (bs, ss // MOD_CHUNK, mod.shape[1])).reshape(ts // MOD_CHUNK, -1)])
    sh1 = sc1 = g1 = sh2 = sc2 = g2 = mod_c

    w_in0 = w_in[0]
    o0 = pw
    w_main = jnp.concatenate([w_in0[:, 0:pw], w_in0[:, o0:o0 + mw], w_in0[:, o0 + 2 * mw:o0 + 4 * mw]],
                             axis=1).astype(BF16)
    w_kt = (w_in0[:, o0 + mw:o0 + 2 * mw] * (dh ** -0.5)).T.astype(BF16)
    w_g = w_in0[:, o0 + 4 * mw:]
    ng = w_g.shape[1]
    w_gc = jnp.pad(w_g, ((0, 0), (0, LANES - ng))).astype(BF16)
    w_gt = jnp.pad(w_g.T, ((0, SUBLANES - ng), (0, 0))).astype(BF16)
    b_gc = jnp.pad(b_gate[0], (0, LANES - ng)).reshape(1, LANES)
    b_gr = jnp.pad(b_gate[0], (0, SUBLANES - ng)).reshape(SUBLANES, 1)
    u, q, kt, v, og, gc, gr = _inproj(xp, xs, sc1, sh1, w_main, w_kt, w_gc, w_gt, b_gc, b_gr, pw, mw)

    pool_wb = pool_w[0].astype(BF16)
    zeros_p = jnp.zeros((bp, POOL_HALO, pw), F32)
    st_s = jnp.concatenate([jnp.zeros((bs, POOL_HALO - pad_rows, pw), F32), state_pool[0]], axis=1)
    pool_p = _pool(u, zeros_p, pool_wb, pool_scale, batch=bp, seq=sp, row0=0, tm=TOK_TILE, pos0=0)
    pool_s = _pool(u, st_s, pool_wb, pool_scale, batch=bs, seq=ss, row0=tp, tm=ss, pos0=PAST_LEN)
    new_pool_p = jnp.stack([lax.slice(u, ((b + 1) * sp - pad_rows, 0), ((b + 1) * sp, pw)) for b in range(bp)])
    new_pool_s = jnp.stack([lax.slice(u, (tp + (b + 1) * ss - pad_rows, 0), (tp + (b + 1) * ss, pw))
                            for b in range(bs)])

    gain = mlstm_gain[0].reshape(1, mw)
    c0_p = jnp.zeros((bp, heads, dh, dh + LANES), F32)
    m0_p = jnp.zeros((bp, heads, 1, 1), F32)
    chunk_p = MLSTM_CHUNK if sp % MLSTM_CHUNK == 0 else MOD_CHUNK
    ml_p, cx_p, m_p = _mlstm(q, kt, v, og, gc, gr, gain, c0_p, m0_p, batch=bp, seq=sp, chunk=chunk_p,
                             row0=0, kt_per_seq=False)
    c0_s = _state_ext(state_mlstm_C[0], state_mlstm_n[0])
    m0_s = state_mlstm_m[0].reshape(bs, heads, 1, 1)
    kt_s = kt[:, tp:].reshape(mw, bs, ss).transpose(1, 0, 2)
    gr_s = gr[:, tp:].reshape(gr.shape[0], bs, ss).transpose(1, 0, 2)
    ml_s, cx_s, m_s = _mlstm(q, kt_s, v, og, gc, gr_s, gain, c0_s, m0_s, batch=bs, seq=ss, chunk=ss,
                             row0=tp, kt_per_seq=True)
    new_c_p, new_n_p = _state_split(cx_p)
    new_c_s, new_n_s = _state_split(cx_s)

    w_rt = w_router[0].T.astype(BF16)
    b_r = b_router[0].reshape(n_exp, 1)
    x1, h_packed, idx_t, gate_t = _outproj(pool_p, pool_s, ml_p, ml_s, xp, xs, g1, sc2, sh2,
                                           w_out[0].astype(BF16), w_rt, b_r)

    lpos_t, cnt = _ranks(idx_t, n_exp)
    lpos_flat = (lpos_t.T * SUBLANES).reshape(-1)
    c_te = cnt[:, ::LANES].T.astype(I32)
    counts = jnp.sum(c_te, axis=0)
    padded = (counts + MOE_ROWS - 1) // MOE_ROWS * MOE_ROWS
    pend = jnp.cumsum(padded)
    pstart = pend - padded
    seg_cnt = c_te.reshape(-1)
    seg_l = (jnp.cumsum(c_te, axis=1) - c_te).reshape(-1).astype(I32)
    seg_g = (pstart[None, :] + jnp.cumsum(c_te, axis=0) - c_te).reshape(-1).astype(I32)
    n_assign = t * TOP_K
    nb = -(-(n_assign + n_exp * (MOE_ROWS - 1)) // MOE_ROWS)
    n_act = pend[-1] // MOE_ROWS
    blk_ids = jnp.arange(nb, dtype=I32)
    last = jnp.minimum(blk_ids, n_act - 1)
    blk_e = jnp.minimum(jnp.searchsorted(pend, last * MOE_ROWS, side='right'), n_exp - 1).astype(I32)
    blk_valid = jnp.where(blk_ids < n_act,
                          jnp.clip(counts[blk_e] - (last * MOE_ROWS - pstart[blk_e]), 0, MOE_ROWS), 0).astype(I32)
    sub_end = (counts + MOE_SUB - 1) // MOE_SUB * MOE_SUB
    tail = pend[-1:]
    pad_row = jnp.concatenate([pstart + counts, tail]).astype(I32)
    pad_len = jnp.concatenate([sub_end - counts, jnp.zeros((1,), I32)]).astype(I32)
    fill_row = jnp.concatenate([pstart + sub_end, tail]).astype(I32)
    fill_n = jnp.concatenate([(padded - sub_end) // MOE_SUB, (nb * MOE_ROWS - tail) // MOE_SUB]).astype(I32)
    xs_grouped = _dispatch(lpos_flat, seg_cnt, seg_l, seg_g, pad_row, pad_len, fill_row, fill_n, h_packed,
                           nb * MOE_ROWS, n_exp)
    ys_a, ys_b = _moe(blk_e, blk_valid, n_act.reshape(1).astype(I32), xs_grouped, w_up, b_up[0], w_down, b_down[0])

    gate_x = jnp.broadcast_to(gate_t[:, :, None], (TOP_K, t, LANES))
    y_p, y_s = _combine(lpos_flat, seg_cnt, seg_l, seg_g, ys_a, ys_b, x1, gate_x, g2, final_gain.reshape(1, d),
                        tp, n_exp)

    return (y_p.reshape(bp, sp, d), y_s.reshape(bs, ss, d),
            new_pool_p[None], new_c_p[None], new_n_p[None], m_p.reshape(1, bp, heads),
            new_pool_s[None], new_c_s[None], new_n_s[None], m_s.reshape(1, bs, heads))
```

```python
import functools

import jax
import jax.numpy as jnp
from jax import lax
from jax.experimental import pallas as pl
from jax.experimental.pallas import tpu as pltpu

F32 = jnp.float32
BF16 = jnp.bfloat16
I32 = jnp.int32
U32 = jnp.uint32

EPS = 1e-6
POOL_WINDOWS = (2, 4, 8, 16)
TOP_K = 4
SWIGLU_LIMIT = 7.0
SWIGLU_ALPHA = 1.702
PAST_LEN = 2048

LANES = 128
SUBLANES = 8
MOD_CHUNK = 64
TOK_TILE = 512
POOL_HALO = 16
MLSTM_CHUNK = 256
ADA_TILE = 1024
MOE_ROWS = 768
MOE_SUB = 256
MOE_FF_TILE = 512
SEG_SHIFT = 5
SEG_CHUNK = 1 << SEG_SHIFT
ROW_UNROLL = 16
VMEM_LIMIT = 56 * 1024 * 1024


def _cparams(sem, vmem=VMEM_LIMIT):
    return pltpu.CompilerParams(dimension_semantics=sem, vmem_limit_bytes=vmem)


MOD_SHIFT1, MOD_SCALE1, MOD_GATE1, MOD_SHIFT2, MOD_SCALE2, MOD_GATE2 = range(6)


def _mod_spec(rows, d, which):
    return pl.BlockSpec((rows, d), lambda i, *_: (i, which))


def _resident(shape, index_map):
    return pl.BlockSpec(shape, index_map, pipeline_mode=pl.Buffered(1))


def _dot(a, b):
    return jnp.dot(a, b, preferred_element_type=F32)


def _dot_nt(a, b):
    return lax.dot_general(a, b, (((1,), (1,)), ((), ())), preferred_element_type=F32)


def _split3(x):
    hi = x.astype(BF16)
    r1 = x - hi.astype(F32)
    mid = r1.astype(BF16)
    lo = (r1 - mid.astype(F32)).astype(BF16)
    return hi, mid, lo


def _log_sigmoid(x):
    return jnp.minimum(x, 0.0) - jnp.log1p(jnp.exp(-jnp.abs(x)))


def _adaln_kernel(c_ref, w_ref, b_ref, o_ref):
    c = c_ref[...]
    s = (c * jax.nn.sigmoid(c)).astype(BF16)
    o_ref[...] = _dot(s, w_ref[...].astype(BF16)) + b_ref[...]


def _adaln(c_all, w_ada, b_ada):
    rows, d = c_all.shape
    n = w_ada.shape[1]
    return pl.pallas_call(
        _adaln_kernel,
        out_shape=jax.ShapeDtypeStruct((rows, n), F32),
        grid=(n // ADA_TILE,),
        in_specs=[pl.BlockSpec((rows, d), lambda j: (0, 0)),
                  pl.BlockSpec((d, ADA_TILE), lambda j: (0, j)),
                  pl.BlockSpec((1, ADA_TILE), lambda j: (0, j))],
        out_specs=pl.BlockSpec((rows, ADA_TILE), lambda j: (0, j)),
        compiler_params=_cparams(("parallel",)),
        name="adaln",
    )(c_all, w_ada, b_ada)


def _modulated_norm(x, sc_ref, sh_ref, h_scr):
    xn = x * lax.rsqrt(jnp.mean(x * x, axis=-1, keepdims=True) + EPS)
    for c in range(x.shape[0] // MOD_CHUNK):
        rows = slice(c * MOD_CHUNK, (c + 1) * MOD_CHUNK)
        h_scr[rows, :] = (xn[rows, :] * (1.0 + sc_ref[c:c + 1, :]) + sh_ref[c:c + 1, :]).astype(h_scr.dtype)


def _inproj_kernel(xp_ref, xs_ref, sc_ref, sh_ref, w_ref, wkt_ref, wg_ref, wgt_ref, bgc_ref, bgr_ref,
                   u_ref, q_ref, kt_ref, v_ref, og_ref, gc_ref, gr_ref, h_scr, *, n_p, pw, mw):
    i = pl.program_id(0)

    def body(x_ref):
        _modulated_norm(x_ref[...], sc_ref, sh_ref, h_scr)
        h = h_scr[...]
        u_ref[...] = _dot(h, w_ref[:, 0:pw])
        q_ref[...] = _dot(h, w_ref[:, pw:pw + mw]).astype(BF16)
        v_ref[...] = _dot(h, w_ref[:, pw + mw:pw + 2 * mw]).astype(BF16)
        og_ref[...] = _dot(h, w_ref[:, pw + 2 * mw:pw + 3 * mw])
        kt_ref[...] = _dot_nt(wkt_ref[...], h).astype(BF16)
        gc_ref[...] = _dot(h, wg_ref[...]) + bgc_ref[...]
        gr_ref[...] = _dot_nt(wgt_ref[...], h) + bgr_ref[...]

    pl.when(i < n_p)(functools.partial(body, xp_ref))
    pl.when(i >= n_p)(functools.partial(body, xs_ref))


def _inproj(xp, xs, sc1, sh1, w_main, w_kt, w_gc, w_gt, b_gc, b_gr, pw, mw):
    tp, d = xp.shape
    ts = xs.shape[0]
    t = tp + ts
    n_p = tp // TOK_TILE
    n_s = ts // TOK_TILE
    cpt = TOK_TILE // MOD_CHUNK
    ng = w_gt.shape[0]
    kern = functools.partial(_inproj_kernel, n_p=n_p, pw=pw, mw=mw)
    tok = lambda i: (i, 0)
    return pl.pallas_call(
        kern,
        out_shape=(jax.ShapeDtypeStruct((t, pw), F32),
                   jax.ShapeDtypeStruct((t, mw), BF16),
                   jax.ShapeDtypeStruct((mw, t), BF16),
                   jax.ShapeDtypeStruct((t, mw), BF16),
                   jax.ShapeDtypeStruct((t, mw), F32),
                   jax.ShapeDtypeStruct((t, LANES), F32),
                   jax.ShapeDtypeStruct((ng, t), F32)),
        grid=(n_p + n_s,),
        in_specs=[pl.BlockSpec((TOK_TILE, d), lambda i: (jnp.minimum(i, n_p - 1), 0)),
                  pl.BlockSpec((TOK_TILE, d), lambda i: (jnp.maximum(i - n_p, 0), 0)),
                  _mod_spec(cpt, d, MOD_SCALE1),
                  _mod_spec(cpt, d, MOD_SHIFT1),
                  _resident(w_main.shape, lambda i: (0, 0)),
                  _resident(w_kt.shape, lambda i: (0, 0)),
                  _resident(w_gc.shape, lambda i: (0, 0)),
                  _resident(w_gt.shape, lambda i: (0, 0)),
                  _resident(b_gc.shape, lambda i: (0, 0)),
                  _resident(b_gr.shape, lambda i: (0, 0))],
        out_specs=(pl.BlockSpec((TOK_TILE, pw), tok),
                   pl.BlockSpec((TOK_TILE, mw), tok),
                   pl.BlockSpec((mw, TOK_TILE), lambda i: (0, i)),
                   pl.BlockSpec((TOK_TILE, mw), tok),
                   pl.BlockSpec((TOK_TILE, mw), tok),
                   pl.BlockSpec((TOK_TILE, LANES), tok),
                   pl.BlockSpec((ng, TOK_TILE), lambda i: (0, i))),
        scratch_shapes=[pltpu.VMEM((TOK_TILE, d), BF16)],
        compiler_params=_cparams(("parallel",)),
        name="inproj",
    )(xp, xs, sc1, sh1, w_main, w_kt, w_gc, w_gt, b_gc, b_gr)


def _pool_kernel(u_ref, st_ref, pw_ref, ps_ref, o_ref, xp_scr, *, tm, pos0, group):
    j = pl.program_id(1)

    @pl.when(j == 0)
    def _():
        xp_scr[0:POOL_HALO, :] = st_ref[...]

    @pl.when(j > 0)
    def _():
        xp_scr[0:POOL_HALO, :] = xp_scr[tm:tm + POOL_HALO, :]

    xp_scr[POOL_HALO:POOL_HALO + tm, :] = u_ref[...]
    pos = pos0 + j * tm + lax.broadcasted_iota(I32, (tm, group), 0)
    for g, w in enumerate(POOL_WINDOWS):
        cs = slice(g * group, (g + 1) * group)
        x = xp_scr[POOL_HALO:POOL_HALO + tm, cs]
        acc = x
        for s in range(1, w):
            acc = acc + xp_scr[POOL_HALO - s:POOL_HALO - s + tm, cs]
        cnt = jnp.minimum(pos + 1, w).astype(F32)
        pooled = acc / cnt - x
        mixed = _dot(pooled.astype(BF16), pw_ref[g]) * ps_ref[:, cs]
        o_ref[:, cs] = mixed.astype(o_ref.dtype)


def _pool(u, state16, pool_w, pool_scale, *, batch, seq, row0, tm, pos0):
    c = u.shape[1]
    group = c // len(POOL_WINDOWS)
    nt = seq // tm
    blk0 = row0 // tm
    kern = functools.partial(_pool_kernel, tm=tm, pos0=pos0, group=group)
    return pl.pallas_call(
        kern,
        out_shape=jax.ShapeDtypeStruct((batch * seq, c), BF16),
        grid=(batch, nt),
        in_specs=[pl.BlockSpec((tm, c), lambda b, j: (blk0 + b * nt + j, 0)),
                  pl.BlockSpec((None, POOL_HALO, c), lambda b, j: (b, 0, 0)),
                  _resident(pool_w.shape, lambda b, j: (0, 0, 0)),
                  _resident(pool_scale.shape, lambda b, j: (0, 0))],
        out_specs=pl.BlockSpec((tm, c), lambda b, j: (b * nt + j, 0)),
        scratch_shapes=[pltpu.VMEM((POOL_HALO + tm, c), F32)],
        compiler_params=_cparams(("parallel", "arbitrary")),
        name="pool",
    )(u, state16, pool_w, pool_scale)


def _mlstm_kernel(q_ref, kt_ref, v_ref, og_ref, gc_ref, gr_ref, gain_ref, c0_ref, m0_ref,
                  o_ref, c_ref, m_ref, *, chunk, heads, dh):
    ci = pl.program_id(1)
    L = chunk

    @pl.when(ci == 0)
    def _():
        c_ref[...] = c0_ref[...]
        m_ref[...] = m0_ref[...]

    gc = gc_ref[...]
    gr = gr_ref[...]
    lf_c = _log_sigmoid(gc)
    lf_r = _log_sigmoid(gr)
    row_i = lax.broadcasted_iota(I32, (L, L), 0)
    col_i = lax.broadcasted_iota(I32, (L, L), 1)
    causal = col_i <= row_i
    tri = causal.astype(BF16)
    tri_t = (row_i <= col_i).astype(BF16)
    f_c = sum(_dot(tri, p) for p in _split3(lf_c))
    f_r = sum(_dot(p, tri_t) for p in _split3(lf_r))
    one_col = (lax.broadcasted_iota(I32, (L, LANES), 1) == 0).astype(BF16)
    neg_inf = jnp.float32(-jnp.inf)

    for h in range(heads):
        hs = slice(h * dh, (h + 1) * dh)
        fc = f_c[:, heads + h:heads + h + 1]
        a_r = gr[h:h + 1, :] - f_r[heads + h:heads + h + 1, :]
        m_prev = m_ref[h]
        cm = jnp.max(jnp.where(causal, a_r, neg_inf), axis=1, keepdims=True)
        m_t = fc + jnp.maximum(m_prev, cm)
        dm = jnp.exp(jnp.where(causal, (fc - m_t) + a_r, neg_inf))
        qh = q_ref[:, hs]
        kth = kt_ref[hs, :]
        vh = v_ref[:, hs]
        sc = _dot(qh, kth) * dm
        inter = jnp.exp(fc + m_prev - m_t)
        cx = c_ref[h]
        g = _dot(qh, cx.astype(BF16))
        num = _dot(sc.astype(BF16), vh) + inter * g[:, 0:dh]
        den = jnp.sum(sc, axis=1, keepdims=True) + inter * g[:, dh:dh + 1]
        hh = num * (1.0 / jnp.maximum(jnp.abs(den), jnp.exp(-m_t)))
        hn = hh * lax.rsqrt(jnp.mean(hh * hh, axis=1, keepdims=True) + EPS) * gain_ref[:, hs]
        o_ref[:, hs] = (jax.nn.sigmoid(og_ref[:, hs]) * hn).astype(o_ref.dtype)

        m_last = m_t[L - 1:L, :]
        f_last = fc[L - 1:L, :]
        w_r = jnp.exp(f_last + a_r - m_last)
        decay = jnp.exp(f_last + m_prev - m_last)
        kw = (kth.astype(F32) * w_r).astype(BF16)
        v_ext = jnp.concatenate([vh, one_col], axis=1)
        c_ref[h] = decay * cx + _dot(kw, v_ext)
        m_ref[h] = m_last


def _mlstm(q, kt, v, og, gc, gr, gain, c0, m0, *, batch, seq, chunk, row0, kt_per_seq):
    mw = q.shape[1]
    heads, dh = c0.shape[1], c0.shape[2]
    nc = seq // chunk
    blk0 = row0 // chunk
    ng = gr.shape[-2]
    tok = lambda b, c: (blk0 + b * nc + c, 0)
    if kt_per_seq:
        kt_spec = pl.BlockSpec((None, mw, chunk), lambda b, c: (b, 0, c))
        gr_spec = pl.BlockSpec((None, ng, chunk), lambda b, c: (b, 0, c))
    else:
        kt_spec = pl.BlockSpec((mw, chunk), lambda b, c: (0, blk0 + b * nc + c))
        gr_spec = pl.BlockSpec((ng, chunk), lambda b, c: (0, blk0 + b * nc + c))
    kern = functools.partial(_mlstm_kernel, chunk=chunk, heads=heads, dh=dh)
    st_spec = pl.BlockSpec((None, heads, dh, dh + LANES), lambda b, c: (b, 0, 0, 0))
    m_spec = pl.BlockSpec((None, heads, 1, 1), lambda b, c: (b, 0, 0, 0))
    return pl.pallas_call(
        kern,
        out_shape=(jax.ShapeDtypeStruct((batch * seq, mw), BF16),
                   jax.ShapeDtypeStruct(c0.shape, F32),
                   jax.ShapeDtypeStruct(m0.shape, F32)),
        grid=(batch, nc),
        in_specs=[pl.BlockSpec((chunk, mw), tok),
                  kt_spec,
                  pl.BlockSpec((chunk, mw), tok),
                  pl.BlockSpec((chunk, mw), tok),
                  pl.BlockSpec((chunk, LANES), tok),
                  gr_spec,
                  _resident(gain.shape, lambda b, c: (0, 0)),
                  st_spec, m_spec],
        out_specs=(pl.BlockSpec((chunk, mw), lambda b, c: (b * nc + c, 0)), st_spec, m_spec),
        compiler_params=_cparams(("parallel", "arbitrary")),
        name="mlstm",
    )(q, kt, v, og, gc, gr, gain, c0, m0)


def _outproj_body(p_ref, m_ref, x_ref, g1_ref, sc_ref, sh_ref, wo_ref, wrt_ref, br_ref,
                  x1_ref, hp_ref, idx_ref, gate_ref, h_scr, *, pw, n_exp):
    mix = _dot(p_ref[...], wo_ref[0:pw, :]) + _dot(m_ref[...], wo_ref[pw:, :])
    x = x_ref[...]
    tm, d = x.shape
    for c in range(tm // MOD_CHUNK):
        rows = slice(c * MOD_CHUNK, (c + 1) * MOD_CHUNK)
        x1_ref[rows, :] = x[rows, :] + g1_ref[c:c + 1, :] * mix[rows, :]
    _modulated_norm(x1_ref[...], sc_ref, sh_ref, h_scr)
    hb = h_scr[...]
    hi = lax.bitcast_convert_type(hb[:, 0:d // 2].astype(F32), U32)
    lo = lax.bitcast_convert_type(hb[:, d // 2:].astype(F32), U32)
    words = (hi & jnp.uint32(0xFFFF0000)) | (lo >> 16)
    n_slab = words.shape[1] // LANES
    for k in range(n_slab):
        hp_ref[pl.ds(k, tm, stride=n_slab), :] = words[:, k * LANES:(k + 1) * LANES]

    logits = _dot_nt(wrt_ref[...], hb) + br_ref[...]
    e_iota = lax.broadcasted_iota(I32, logits.shape, 0)
    vals = []
    for j in range(TOP_K):
        mx = jnp.max(logits, axis=0, keepdims=True)
        ix = jnp.min(jnp.where(logits == mx, e_iota, n_exp), axis=0, keepdims=True)
        idx_ref[j:j + 1, :] = ix
        vals.append(mx)
        logits = jnp.where(e_iota == ix, -jnp.inf, logits)
    ex = [jnp.exp(v - vals[0]) for v in vals]
    tot = ex[0]
    for e in ex[1:]:
        tot = tot + e
    inv = 1.0 / tot
    for j in range(TOP_K):
        gate_ref[j:j + 1, :] = ex[j] * inv


def _outproj_kernel(pp_ref, ps_ref, mp_ref, ms_ref, xp_ref, xs_ref, g1_ref, sc_ref, sh_ref,
                    wo_ref, wrt_ref, br_ref, x1_ref, hp_ref, idx_ref, gate_ref, h_scr, *, n_p, pw, n_exp):
    i = pl.program_id(0)
    rest = (g1_ref, sc_ref, sh_ref, wo_ref, wrt_ref, br_ref, x1_ref, hp_ref, idx_ref, gate_ref, h_scr)

    @pl.when(i < n_p)
    def _():
        _outproj_body(pp_ref, mp_ref, xp_ref, *rest, pw=pw, n_exp=n_exp)

    @pl.when(i >= n_p)
    def _():
        _outproj_body(ps_ref, ms_ref, xs_ref, *rest, pw=pw, n_exp=n_exp)


def _outproj(pool_p, pool_s, ml_p, ml_s, xp, xs, g1, sc2, sh2, w_out, w_rt, b_r):
    tp, d = xp.shape
    ts = xs.shape[0]
    t = tp + ts
    pw = pool_p.shape[1]
    mw = ml_p.shape[1]
    n_exp = w_rt.shape[0]
    n_p = tp // TOK_TILE
    n_s = ts // TOK_TILE
    cpt = TOK_TILE // MOD_CHUNK
    kern = functools.partial(_outproj_kernel, n_p=n_p, pw=pw, n_exp=n_exp)
    tok = lambda i: (i, 0)
    pidx = lambda i: (jnp.minimum(i, n_p - 1), 0)
    sidx = lambda i: (jnp.maximum(i - n_p, 0), 0)
    return pl.pallas_call(
        kern,
        out_shape=(jax.ShapeDtypeStruct((t, d), F32),
                   jax.ShapeDtypeStruct((t * (d // 2 // LANES), LANES), U32),
                   jax.ShapeDtypeStruct((TOP_K, t), I32),
                   jax.ShapeDtypeStruct((TOP_K, t), F32)),
        grid=(n_p + n_s,),
        in_specs=[pl.BlockSpec((TOK_TILE, pw), pidx), pl.BlockSpec((TOK_TILE, pw), sidx),
                  pl.BlockSpec((TOK_TILE, mw), pidx), pl.BlockSpec((TOK_TILE, mw), sidx),
                  pl.BlockSpec((TOK_TILE, d), pidx), pl.BlockSpec((TOK_TILE, d), sidx),
                  _mod_spec(cpt, d, MOD_GATE1), _mod_spec(cpt, d, MOD_SCALE2), _mod_spec(cpt, d, MOD_SHIFT2),
                  _resident(w_out.shape, lambda i: (0, 0)),
                  _resident(w_rt.shape, lambda i: (0, 0)),
                  _resident(b_r.shape, lambda i: (0, 0))],
        out_specs=(pl.BlockSpec((TOK_TILE, d), tok),
                   pl.BlockSpec((TOK_TILE * (d // 2 // LANES), LANES), tok),
                   pl.BlockSpec((TOP_K, TOK_TILE), lambda i: (0, i)),
                   pl.BlockSpec((TOP_K, TOK_TILE), lambda i: (0, i))),
        scratch_shapes=[pltpu.VMEM((TOK_TILE, d), BF16)],
        compiler_params=_cparams(("parallel",)),
        name="outproj",
    )(pool_p, pool_s, ml_p, ml_s, xp, xs, g1, sc2, sh2, w_out, w_rt, b_r)


def _rank_kernel(idx_ref, lpos_ref, cnt_ref, *, n_exp):
    tm = idx_ref.shape[1]
    e_iota = lax.broadcasted_iota(I32, (n_exp, tm), 0)
    hots = [e_iota == idx_ref[j:j + 1, :] for j in range(TOP_K)]
    cnt = hots[0].astype(F32)
    for hot in hots[1:]:
        cnt = cnt + hot.astype(F32)
    cnt_b = cnt.astype(BF16)
    r = lax.broadcasted_iota(I32, (tm, tm), 0)
    c = lax.broadcasted_iota(I32, (tm, tm), 1)
    earlier_tok = _dot(cnt_b, (r < c).astype(BF16))
    er = lax.broadcasted_iota(I32, (n_exp, n_exp), 0)
    ec = lax.broadcasted_iota(I32, (n_exp, n_exp), 1)
    lower_exp = _dot((ec < er).astype(BF16), cnt_b)
    seg_start = jnp.sum(lower_exp, axis=1, keepdims=True)
    pos = earlier_tok + seg_start
    for j in range(TOP_K):
        lpos_ref[j:j + 1, :] = jnp.sum(jnp.where(hots[j], pos, 0.0), axis=0, keepdims=True).astype(I32)
    cnt_ref[...] = jnp.broadcast_to(jnp.sum(cnt, axis=1, keepdims=True), cnt_ref.shape)


def _ranks(idx_t, n_exp):
    k, t = idx_t.shape
    n_tiles = t // TOK_TILE
    return pl.pallas_call(
        functools.partial(_rank_kernel, n_exp=n_exp),
        out_shape=(jax.ShapeDtypeStruct((k, t), I32), jax.ShapeDtypeStruct((n_exp, n_tiles * LANES), F32)),
        grid=(n_tiles,),
        in_specs=[pl.BlockSpec((k, TOK_TILE), lambda i: (0, i))],
        out_specs=(pl.BlockSpec((k, TOK_TILE), lambda i: (0, i)),
                   pl.BlockSpec((n_exp, LANES), lambda i: (0, i))),
        compiler_params=_cparams(("parallel",)),
        name="ranks",
    )(idx_t)


def _tile_rows(first, n):
    start = first * SUBLANES
    if not isinstance(start, int):
        start = pl.multiple_of(start, SUBLANES)
    return pl.ds(start, n * SUBLANES)


def _slab(first, n, k):
    return pl.ds(first * SUBLANES + k, n, stride=SUBLANES)


def _segment_copies(count, make_copy, op):
    n_full = count >> SEG_SHIFT

    def full(k, carry):
        getattr(make_copy(k * SEG_CHUNK, SEG_CHUNK), op)()
        return carry

    lax.fori_loop(0, n_full, full, 0)
    rest = count - n_full * SEG_CHUNK
    base = n_full * SEG_CHUNK
    bit = SEG_CHUNK // 2
    while bit:
        off = base + (rest & -(2 * bit))

        def one(off=off, bit=bit):
            getattr(make_copy(off, bit), op)()

        pl.when((rest & bit) != 0)(one)
        bit //= 2


def _dispatch_kernel(lpos_ref, segc_ref, segl_ref, segg_ref, padrow_ref, padlen_ref, fillrow_ref, filln_ref,
                     h_ref, xs_ref, sorted_a, sorted_b, zero_scr, sem, *, n_steps, n_exp, n_regions):
    i = pl.program_id(0)
    tm = h_ref.shape[0] // SUBLANES
    base = i * tm
    slots = ((sorted_a, 0), (sorted_b, 1))

    def wait_sent(buf, which):
        pltpu.make_async_copy(buf, xs_ref.at[pl.ds(0, buf.shape[0]), :], sem.at[which]).wait()

    def run(buf, which):
        @pl.when(i >= 2)
        def _():
            wait_sent(buf, which)

        def place(t, carry):
            row = h_ref[_tile_rows(t, 1), :]
            for j in range(TOP_K):
                dst = pl.multiple_of(lpos_ref[(base + t) * TOP_K + j], SUBLANES)
                buf[pl.ds(dst, SUBLANES), :] = row
            return carry

        lax.fori_loop(0, tm, place, 0, unroll=ROW_UNROLL)

        def send(e, carry):
            s = i * n_exp + e
            src0 = segl_ref[s]
            dst0 = segg_ref[s]
            _segment_copies(segc_ref[s], lambda off, n: pltpu.make_async_copy(
                buf.at[_tile_rows(src0 + off, n), :], xs_ref.at[_tile_rows(dst0 + off, n), :], sem.at[which]), "start")
            return carry

        lax.fori_loop(0, n_exp, send, 0)

    for buf, which in slots:
        pl.when(i % 2 == which)(functools.partial(run, buf, which))

    @pl.when(i == 0)
    def _():
        zero_scr[...] = jnp.zeros_like(zero_scr)

        def fill_copy(e, c):
            return pltpu.make_async_copy(zero_scr, xs_ref.at[_tile_rows(fillrow_ref[e] + c * MOE_SUB, MOE_SUB), :],
                                         sem.at[2])

        def region(op):
            def go(e, carry):
                dst0 = padrow_ref[e]
                _segment_copies(padlen_ref[e], lambda off, n: pltpu.make_async_copy(
                    zero_scr.at[_tile_rows(0, n), :], xs_ref.at[_tile_rows(dst0 + off, n), :], sem.at[2]), op)
                lax.fori_loop(0, filln_ref[e], lambda k, c: (getattr(fill_copy(e, k), op)(), c)[1], 0)
                return carry
            return go

        lax.fori_loop(0, n_regions, region("start"), 0)
        lax.fori_loop(0, n_regions, region("wait"), 0)

    @pl.when(i == n_steps - 1)
    def _():
        for buf, which in slots[:min(n_steps, 2)]:
            wait_sent(buf, which)


def _dispatch(lpos_flat, seg_cnt, seg_l, seg_g, pad_row, pad_len, fill_row, fill_n, h_tiles, rows, n_exp):
    t = h_tiles.shape[0] // SUBLANES
    kern = functools.partial(_dispatch_kernel, n_steps=t // TOK_TILE, n_exp=n_exp, n_regions=pad_row.shape[0])
    return pl.pallas_call(
        kern,
        out_shape=jax.ShapeDtypeStruct((rows * SUBLANES, LANES), U32),
        grid_spec=pltpu.PrefetchScalarGridSpec(
            num_scalar_prefetch=8,
            grid=(t // TOK_TILE,),
            in_specs=[pl.BlockSpec((TOK_TILE * SUBLANES, LANES), lambda i, *_: (i, 0))],
            out_specs=pl.BlockSpec(memory_space=pl.ANY),
            scratch_shapes=[pltpu.VMEM((TOK_TILE * TOP_K * SUBLANES, LANES), U32),
                            pltpu.VMEM((TOK_TILE * TOP_K * SUBLANES, LANES), U32),
                            pltpu.VMEM((MOE_SUB * SUBLANES, LANES), U32),
                            pltpu.SemaphoreType.DMA((3,))]),
        compiler_params=_cparams(("arbitrary",)),
        name="dispatch",
    )(lpos_flat, seg_cnt, seg_l, seg_g, pad_row, pad_len, fill_row, fill_n, h_tiles)


def _unpack_rows(x_ref, first, n):
    his, los = [], []
    for k in range(SUBLANES):
        words = x_ref[_slab(first, n, k), :]
        his.append(lax.bitcast_convert_type(words & jnp.uint32(0xFFFF0000), F32).astype(BF16))
        los.append(lax.bitcast_convert_type(words << 16, F32).astype(BF16))
    return jnp.concatenate(his + los, axis=1)


def _moe_kernel(be_ref, bv_ref, na_ref, x_ref, wg_ref, wl_ref, bg_ref, bl_ref, wd_ref, bd_ref,
                ya_ref, yb_ref, acc_scr, sta_scr, stb_scr, wup_scr, wd_scr, sem, *, tf):
    b = pl.program_id(0)
    f = pl.program_id(1)
    n_blk = pl.num_programs(0)
    last_f = pl.num_programs(1) - 1
    rows_total = x_ref.shape[0] // SUBLANES
    n_sub_max = rows_total // MOE_SUB
    half = SUBLANES * LANES
    active = b < na_ref[0]

    def writeback(blk):
        dst = _tile_rows(blk * rows_total, rows_total)
        return (pltpu.make_async_copy(sta_scr, ya_ref.at[dst, :], sem.at[0]),
                pltpu.make_async_copy(stb_scr, yb_ref.at[dst, :], sem.at[1]))

    @pl.when(jnp.logical_and(f == last_f, b > 0))
    def _():
        for cp in writeback(b - 1):
            cp.wait()

    @pl.when(jnp.logical_and(b == na_ref[0], f == last_f))
    def _():
        sta_scr[...] = jnp.zeros_like(sta_scr)
        stb_scr[...] = jnp.zeros_like(stb_scr)

    @pl.when(active)
    def _():
        @pl.when(f == 0)
        def _():
            acc_scr[...] = jnp.broadcast_to(bd_ref[...], acc_scr.shape)

        n_sub = (bv_ref[b] + MOE_SUB - 1) // MOE_SUB

        def stage(s, y):
            for k in range(SUBLANES):
                slab = _slab(s * MOE_SUB, MOE_SUB, k)
                sta_scr[slab, :] = y[:, k * LANES:(k + 1) * LANES]
                stb_scr[slab, :] = y[:, half + k * LANES:half + (k + 1) * LANES]

        def body(n_live, to_staging):
            wup_scr[:, 0:tf] = wg_ref[...].astype(BF16)
            wup_scr[:, tf:] = wl_ref[...].astype(BF16)
            wd_scr[...] = wd_ref[...].astype(BF16)
            for s in range(n_live):
                rows = slice(s * MOE_SUB, (s + 1) * MOE_SUB)
                x = _unpack_rows(x_ref, s * MOE_SUB, MOE_SUB)
                up = _dot(x, wup_scr[...])
                g = jnp.minimum(up[:, 0:tf] + bg_ref[...], SWIGLU_LIMIT)
                lin = jnp.clip(up[:, tf:] + bl_ref[...], -SWIGLU_LIMIT, SWIGLU_LIMIT)
                act = g * jax.nn.sigmoid(SWIGLU_ALPHA * g) * (lin + 1.0)
                y = acc_scr[rows, :] + _dot(act.astype(BF16), wd_scr[...])
                if to_staging:
                    stage(s, y)
                else:
                    acc_scr[rows, :] = y

        full = n_sub == n_sub_max
        for n_live in range(1, n_sub_max):
            pl.when(n_sub == n_live)(functools.partial(body, n_live, False))
        pl.when(jnp.logical_and(full, f != last_f))(functools.partial(body, n_sub_max, False))
        pl.when(jnp.logical_and(full, f == last_f))(functools.partial(body, n_sub_max, True))

        @pl.when(jnp.logical_and(jnp.logical_not(full), f == last_f))
        def _():
            for s in range(n_sub_max):
                stage(s, acc_scr[s * MOE_SUB:(s + 1) * MOE_SUB, :])

    @pl.when(f == last_f)
    def _():
        for cp in writeback(b):
            cp.start()

        @pl.when(b == n_blk - 1)
        def _():
            for cp in writeback(b):
                cp.wait()


def _moe(blk_e, blk_valid, n_act, xs, w_up, b_up, w_down, b_down):
    rows = xs.shape[0] // SUBLANES
    d = 2 * SUBLANES * LANES
    n_exp, _, two_f = w_up.shape[1:]
    assert w_up.shape[2] == d, "one (8, 128) tile of packed words per row"
    ff = two_f // 2
    tf = MOE_FF_TILE
    nf = ff // tf
    nb = rows // MOE_ROWS
    b_up3 = b_up.reshape(n_exp, 1, two_f)
    b_dn3 = b_down.reshape(n_exp, 1, d)

    def blk(b, f, be, bv, na):
        return jnp.minimum(b, na[0] - 1)

    def ftile(b, f, be, bv, na):
        return jnp.where(b < na[0], f, nf - 1)

    y_shape = jax.ShapeDtypeStruct((rows * SUBLANES, LANES), F32)
    return pl.pallas_call(
        functools.partial(_moe_kernel, tf=tf),
        out_shape=(y_shape, y_shape),
        grid_spec=pltpu.PrefetchScalarGridSpec(
            num_scalar_prefetch=3,
            grid=(nb, nf),
            in_specs=[pl.BlockSpec((MOE_ROWS * SUBLANES, LANES), lambda *a: (blk(*a), 0)),
                      pl.BlockSpec((None, None, d, tf), lambda *a: (0, a[2][a[0]], 0, ftile(*a))),
                      pl.BlockSpec((None, None, d, tf), lambda *a: (0, a[2][a[0]], 0, ftile(*a) + nf)),
                      pl.BlockSpec((None, 1, tf), lambda *a: (a[2][a[0]], 0, ftile(*a))),
                      pl.BlockSpec((None, 1, tf), lambda *a: (a[2][a[0]], 0, ftile(*a) + nf)),
                      pl.BlockSpec((None, None, tf, d), lambda *a: (0, a[2][a[0]], ftile(*a), 0)),
                      pl.BlockSpec((None, 1, d), lambda *a: (a[2][a[0]], 0, 0))],
            out_specs=(pl.BlockSpec(memory_space=pl.ANY), pl.BlockSpec(memory_space=pl.ANY)),
            scratch_shapes=[pltpu.VMEM((MOE_ROWS, d), F32),
                            pltpu.VMEM((MOE_ROWS * SUBLANES, LANES), F32),
                            pltpu.VMEM((MOE_ROWS * SUBLANES, LANES), F32),
                            pltpu.VMEM((d, 2 * tf), BF16), pltpu.VMEM((tf, d), BF16),
                            pltpu.SemaphoreType.DMA((2,))]),
        compiler_params=_cparams(("arbitrary", "arbitrary")),
        name="moe_experts",
    )(blk_e, blk_valid, n_act, xs, w_up, w_up, b_up3, b_up3, w_down, b_dn3)


def _combine_kernel(lpos_ref, segc_ref, segl_ref, segg_ref, ya_ref, yb_ref, x1_ref, gate_ref, g2_ref, fg_ref,
                    yp_ref, yo_ref, bufa, bufb, outa, outb, sem, *, n_tok, n_exp, n_p):
    i = pl.program_id(0)
    n_steps = pl.num_programs(0)
    tm = x1_ref.shape[0]
    base = i * tm
    halves = ((ya_ref, bufa, outa, 0), (yb_ref, bufb, outb, 1))

    def fetch(tile, y_ref, buf, which):
        def go(e, carry):
            s = tile * n_exp + e
            src0 = segg_ref[s]
            dst0 = segl_ref[s]
            _segment_copies(segc_ref[s], lambda off, n: pltpu.make_async_copy(
                y_ref.at[_tile_rows(src0 + off, n), :], buf.at[_tile_rows(dst0 + off, n), :], sem.at[which]), "start")
            return carry
        lax.fori_loop(0, n_exp, go, 0)

    def mix(buf, out):
        def go(t, carry):
            acc = None
            for j in range(TOP_K):
                src = pl.ds(pl.multiple_of(lpos_ref[(base + t) * TOP_K + j], SUBLANES), SUBLANES)
                term = gate_ref[j, pl.ds(t, 1), :] * buf[src, :]
                acc = term if acc is None else acc + term
            out[_tile_rows(t, 1), :] = acc
            return carry
        lax.fori_loop(0, tm, go, 0, unroll=ROW_UNROLL)

    @pl.when(i == 0)
    def _():
        for y_ref, buf, _, which in halves:
            fetch(0, y_ref, buf, which)

    for y_ref, buf, out, which in halves:
        pltpu.make_async_copy(y_ref.at[pl.ds(0, buf.shape[0]), :], buf, sem.at[which]).wait()
        mix(buf, out)

        @pl.when(i + 1 < n_steps)
        def _():
            fetch(i + 1, y_ref, buf, which)

    def finish(out_ref):
        for c in range(tm // MOD_CHUNK):
            rows = slice(c * MOD_CHUNK, (c + 1) * MOD_CHUNK)
            moe = jnp.concatenate([outa[_slab(c * MOD_CHUNK, MOD_CHUNK, k), :] for k in range(SUBLANES)] +
                                  [outb[_slab(c * MOD_CHUNK, MOD_CHUNK, k), :] for k in range(SUBLANES)], axis=1)
            xo = x1_ref[rows, :] + g2_ref[c:c + 1, :] * moe
            out_ref[rows, :] = xo * lax.rsqrt(jnp.mean(xo * xo, axis=-1, keepdims=True) + EPS) * fg_ref[...]

    @pl.when(i < n_p)
    def _():
        finish(yp_ref)

    @pl.when(i >= n_p)
    def _():
        finish(yo_ref)


def _combine(lpos_flat, seg_cnt, seg_l, seg_g, ys_a, ys_b, x1, gate_x, g2, final_gain, tp, n_exp):
    t, d = x1.shape
    lanes = LANES
    ts = t - tp
    n_p = tp // TOK_TILE
    n_s = ts // TOK_TILE
    cpt = TOK_TILE // MOD_CHUNK
    kern = functools.partial(_combine_kernel, n_tok=t, n_exp=n_exp, n_p=n_p)
    tok = lambda i, *_: (i, 0)
    return pl.pallas_call(
        kern,
        out_shape=(jax.ShapeDtypeStruct((tp, d), F32), jax.ShapeDtypeStruct((ts, d), F32)),
        grid_spec=pltpu.PrefetchScalarGridSpec(
            num_scalar_prefetch=4,
            grid=(n_p + n_s,),
            in_specs=[pl.BlockSpec(memory_space=pl.ANY),
                      pl.BlockSpec(memory_space=pl.ANY),
                      pl.BlockSpec((TOK_TILE, d), tok),
                      pl.BlockSpec((TOP_K, TOK_TILE, lanes), lambda i, *_: (0, i, 0)),
                      _mod_spec(cpt, d, MOD_GATE2),
                      pl.BlockSpec((1, d), lambda i, *_: (0, 0))],
            out_specs=(pl.BlockSpec((TOK_TILE, d), lambda i, *_: (jnp.minimum(i, n_p - 1), 0)),
                       pl.BlockSpec((TOK_TILE, d), lambda i, *_: (jnp.maximum(i - n_p, 0), 0))),
            scratch_shapes=[pltpu.VMEM((TOK_TILE * TOP_K * SUBLANES, lanes), F32),
                            pltpu.VMEM((TOK_TILE * TOP_K * SUBLANES, lanes), F32),
                            pltpu.VMEM((TOK_TILE * SUBLANES, lanes), F32),
                            pltpu.VMEM((TOK_TILE * SUBLANES, lanes), F32),
                            pltpu.SemaphoreType.DMA((2,))]),
        compiler_params=_cparams(("arbitrary",)),
        name="combine",
    )(lpos_flat, seg_cnt, seg_l, seg_g, ys_a, ys_b, x1, gate_x, g2, final_gain)


def _state_ext(c_state, n_state):
    b, h, dh, _ = c_state.shape
    ct = jnp.swapaxes(c_state, -1, -2)
    pad = jnp.zeros((b, h, dh, LANES - 1), F32)
    return jnp.concatenate([ct, n_state[..., None], pad], axis=-1)


def _state_split(cx):
    dh = cx.shape[2]
    return jnp.swapaxes(cx[..., 0:dh], -1, -2), cx[..., dh]


def kernel(x_prompt, x_sample, state_pool, state_mlstm_C, state_mlstm_n, state_mlstm_m, c_prompt, c_sample,
           w_ada, b_ada, w_in, b_gate, pool_w, pool_scale, mlstm_gain, w_out, w_router, b_router,
           w_up, b_up, w_down, b_down, final_gain):
    depth = w_ada.shape[0]
    assert depth == 1, "single-layer trunk"
    bp, sp, d = x_prompt.shape
    bs, ss, _ = x_sample.shape
    tp, ts = bp * sp, bs * ss
    t = tp + ts
    pw = state_pool.shape[-1]
    heads, dh = state_mlstm_C.shape[2], state_mlstm_C.shape[3]
    mw = heads * dh
    n_exp = w_router.shape[-1]
    pad_rows = state_pool.shape[2]
    assert tp % TOK_TILE == 0 and ts % TOK_TILE == 0 and sp % TOK_TILE == 0
    assert ss % MOD_CHUNK == 0 and 2 * heads <= SUBLANES and pad_rows < POOL_HALO

    xp = x_prompt.reshape(tp, d)
    xs = x_sample.reshape(ts, d)

    n_c = bp + bs
    c_rows = -(-n_c // SUBLANES) * SUBLANES
    c_all = jnp.concatenate([c_prompt, c_sample, jnp.zeros((c_rows - n_c, d), F32)], axis=0)
    mod = _adaln(c_all, w_ada[0], b_ada)
    mod_c = jnp.concatenate([
        jnp.broadcast_to(mod[0:bp, None, :], (bp, sp // MOD_CHUNK, mod.shape[1])).reshape(tp // MOD_CHUNK, -1),
        jnp.broadcast_to(mod[bp:n_c, None, :], (bs, ss // MOD_CHUNK, mod.shape[1])).reshape(ts // MOD_CHUNK, -1)])
    sh1 = sc1 = g1 = sh2 = sc2 = g2 = mod_c

    w_in0 = w_in[0]
    o0 = pw
    w_main = jnp.concatenate([w_in0[:, 0:pw], w_in0[:, o0:o0 + mw], w_in0[:, o0 + 2 * mw:o0 + 4 * mw]],
                             axis=1).astype(BF16)
    w_kt = (w_in0[:, o0 + mw:o0 + 2 * mw] * (dh ** -0.5)).T.astype(BF16)
    w_g = w_in0[:, o0 + 4 * mw:]
    ng = w_g.shape[1]
    w_gc = jnp.pad(w_g, ((0, 0), (0, LANES - ng))).astype(BF16)
    w_gt = jnp.pad(w_g.T, ((0, SUBLANES - ng), (0, 0))).astype(BF16)
    b_gc = jnp.pad(b_gate[0], (0, LANES - ng)).reshape(1, LANES)
    b_gr = jnp.pad(b_gate[0], (0, SUBLANES - ng)).reshape(SUBLANES, 1)
    u, q, kt, v, og, gc, gr = _inproj(xp, xs, sc1, sh1, w_main, w_kt, w_gc, w_gt, b_gc, b_gr, pw, mw)

    pool_wb = pool_w[0].astype(BF16)
    zeros_p = jnp.zeros((bp, POOL_HALO, pw), F32)
    st_s = jnp.concatenate([jnp.zeros((bs, POOL_HALO - pad_rows, pw), F32), state_pool[0]], axis=1)
    pool_p = _pool(u, zeros_p, pool_wb, pool_scale, batch=bp, seq=sp, row0=0, tm=TOK_TILE, pos0=0)
    pool_s = _pool(u, st_s, pool_wb, pool_scale, batch=bs, seq=ss, row0=tp, tm=ss, pos0=PAST_LEN)
    new_pool_p = jnp.stack([lax.slice(u, ((b + 1) * sp - pad_rows, 0), ((b + 1) * sp, pw)) for b in range(bp)])
    new_pool_s = jnp.stack([lax.slice(u, (tp + (b + 1) * ss - pad_rows, 0), (tp + (b + 1) * ss, pw))
                            for b in range(bs)])

    gain = mlstm_gain[0].reshape(1, mw)
    c0_p = jnp.zeros((bp, heads, dh, dh + LANES), F32)
    m0_p = jnp.zeros((bp, heads, 1, 1), F32)
    chunk_p = MLSTM_CHUNK if sp % MLSTM_CHUNK == 0 else MOD_CHUNK
    ml_p, cx_p, m_p = _mlstm(q, kt, v, og, gc, gr, gain, c0_p, m0_p, batch=bp, seq=sp, chunk=chunk_p,
                             row0=0, kt_per_seq=False)
    c0_s = _state_ext(state_mlstm_C[0], state_mlstm_n[0])
    m0_s = state_mlstm_m[0].reshape(bs, heads, 1, 1)
    kt_s = kt[:, tp:].reshape(mw, bs, ss).transpose(1, 0, 2)
    gr_s = gr[:, tp:].reshape(gr.shape[0], bs, ss).transpose(1, 0, 2)
    ml_s, cx_s, m_s = _mlstm(q, kt_s, v, og, gc, gr_s, gain, c0_s, m0_s, batch=bs, seq=ss, chunk=ss,
                             row0=tp, kt_per_seq=True)
    new_c_p, new_n_p = _state_split(cx_p)
    new_c_s, new_n_s = _state_split(cx_s)

    w_rt = w_router[0].T.astype(BF16)
    b_r = b_router[0].reshape(n_exp, 1)
    x1, h_packed, idx_t, gate_t = _outproj(pool_p, pool_s, ml_p, ml_s, xp, xs, g1, sc2, sh2,
                                           w_out[0].astype(BF16), w_rt, b_r)

    lpos_t, cnt = _ranks(idx_t, n_exp)
    lpos_flat = (lpos_t.T * SUBLANES).reshape(-1)
    c_te = cnt[:, ::LANES].T.astype(I32)
    counts = jnp.sum(c_te, axis=0)
    padded = (counts + MOE_ROWS - 1) // MOE_ROWS * MOE_ROWS
    pend = jnp.cumsum(padded)
    pstart = pend - padded
    seg_cnt = c_te.reshape(-1)
    seg_l = (jnp.cumsum(c_te, axis=1) - c_te).reshape(-1).astype(I32)
    seg_g = (pstart[None, :] + jnp.cumsum(c_te, axis=0) - c_te).reshape(-1).astype(I32)
    n_assign = t * TOP_K
    nb = -(-(n_assign + n_exp * (MOE_ROWS - 1)) // MOE_ROWS)
    n_act = pend[-1] // MOE_ROWS
    blk_ids = jnp.arange(nb, dtype=I32)
    last = jnp.minimum(blk_ids, n_act - 1)
    blk_e = jnp.minimum(jnp.searchsorted(pend, last * MOE_ROWS, side='right'), n_exp - 1).astype(I32)
    blk_valid = jnp.where(blk_ids < n_act,
                          jnp.clip(counts[blk_e] - (last * MOE_ROWS - pstart[blk_e]), 0, MOE_ROWS), 0).astype(I32)
    sub_end = (counts + MOE_SUB - 1) // MOE_SUB * MOE_SUB
    tail = pend[-1:]
    pad_row = jnp.concatenate([pstart + counts, tail]).astype(I32)
    pad_len = jnp.concatenate([sub_end - counts, jnp.zeros((1,), I32)]).astype(I32)
    fill_row = jnp.concatenate([pstart + sub_end, tail]).astype(I32)
    fill_n = jnp.concatenate([(padded - sub_end) // MOE_SUB, (nb * MOE_ROWS - tail) // MOE_SUB]).astype(I32)
    xs_grouped = _dispatch(lpos_flat, seg_cnt, seg_l, seg_g, pad_row, pad_len, fill_row, fill_n, h_packed,
                           nb * MOE_ROWS, n_exp)
    ys_a, ys_b = _moe(blk_e, blk_valid, n_act.reshape(1).astype(I32), xs_grouped, w_up, b_up[0], w_down, b_down[0])

    gate_x = jnp.broadcast_to(gate_t[:, :, None], (TOP_K, t, LANES))
    y_p, y_s = _combine(lpos_flat, seg_cnt, seg_l, seg_g, ys_a, ys_b, x1, gate_x, g2, final_gain.reshape(1, d),
                        tp, n_exp)

    return (y_p.reshape(bp, sp, d), y_s.reshape(bs, ss, d),
            new_pool_p[None], new_c_p[None], new_n_p[None], m_p.reshape(1, bp, heads),
            new_pool_s[None], new_c_s[None], new_n_s[None], m_s.reshape(1, bs, heads))
```
